```python
import math
import jax, jax.numpy as jnp
from jax import lax
import numpy as np

D_MODEL = 1024
BATCH = 8
SEQ = 2048
DEPTH = 1
DEC_BATCH = 16
DEC_SEQ = 4096
PAST_LEN = 128

HEAD_DIM = 64
H_A = 8
KV_A = 2
G_A = H_A // KV_A
H_B = 8
KV_B = 2
G_B = H_B // KV_B
D_A = H_A * HEAD_DIM
D_B = H_B * HEAD_DIM
D_MIX = D_A + D_B
IN_COLS = (H_A + 2 * KV_A + H_B + 2 * KV_B) * HEAD_DIM
WINDOW = 128
BLOCK = 128
KSPAN = BLOCK + 2 * WINDOW
NUM_BUCKETS = 32
MAX_DISTANCE = 128
GRID_W = 64
ROPE_THETA = 10000.0
AX_PAIRS = HEAD_DIM // 4
N_EXPERTS = 16
CAPACITY_FACTOR = 2
D_EXPERT = 512
D_PLE = 256
EPS = 1e-6
NEG = -1e30

kernel_name = "hymba_hybrid_encoder_ec_moe"


def rmsnorm(x, g):
    xf = x.astype(jnp.float32)
    r = lax.rsqrt(jnp.mean(xf * xf, axis=-1, keepdims=True) + EPS)
    return (xf * r).astype(x.dtype) * g


def t5_bucket(rel):
    nb = NUM_BUCKETS // 2
    ret = jnp.where(rel > 0, nb, 0)
    n = jnp.abs(rel)
    max_exact = nb // 2
    large = max_exact + (jnp.log(jnp.maximum(n, 1).astype(jnp.float32) / max_exact)
                         / math.log(MAX_DISTANCE / max_exact) * (nb - max_exact)).astype(jnp.int32)
    large = jnp.minimum(large, nb - 1)
    return ret + jnp.where(n < max_exact, n, large)


def window_sink_attention(q, k, v, sink, rel_bias):
    B, T, _ = q.shape
    q = q.reshape(B, T, KV_A, G_A, HEAD_DIM)
    k = k.reshape(B, T, KV_A, HEAD_DIM)
    v = v.reshape(B, T, KV_A, HEAD_DIM)
    pad = ((0, 0), (WINDOW, WINDOW), (0, 0), (0, 0))
    kp = jnp.pad(k, pad)
    vp = jnp.pad(v, pad)
    nb = T // BLOCK
    scale = HEAD_DIM ** -0.5
    qi = jnp.arange(BLOCK)[:, None]
    kj = jnp.arange(KSPAN)[None, :]
    rel = kj - WINDOW - qi
    band = jnp.abs(rel) <= WINDOW
    bias = rel_bias.astype(jnp.float32)[t5_bucket(rel)]
    bias = bias.transpose(2, 0, 1).reshape(KV_A, G_A, BLOCK, KSPAN)
    sink_f = sink.astype(jnp.float32).reshape(KV_A, G_A)[None, :, :, None, None]

    def one_block(i):
        start = i * BLOCK
        qb = lax.dynamic_slice_in_dim(q, start, BLOCK, axis=1)
        kb = lax.dynamic_slice_in_dim(kp, start, KSPAN, axis=1)
        vb = lax.dynamic_slice_in_dim(vp, start, KSPAN, axis=1)
        s = jnp.einsum('bqhgd,bkhd->bhgqk', qb, kb,
                       preferred_element_type=jnp.float32) * scale + bias
        kpos = start - WINDOW + jnp.arange(KSPAN)
        valid = band & ((kpos >= 0) & (kpos < T))[None, :]
        s = jnp.where(valid, s, NEG)
        m = jnp.maximum(jnp.max(s, axis=-1, keepdims=True), sink_f)
        e = jnp.exp(s - m)
        p = e / (jnp.sum(e, axis=-1, keepdims=True) + jnp.exp(sink_f - m))
        return jnp.einsum('bhgqk,bkhd->bqhgd', p.astype(vb.dtype), vb)

    o = lax.map(one_block, jnp.arange(nb))
    return o.transpose(1, 0, 2, 3, 4, 5).reshape(B, T, D_A)


def axial_rope_tables(T):
    rows = T // GRID_W
    row = jnp.repeat(jnp.arange(rows), GRID_W).astype(jnp.float32)
    col = jnp.tile(jnp.arange(GRID_W), rows).astype(jnp.float32)
    freqs = ROPE_THETA ** (-jnp.arange(AX_PAIRS, dtype=jnp.float32) / AX_PAIRS)
    ang = jnp.concatenate([row[:, None] * freqs, col[:, None] * freqs], axis=-1)
    return jnp.cos(ang), jnp.sin(ang)


def apply_rope(x, cos, sin):
    B, T, H, _ = x.shape
    xr = x.astype(jnp.float32).reshape(B, T, H, HEAD_DIM // 2, 2)
    x0, x1 = xr[..., 0], xr[..., 1]
    c = cos[None, :, None, :]
    s = sin[None, :, None, :]
    out = jnp.stack([x0 * c - x1 * s, x0 * s + x1 * c], axis=-1)
    return out.reshape(B, T, H, HEAD_DIM).astype(x.dtype)


def axial_qknorm_attention(q, k, v, gq, gk):
    B, T, _ = q.shape
    q = rmsnorm(q.reshape(B, T, H_B, HEAD_DIM), gq)
    k = rmsnorm(k.reshape(B, T, KV_B, HEAD_DIM), gk)
    v = v.reshape(B, T, KV_B, HEAD_DIM)
    cos, sin = axial_rope_tables(T)
    q = apply_rope(q, cos, sin).reshape(B, T, KV_B, G_B, HEAD_DIM)
    k = apply_rope(k, cos, sin)
    scale = HEAD_DIM ** -0.5
    nb = T // BLOCK

    def one_block(i):
        qb = lax.dynamic_slice_in_dim(q, i * BLOCK, BLOCK, axis=1)
        s = jnp.einsum('bqhgd,bkhd->bhgqk', qb, k,
                       preferred_element_type=jnp.float32) * scale
        p = jax.nn.softmax(s, axis=-1)
        return jnp.einsum('bhgqk,bkhd->bqhgd', p.astype(v.dtype), v)

    o = lax.map(one_block, jnp.arange(nb))
    return o.transpose(1, 0, 2, 3, 4, 5).reshape(B, T, D_B)


def expert_choice_moe(h, w_router, w_gate, w_up, w_down):
    B, T, D = h.shape
    N = B * T
    cap = CAPACITY_FACTOR * N // N_EXPERTS
    hf = h.reshape(N, D)
    logits = jnp.einsum('nd,de->ne', hf, w_router, preferred_element_type=jnp.float32)
    aff = jax.nn.softmax(logits, axis=-1)
    gates, idx = lax.top_k(aff.T, cap)
    xe = hf[idx]
    a = jnp.einsum('ecd,edf->ecf', xe, w_gate)
    u = jnp.einsum('ecd,edf->ecf', xe, w_up)
    ye = jnp.einsum('ecf,efd->ecd', jax.nn.silu(a) * u, w_down)
    ye = gates[..., None].astype(ye.dtype) * ye
    out = jnp.zeros_like(hf).at[idx].add(ye)
    return out.reshape(B, T, D)


def run_trunk(x, p, g_attn, w_in, sink_a, rel_bias, gq_b, gk_b, g_out_a, g_out_b, w_out,
              g_ffn, w_router, w_gate, w_up, w_down, g_ple, w_ple_gate, w_ple_proj, g_final):
    splits = np.cumsum([D_A, KV_A * HEAD_DIM, KV_A * HEAD_DIM, D_B, KV_B * HEAD_DIM])
    for i in range(DEPTH):
        h = rmsnorm(x, g_attn[i])
        proj = h @ w_in[i]
        qa, ka, va, qb, kb, vb = jnp.split(proj, splits, axis=-1)
        oa = window_sink_attention(qa, ka, va, sink_a[i], rel_bias)
        ob = axial_qknorm_attention(qb, kb, vb, gq_b[i], gk_b[i])
        mixed = jnp.concatenate([rmsnorm(oa, g_out_a[i]), rmsnorm(ob, g_out_b[i])], axis=-1)
        x = x + mixed @ w_out[i]
        x = x + expert_choice_moe(rmsnorm(x, g_ffn[i]), w_router[i], w_gate[i], w_up[i], w_down[i])
        gate = jax.nn.sigmoid(rmsnorm(x, g_ple[i]) @ w_ple_gate[i])
        x = x + gate * (p[i] @ w_ple_proj[i])
    return rmsnorm(x, g_final)


def setup_inputs(seed: int = 0) -> dict:
    key = jax.random.key(seed)
    ks = jax.random.split(key, 24)
    f = jnp.float32

    def nrm(k, shape, scale):
        return jax.random.normal(k, shape, f) * scale

    def gain(k, shape):
        return 1.0 + 0.02 * jax.random.normal(k, shape, f)

    return {
        "x_prompt": nrm(ks[0], (BATCH, SEQ, D_MODEL), 1.0),
        "x_sample": nrm(ks[1], (DEC_BATCH, DEC_SEQ, D_MODEL), 1.0),
        "p_prompt": nrm(ks[2], (DEPTH, BATCH, SEQ, D_PLE), 1.0),
        "p_sample": nrm(ks[3], (DEPTH, DEC_BATCH, DEC_SEQ, D_PLE), 1.0),
        "g_attn": gain(ks[4], (DEPTH, D_MODEL)),
        "w_in": nrm(ks[5], (DEPTH, D_MODEL, IN_COLS), D_MODEL ** -0.5),
        "sink_a": nrm(ks[6], (DEPTH, H_A), 0.5),
        "rel_bias": nrm(ks[7], (NUM_BUCKETS, H_A), 0.5),
        "gq_b": gain(ks[8], (DEPTH, HEAD_DIM)),
        "gk_b": gain(ks[9], (DEPTH, HEAD_DIM)),
        "g_out_a": gain(ks[10], (DEPTH, D_A)),
        "g_out_b": gain(ks[11], (DEPTH, D_B)),
        "w_out": nrm(ks[12], (DEPTH, D_MIX, D_MODEL), D_MIX ** -0.5),
        "g_ffn": gain(ks[13], (DEPTH, D_MODEL)),
        "w_router": nrm(ks[14], (DEPTH, D_MODEL, N_EXPERTS), D_MODEL ** -0.5),
        "w_gate": nrm(ks[15], (DEPTH, N_EXPERTS, D_MODEL, D_EXPERT), D_MODEL ** -0.5),
        "w_up": nrm(ks[16], (DEPTH, N_EXPERTS, D_MODEL, D_EXPERT), D_MODEL ** -0.5),
        "w_down": nrm(ks[17], (DEPTH, N_EXPERTS, D_EXPERT, D_MODEL), D_EXPERT ** -0.5),
        "g_ple": gain(ks[18], (DEPTH, D_MODEL)),
        "w_ple_gate": nrm(ks[19], (DEPTH, D_MODEL, D_MODEL), D_MODEL ** -0.5),
        "w_ple_proj": nrm(ks[20], (DEPTH, D_PLE, D_MODEL), D_PLE ** -0.5),
        "g_final": gain(ks[21], (D_MODEL,)),
    }


def reference(x_prompt, x_sample, p_prompt, p_sample, g_attn, w_in, sink_a, rel_bias, gq_b, gk_b,
              g_out_a, g_out_b, w_out, g_ffn, w_router, w_gate, w_up, w_down, g_ple, w_ple_gate,
              w_ple_proj, g_final):
    y_prompt = run_trunk(x_prompt, p_prompt, g_attn, w_in, sink_a, rel_bias, gq_b, gk_b, g_out_a,
                         g_out_b, w_out, g_ffn, w_router, w_gate, w_up, w_down, g_ple, w_ple_gate,
                         w_ple_proj, g_final)
    y_sample = run_trunk(x_sample, p_sample, g_attn, w_in, sink_a, rel_bias, gq_b, gk_b, g_out_a,
                         g_out_b, w_out, g_ffn, w_router, w_gate, w_up, w_down, g_ple, w_ple_gate,
                         w_ple_proj, g_final)
    return (y_prompt, y_sample)
```

```python
import functools
import math

import jax
import jax.numpy as jnp
import numpy as np
from jax import lax
from jax.experimental import pallas as pl
from jax.experimental.pallas import tpu as pltpu

D_MODEL = 1024
HEAD_DIM = 64
N_HEADS = 8
N_KV = 2
GROUP = N_HEADS // N_KV
D_MIX = N_HEADS * HEAD_DIM
D_KV = N_KV * HEAD_DIM
D_GRP = GROUP * HEAD_DIM
WINDOW = 128
NUM_BUCKETS = 32
MAX_DISTANCE = 128
GRID_W = 64
ROPE_THETA = 10000.0
AX_PAIRS = HEAD_DIM // 4
N_EXPERTS = 16
CAPACITY_FACTOR = 2
D_EXPERT = 512
D_PLE = 256
EPS = 1e-6
NEG = -1e30

LANES = 128
SUBLANES = 8
ROW_SLABS = D_MODEL // LANES

F32 = jnp.float32
BF16 = jnp.bfloat16

_NT = (((1,), (1,)), ((), ()))
_TN = (((0,), (0,)), ((), ()))


def _vmem_limit(mib):
    return pltpu.CompilerParams(vmem_limit_bytes=mib * 1024 * 1024)


def _cparams(mib, n_axes):
    return pltpu.CompilerParams(vmem_limit_bytes=mib * 1024 * 1024,
                                dimension_semantics=("arbitrary",) * n_axes)


def _const_spec(shape):
    zeros = (0,) * len(shape)
    return pl.BlockSpec(shape, lambda *_: zeros)


def _rms(x, g):
    r = lax.rsqrt(jnp.mean(x * x, axis=-1, keepdims=True) + EPS)
    return (x * r) * g


def _split_bf16(x):
    hi = x.astype(BF16)
    lo = (x - hi.astype(F32)).astype(BF16)
    return hi, lo


def _qkv_kernel(x_ref, g_ref, w1_ref, w2t_ref, bd_ref, gq_ref, gqs_ref, gk_ref, gks_ref,
                cq_ref, sq_ref, ck_ref, sk_ref,
                qa_ref, qb_ref, va_ref, vb_ref, kat_ref, kbt_ref):
    tn = x_ref.shape[1]
    h = _rms(x_ref[0], g_ref[...]).astype(BF16)
    p1 = jnp.dot(h, w1_ref[...], preferred_element_type=F32)
    p2 = lax.dot_general(w2t_ref[...], h, _NT, preferred_element_type=F32)

    qa_ref[0] = p1[:, 0:D_MIX].astype(BF16)
    va_ref[0] = p1[:, 3 * D_MIX:3 * D_MIX + D_KV].astype(BF16)
    vb_ref[0] = p1[:, 3 * D_MIX + D_KV:3 * D_MIX + 2 * D_KV].astype(BF16)
    kat_ref[0] = p2[0:D_KV].astype(BF16)

    q_raw = p1[:, D_MIX:2 * D_MIX]
    q_swp = p1[:, 2 * D_MIX:3 * D_MIX]
    hi, lo = _split_bf16(q_raw * q_raw)
    ssq = (jnp.dot(hi, bd_ref[...], preferred_element_type=F32)
           + jnp.dot(lo, bd_ref[...], preferred_element_type=F32))
    rq = lax.rsqrt(ssq * (1.0 / HEAD_DIM) + EPS)
    cq = jnp.concatenate([cq_ref[...]] * (D_MIX // LANES), axis=1)
    sq = jnp.concatenate([sq_ref[...]] * (D_MIX // LANES), axis=1)
    qn = (q_raw * rq) * gq_ref[...]
    qs = (q_swp * rq) * gqs_ref[...]
    qb_ref[0] = (qn * cq + qs * sq).astype(BF16)

    k_raw = p2[D_KV:2 * D_KV].reshape(N_KV, HEAD_DIM, tn)
    k_swp = p2[2 * D_KV:3 * D_KV].reshape(N_KV, HEAD_DIM, tn)
    rk = lax.rsqrt(jnp.sum(k_raw * k_raw, axis=1, keepdims=True) * (1.0 / HEAD_DIM) + EPS)
    kn = (k_raw * rk).reshape(D_KV, tn) * gk_ref[...]
    ks = (k_swp * rk).reshape(D_KV, tn) * gks_ref[...]
    kbt_ref[0] = (kn * ck_ref[...] + ks * sk_ref[...]).astype(BF16)


def _qkv(x, g_attn, w1, w2t, bd, gq, gqs, gk, gks, cq, sq, ck, sk, *, tn):
    B, T, _ = x.shape
    tn = min(tn, T)
    n1 = w1.shape[1]
    grid = (B, T // tn)
    tok = lambda w: pl.BlockSpec((1, tn, w), lambda b, j: (b, j, 0))
    keyt = pl.BlockSpec((1, D_KV, tn), lambda b, j: (b, 0, j))
    return pl.pallas_call(
        _qkv_kernel,
        grid=grid,
        in_specs=[tok(D_MODEL), _const_spec((1, D_MODEL)), _const_spec((D_MODEL, n1)),
                  _const_spec((3 * D_KV, D_MODEL)), _const_spec((D_MIX, D_MIX)),
                  _const_spec((1, D_MIX)), _const_spec((1, D_MIX)),
                  _const_spec((D_KV, 1)), _const_spec((D_KV, 1)),
                  pl.BlockSpec((tn, LANES), lambda b, j: (j, 0)),
                  pl.BlockSpec((tn, LANES), lambda b, j: (j, 0)),
                  pl.BlockSpec((D_KV, tn), lambda b, j: (0, j)),
                  pl.BlockSpec((D_KV, tn), lambda b, j: (0, j))],
        out_specs=[tok(D_MIX), tok(D_MIX), tok(D_KV), tok(D_KV), keyt, keyt],
        out_shape=[jax.ShapeDtypeStruct((B, T, D_MIX), BF16), jax.ShapeDtypeStruct((B, T, D_MIX), BF16),
                   jax.ShapeDtypeStruct((B, T, D_KV), BF16), jax.ShapeDtypeStruct((B, T, D_KV), BF16),
                   jax.ShapeDtypeStruct((B, D_KV, T), BF16), jax.ShapeDtypeStruct((B, D_KV, T), BF16)],
        compiler_params=_cparams(48, 2),
        name="qkv",
    )(x, g_attn, w1, w2t, bd, gq, gqs, gk, gks, cq, sq, ck, sk)


def _first_step():
    return (pl.program_id(0) == 0) & (pl.program_id(1) == 0) & (pl.program_id(2) == 0)


def _place_kv(kt_ref, v_ref, ktp_ref, vp_ref, key_off, T):
    kvh = pl.program_id(1)
    for g in range(GROUP):
        ktp_ref[g, g * HEAD_DIM:(g + 1) * HEAD_DIM, key_off:key_off + T] = kt_ref[0]
    for h in range(N_KV):
        @pl.when(kvh == h)
        def _():
            vv = v_ref[0][:, h * HEAD_DIM:(h + 1) * HEAD_DIM]
            for g in range(GROUP):
                vp_ref[g, key_off:key_off + T, g * HEAD_DIM:(g + 1) * HEAD_DIM] = vv


def _lane_group_select(cols):
    tq = cols[0].shape[0]
    grp = lax.broadcasted_iota(jnp.int32, (tq, D_GRP), 1) // HEAD_DIM
    out = jnp.broadcast_to(cols[GROUP - 1], (tq, D_GRP))
    for g in range(GROUP - 2, -1, -1):
        out = jnp.where(grp == g, cols[g], out)
    return out


def _attn_b_kernel(q_ref, kt_ref, v_ref, o_ref, ktp_ref, vp_ref, acc_ref, m_ref, l_ref, *, kc):
    T = kt_ref.shape[2]
    tq = q_ref.shape[1]

    @pl.when(_first_step())
    def _():
        ktp_ref[...] = jnp.zeros(ktp_ref.shape, BF16)
        vp_ref[...] = jnp.zeros(vp_ref.shape, BF16)

    @pl.when(pl.program_id(2) == 0)
    def _():
        _place_kv(kt_ref, v_ref, ktp_ref, vp_ref, 0, T)

    q = q_ref[0]
    acc_ref[...] = jnp.zeros(acc_ref.shape, F32)
    m_ref[...] = jnp.full(m_ref.shape, NEG, F32)
    l_ref[...] = jnp.zeros(l_ref.shape, F32)

    def chunk(c, carry):
        k0 = pl.multiple_of(c * kc, kc)
        pv = jnp.zeros((tq, D_GRP), F32)
        alphas = []
        for g in range(GROUP):
            s = jnp.dot(q, ktp_ref[g, :, pl.ds(k0, kc)], preferred_element_type=F32)
            m_old = m_ref[g]
            m_new = jnp.maximum(m_old, jnp.max(s, axis=1, keepdims=True))
            p = jnp.exp(s - m_new)
            alpha = jnp.exp(m_old - m_new)
            l_ref[g] = alpha * l_ref[g] + jnp.sum(p, axis=1, keepdims=True)
            m_ref[g] = m_new
            pv = pv + jnp.dot(p.astype(BF16), vp_ref[g, pl.ds(k0, kc), :], preferred_element_type=F32)
            alphas.append(alpha)
        acc_ref[...] = acc_ref[...] * _lane_group_select(alphas) + pv
        return carry

    lax.fori_loop(0, T // kc, chunk, 0)
    inv = [1.0 / l_ref[g] for g in range(GROUP)]
    o_ref[0] = acc_ref[...] * _lane_group_select(inv)


def _attn_b(q, kt, v, *, tq, kc):
    B, T, _ = q.shape
    tq = min(tq, T)
    kc = min(kc, T)
    grid = (B, N_KV, T // tq)
    return pl.pallas_call(
        functools.partial(_attn_b_kernel, kc=kc),
        grid=grid,
        in_specs=[pl.BlockSpec((1, tq, D_GRP), lambda b, h, i: (b, i, h)),
                  pl.BlockSpec((1, HEAD_DIM, T), lambda b, h, i: (b, h, 0)),
                  pl.BlockSpec((1, T, D_KV), lambda b, h, i: (b, 0, 0))],
        out_specs=pl.BlockSpec((1, tq, D_GRP), lambda b, h, i: (b, i, h)),
        out_shape=jax.ShapeDtypeStruct((B, T, D_MIX), F32),
        scratch_shapes=[pltpu.VMEM((GROUP, D_GRP, T), BF16), pltpu.VMEM((GROUP, T, D_GRP), BF16),
                        pltpu.VMEM((tq, D_GRP), F32), pltpu.VMEM((GROUP, tq, 1), F32),
                        pltpu.VMEM((GROUP, tq, 1), F32)],
        compiler_params=_cparams(56, 3),
        name="attn_b",
    )(q, kt, v)


def _attn_a_kernel(sink_ref, rb_ref, bucket_ref, q_ref, kt_ref, v_ref, o_ref,
                   ktp_ref, vp_ref, bias_ref):
    T = kt_ref.shape[2]
    tq = q_ref.shape[1]
    kw = tq + 2 * WINDOW
    kvh = pl.program_id(1)
    qi = pl.program_id(2)

    @pl.when(_first_step())
    def _():
        ktp_ref[...] = jnp.zeros(ktp_ref.shape, BF16)
        vp_ref[...] = jnp.zeros(vp_ref.shape, BF16)
        bucket = bucket_ref[...]
        for h in range(N_HEADS):
            bias = jnp.full((tq, kw), NEG, F32)
            for k in range(NUM_BUCKETS):
                bias = jnp.where(bucket == k, rb_ref[k, h], bias)
            bias_ref[h] = bias

    @pl.when(qi == 0)
    def _():
        _place_kv(kt_ref, v_ref, ktp_ref, vp_ref, WINDOW, T)

    q = q_ref[0]
    start = pl.multiple_of(qi * tq, tq)
    kpos = start - WINDOW + lax.broadcasted_iota(jnp.int32, (1, kw), 1)
    colmask = jnp.where((kpos >= 0) & (kpos < T), 0.0, NEG).astype(F32)
    pv = jnp.zeros((tq, D_GRP), F32)
    inv = []
    for g in range(GROUP):
        head = kvh * GROUP + g
        s = jnp.dot(q, ktp_ref[g, :, pl.ds(start, kw)], preferred_element_type=F32)
        s = s + bias_ref[head] + colmask
        sink = sink_ref[head]
        m = jnp.maximum(jnp.max(s, axis=1, keepdims=True), sink)
        e = jnp.exp(s - m)
        den = jnp.sum(e, axis=1, keepdims=True) + jnp.exp(sink - m)
        pv = pv + jnp.dot(e.astype(BF16), vp_ref[g, pl.ds(start, kw), :], preferred_element_type=F32)
        inv.append(1.0 / den)
    o_ref[0] = pv * _lane_group_select(inv)


def _attn_a(q, kt, v, sink, rel_bias, bucket, *, tq):
    B, T, _ = q.shape
    tq = bucket.shape[0]
    kw = tq + 2 * WINDOW
    grid = (B, N_KV, T // tq)
    smem = pl.BlockSpec(memory_space=pltpu.SMEM)
    return pl.pallas_call(
        _attn_a_kernel,
        grid=grid,
        in_specs=[smem, smem, _const_spec((tq, kw)),
                  pl.BlockSpec((1, tq, D_GRP), lambda b, h, i: (b, i, h)),
                  pl.BlockSpec((1, HEAD_DIM, T), lambda b, h, i: (b, h, 0)),
                  pl.BlockSpec((1, T, D_KV), lambda b, h, i: (b, 0, 0))],
        out_specs=pl.BlockSpec((1, tq, D_GRP), lambda b, h, i: (b, i, h)),
        out_shape=jax.ShapeDtypeStruct((B, T, D_MIX), F32),
        scratch_shapes=[pltpu.VMEM((GROUP, D_GRP, T + 2 * WINDOW), BF16),
                        pltpu.VMEM((GROUP, T + 2 * WINDOW, D_GRP), BF16),
                        pltpu.VMEM((N_HEADS, tq, kw), F32)],
        compiler_params=_cparams(56, 3),
        name="attn_a",
    )(sink, rel_bias, bucket, q, kt, v)


def _post_kernel(oa_ref, ob_ref, x_ref, ga_ref, gb_ref, wo_ref, gf_ref, wrh_ref, wrl_ref,
                 x1_ref, h2s_ref, lg_ref):
    tn = x_ref.shape[0]
    mixed = jnp.concatenate([_rms(oa_ref[...], ga_ref[...]), _rms(ob_ref[...], gb_ref[...])], axis=1)
    x1 = x_ref[...] + jnp.dot(mixed.astype(BF16), wo_ref[...], preferred_element_type=F32)
    x1_ref[...] = x1
    h2 = _rms(x1, gf_ref[...])
    for s in range(ROW_SLABS):
        h2s_ref[pl.ds(s, tn, stride=ROW_SLABS), :] = h2[:, s * LANES:(s + 1) * LANES]
    hi, lo = _split_bf16(h2)
    lg_ref[...] = (jnp.dot(hi, wrh_ref[...], preferred_element_type=F32)
                   + jnp.dot(lo, wrh_ref[...], preferred_element_type=F32)
                   + jnp.dot(hi, wrl_ref[...], preferred_element_type=F32))


def _post(oa, ob, x, ga, gb, wo, gf, wrh, wrl, *, tn):
    N = x.shape[0]
    tn = min(tn, N)
    tok = lambda w: pl.BlockSpec((tn, w), lambda j: (j, 0))
    return pl.pallas_call(
        _post_kernel,
        grid=(N // tn,),
        in_specs=[tok(D_MIX), tok(D_MIX), tok(D_MODEL), _const_spec((1, D_MIX)), _const_spec((1, D_MIX)),
                  _const_spec((D_MODEL, D_MODEL)), _const_spec((1, D_MODEL)),
                  _const_spec((D_MODEL, LANES)), _const_spec((D_MODEL, LANES))],
        out_specs=[tok(D_MODEL), pl.BlockSpec((ROW_SLABS * tn, LANES), lambda j: (j, 0)), tok(LANES)],
        out_shape=[jax.ShapeDtypeStruct((N, D_MODEL), F32),
                   jax.ShapeDtypeStruct((ROW_SLABS * N, LANES), F32),
                   jax.ShapeDtypeStruct((N, LANES), F32)],
        compiler_params=_cparams(48, 1),
        name="post",
    )(oa, ob, x, ga, gb, wo, gf, wrh, wrl)


def _select_kernel(lg_ref, idx_ref, gate_ref, pos_ref, bex_ref, *, cap, sc):
    e = pl.program_id(0)
    nb = lg_ref.shape[1]
    lg = lg_ref[...]
    mx = jnp.max(lg, axis=0)
    den = jnp.sum(jnp.exp(lg - mx), axis=0)
    aff = jnp.exp(lg_ref[e] - mx) / den

    def bisect(_, lohi):
        lo, hi = lohi
        mid = lo + (hi - lo) // 2
        thr = lax.bitcast_convert_type(mid, F32)
        cnt = jnp.sum((aff >= thr).astype(jnp.int32))
        ok = cnt >= cap
        return jnp.where(ok, mid, lo), jnp.where(ok, hi, mid)

    lo0 = jnp.zeros((1, 1), jnp.int32)
    hi0 = jnp.full((1, 1), 0x7F800000, jnp.int32)
    lo, hi = lax.fori_loop(0, 31, bisect, (lo0, hi0))
    above = aff >= lax.bitcast_convert_type(hi, F32)
    tie = (aff >= lax.bitcast_convert_type(lo, F32)) & jnp.logical_not(above)
    need = cap - jnp.sum(above.astype(jnp.int32))

    li = lax.broadcasted_iota(jnp.int32, (LANES, LANES), 0)
    lj = lax.broadcasted_iota(jnp.int32, (LANES, LANES), 1)
    ut_incl = (li <= lj).astype(BF16)
    ut_excl = (li < lj).astype(BF16)
    bi = lax.broadcasted_iota(jnp.int32, (nb, nb), 0)
    bj = lax.broadcasted_iota(jnp.int32, (nb, nb), 1)
    lt_incl = (bj <= bi).astype(BF16)
    lt_excl = (bj < bi).astype(BF16)

    def last(x):
        return x[:, LANES - 1:LANES]

    tie_b = tie.astype(BF16)
    tie_excl = jnp.dot(tie_b, ut_excl, preferred_element_type=F32)
    tie_incl = jnp.dot(tie_b, ut_incl, preferred_element_type=F32)
    tie_before = last(jnp.dot(lt_excl, tie_incl.astype(BF16), preferred_element_type=F32))
    tie_rank = (tie_before + tie_excl).astype(jnp.int32)
    sel = above | (tie & (tie_rank < need))

    lcum = jnp.dot(sel.astype(BF16), ut_incl, preferred_element_type=F32)
    lcum_b = lcum.astype(BF16)
    bcum_incl = last(jnp.dot(lt_incl, lcum_b, preferred_element_type=F32))
    bcum_excl = bcum_incl - last(lcum)
    pos_ref[0] = jnp.where(sel, (bcum_excl + lcum).astype(jnp.int32) - 1, -1)
    bex_ref[0] = bcum_excl.astype(jnp.int32)

    a1 = aff.astype(BF16)
    a2 = (aff - a1.astype(F32)).astype(BF16)
    a3 = (aff - a1.astype(F32) - a2.astype(F32)).astype(BF16)
    blk = lax.broadcasted_iota(jnp.int32, (nb, sc), 0)
    row = lax.broadcasted_iota(jnp.int32, (LANES, sc), 0)
    for c in range(cap // sc):
        slot = (c * sc + lax.broadcasted_iota(jnp.int32, (1, sc), 1)).astype(F32)
        hit = (bcum_excl <= slot) & (slot < bcum_incl)
        hit_b = hit.astype(BF16)
        base = jnp.sum(jnp.where(hit, bcum_excl, 0.0), axis=0, keepdims=True)
        blk_of = jnp.sum(jnp.where(hit, blk, 0), axis=0, keepdims=True)
        lc_t = lax.dot_general(lcum_b, hit_b, _TN, preferred_element_type=F32)
        k_in = slot - base
        j_of = jnp.sum((lc_t <= k_in).astype(jnp.int32), axis=0, keepdims=True)
        idx_ref[0, :, c * sc:(c + 1) * sc] = blk_of * LANES + j_of
        aff_t = (lax.dot_general(a1, hit_b, _TN, preferred_element_type=F32)
                 + lax.dot_general(a2, hit_b, _TN, preferred_element_type=F32)
                 + lax.dot_general(a3, hit_b, _TN, preferred_element_type=F32))
        gate_ref[0, :, c * sc:(c + 1) * sc] = jnp.sum(jnp.where(row == j_of, aff_t, 0.0), axis=0, keepdims=True)


def _select(lg3, *, cap, sc):
    E, nb, _ = lg3.shape
    sc = min(sc, cap)
    per_e = lambda r, c, dt: (pl.BlockSpec((1, r, c), lambda e: (e, 0, 0)), jax.ShapeDtypeStruct((E, r, c), dt))
    specs = [per_e(1, cap, jnp.int32), per_e(1, cap, F32), per_e(nb, LANES, jnp.int32), per_e(nb, 1, jnp.int32)]
    return pl.pallas_call(
        functools.partial(_select_kernel, cap=cap, sc=sc),
        grid=(E,),
        in_specs=[_const_spec((E, nb, LANES))],
        out_specs=[s for s, _ in specs],
        out_shape=[o for _, o in specs],
        compiler_params=_cparams(48, 1),
        name="select",
    )(lg3)


def _ffn_kernel(idx_ref, idxn_ref, gate_ref, h2s_ref, wg_ref, wu_ref, wd_ref, ye_ref, xbuf_ref, sem_ref):
    rows = ye_ref.shape[0]
    n = pl.program_id(0)
    n_steps = pl.num_programs(0)
    slot = n % 2

    def row_copy(ids_ref, r, to_slot):
        src = pl.multiple_of(ids_ref[0, 0, r] * ROW_SLABS, ROW_SLABS)
        return pltpu.make_async_copy(h2s_ref.at[pl.ds(src, ROW_SLABS), :],
                                     xbuf_ref.at[to_slot, pl.ds(r * ROW_SLABS, ROW_SLABS), :],
                                     sem_ref.at[to_slot])

    def start_gather(ids_ref, to_slot):
        def body(r, carry):
            row_copy(ids_ref, r, to_slot).start()
            return carry
        lax.fori_loop(0, rows, body, 0)

    @pl.when(n == 0)
    def _():
        start_gather(idx_ref, 0)

    @pl.when(n + 1 < n_steps)
    def _():
        start_gather(idxn_ref, 1 - slot)

    pltpu.make_async_copy(h2s_ref.at[pl.ds(0, rows * ROW_SLABS), :], xbuf_ref.at[slot], sem_ref.at[slot]).wait()

    x = jnp.concatenate(
        [xbuf_ref[slot, pl.ds(s, rows, stride=ROW_SLABS), :].astype(BF16) for s in range(ROW_SLABS)], axis=1)
    a = jnp.dot(x, wg_ref[0], preferred_element_type=F32)
    u = jnp.dot(x, wu_ref[0], preferred_element_type=F32)
    hmid = (jax.nn.silu(a) * u).astype(BF16)
    y = jnp.dot(hmid, wd_ref[0], preferred_element_type=F32)
    ri = lax.broadcasted_iota(jnp.int32, (rows, rows), 0)
    ci = lax.broadcasted_iota(jnp.int32, (rows, rows), 1)
    gate_col = jnp.sum(jnp.where(ri == ci, gate_ref[0], 0.0), axis=1, keepdims=True)
    ye_ref[...] = (gate_col * y).astype(BF16)


def _ffn(idx3, gate3, h2s, wg, wu, wd, *, cap):
    n_steps, _, rows = idx3.shape
    per_e = cap // rows
    last = n_steps - 1
    return pl.pallas_call(
        _ffn_kernel,
        grid=(n_steps,),
        in_specs=[pl.BlockSpec((1, 1, rows), lambda n: (n, 0, 0), memory_space=pltpu.SMEM),
                  pl.BlockSpec((1, 1, rows), lambda n: (jnp.minimum(n + 1, last), 0, 0), memory_space=pltpu.SMEM),
                  pl.BlockSpec((1, 1, rows), lambda n: (n, 0, 0)),
                  pl.BlockSpec(memory_space=pl.ANY),
                  pl.BlockSpec((1, D_MODEL, D_EXPERT), lambda n: (n // per_e, 0, 0)),
                  pl.BlockSpec((1, D_MODEL, D_EXPERT), lambda n: (n // per_e, 0, 0)),
                  pl.BlockSpec((1, D_EXPERT, D_MODEL), lambda n: (n // per_e, 0, 0))],
        out_specs=pl.BlockSpec((rows, D_MODEL), lambda n: (n, 0)),
        out_shape=jax.ShapeDtypeStruct((n_steps * rows, D_MODEL), BF16),
        scratch_shapes=[pltpu.VMEM((2, rows * ROW_SLABS, LANES), F32), pltpu.SemaphoreType.DMA((2,))],
        compiler_params=_cparams(48, 1),
        name="ffn",
    )(idx3, idx3, gate3, h2s, wg, wu, wd)


def _combine_kernel(tab_ref, x1_ref, pos_ref, p_ref, ye_ref, wpg_ref, wpp_ref, gp_ref, gfin_ref,
                    y_ref, win_ref, sem_ref, acc_ref, *, cap, win):
    tt = x1_ref.shape[0]
    i = pl.program_id(0)
    total = N_EXPERTS * cap
    wi = lax.broadcasted_iota(jnp.int32, (win, tt), 0)

    def window(e):
        first = e * cap + tab_ref[e, i]
        end = e * cap + tab_ref[e, i + 1]
        a0 = jnp.minimum((first // 16) * 16, total - win)
        n_chunks = jnp.maximum((end - a0 + win - 1) // win, 1)
        return a0, n_chunks

    def copy(a0, to_slot):
        a0 = pl.multiple_of(a0, 16)
        return pltpu.make_async_copy(ye_ref.at[pl.ds(a0, win), :], win_ref.at[to_slot], sem_ref.at[to_slot])

    def add_chunk(e, a0, from_slot):
        onehot = ((pos_ref[e:e + 1, :] + (e * cap - a0)) == wi) & (pos_ref[e:e + 1, :] >= 0)
        acc_ref[...] += lax.dot_general(onehot.astype(BF16), win_ref[from_slot], _TN,
                                        preferred_element_type=F32)

    acc_ref[...] = jnp.zeros(acc_ref.shape, F32)
    a0_first, _ = window(0)
    copy(a0_first, 0).start()
    for e in range(N_EXPERTS):
        slot = e % 2
        a0, n_chunks = window(e)
        if e + 1 < N_EXPERTS:
            a0_next, _ = window(e + 1)
            copy(a0_next, 1 - slot).start()
        copy(a0, slot).wait()
        add_chunk(e, a0, slot)

        if e + 1 < N_EXPERTS:
            def extra(c, carry):
                a = a0 + c * win
                a = jnp.minimum(a, total - win)
                copy(a, slot).start()
                copy(a, slot).wait()
                add_chunk(e, a, slot)
                return carry
            lax.fori_loop(1, n_chunks, extra, 0)
        else:
            def extra_last(c, carry):
                a = jnp.minimum(a0 + c * win, total - win)
                copy(a, slot).start()
                copy(a, slot).wait()
                add_chunk(e, a, slot)
                return carry
            lax.fori_loop(1, n_chunks, extra_last, 0)

    x2 = x1_ref[...] + acc_ref[...]
    gate = jax.nn.sigmoid(jnp.dot(_rms(x2, gp_ref[...]).astype(BF16), wpg_ref[...], preferred_element_type=F32))
    x3 = x2 + gate * jnp.dot(p_ref[...].astype(BF16), wpp_ref[...], preferred_element_type=F32)
    y_ref[...] = _rms(x3, gfin_ref[...])


def _combine(tab, x1, pos, p, ye, wpg, wpp, gp, gfin, *, cap, tt, win):
    N = x1.shape[0]
    grid_spec = pltpu.PrefetchScalarGridSpec(
        num_scalar_prefetch=1,
        grid=(N // tt,),
        in_specs=[pl.BlockSpec((tt, D_MODEL), lambda i, t: (i, 0)),
                  pl.BlockSpec((N_EXPERTS, tt), lambda i, t: (0, i)),
                  pl.BlockSpec((tt, D_PLE), lambda i, t: (i, 0)),
                  pl.BlockSpec(memory_space=pl.ANY),
                  pl.BlockSpec((D_MODEL, D_MODEL), lambda i, t: (0, 0)),
                  pl.BlockSpec((D_PLE, D_MODEL), lambda i, t: (0, 0)),
                  pl.BlockSpec((1, D_MODEL), lambda i, t: (0, 0)),
                  pl.BlockSpec((1, D_MODEL), lambda i, t: (0, 0))],
        out_specs=pl.BlockSpec((tt, D_MODEL), lambda i, t: (i, 0)),
        scratch_shapes=[pltpu.VMEM((2, win, D_MODEL), BF16), pltpu.SemaphoreType.DMA((2,)),
                        pltpu.VMEM((tt, D_MODEL), F32)],
    )
    return pl.pallas_call(
        functools.partial(_combine_kernel, cap=cap, win=win),
        grid_spec=grid_spec,
        out_shape=jax.ShapeDtypeStruct((N, D_MODEL), F32),
        compiler_params=_cparams(48, 1),
        name="combine",
    )(tab, x1, pos, p, ye, wpg, wpp, gp, gfin)


def _t5_bucket(rel):
    nb = NUM_BUCKETS // 2
    ret = jnp.where(rel > 0, nb, 0)
    n = jnp.abs(rel)
    max_exact = nb // 2
    large = max_exact + (jnp.log(jnp.maximum(n, 1).astype(F32) / max_exact)
                         / math.log(MAX_DISTANCE / max_exact) * (nb - max_exact)).astype(jnp.int32)
    large = jnp.minimum(large, nb - 1)
    return ret + jnp.where(n < max_exact, n, large)


def _bucket_table(tq):
    kw = tq + 2 * WINDOW
    rel = jnp.arange(kw)[None, :] - WINDOW - jnp.arange(tq)[:, None]
    return jnp.where(jnp.abs(rel) <= WINDOW, _t5_bucket(rel), -1).astype(jnp.int32)


def _rope_tables(T):
    rows = T // GRID_W
    row = jnp.repeat(jnp.arange(rows), GRID_W).astype(F32)
    col = jnp.tile(jnp.arange(GRID_W), rows).astype(F32)
    freqs = ROPE_THETA ** (-jnp.arange(AX_PAIRS, dtype=F32) / AX_PAIRS)
    ang = jnp.concatenate([row[:, None] * freqs, col[:, None] * freqs], axis=-1)
    cos = jnp.repeat(jnp.cos(ang), 2, axis=1)
    sin = jnp.repeat(jnp.sin(ang), 2, axis=1) * jnp.tile(jnp.array([-1.0, 1.0], F32), HEAD_DIM // 2)
    reps = LANES // HEAD_DIM
    return (jnp.tile(cos, (1, reps)), jnp.tile(sin, (1, reps)),
            jnp.tile(cos.T, (N_KV, 1)), jnp.tile(sin.T, (N_KV, 1)))


def _prep_params(g_attn, w_in, gq_b, gk_b, w_out, w_router, w_gate, w_up, w_down, w_ple_gate, w_ple_proj):
    w = w_in[0]
    o = np.cumsum([0, D_MIX, D_KV, D_KV, D_MIX, D_KV, D_KV])
    wqa, wka, wva, wqb, wkb, wvb = (w[:, o[i]:o[i + 1]] for i in range(6))
    swap_q = jnp.arange(D_MIX) ^ 1
    swap_k = jnp.arange(D_KV) ^ 1
    swap_h = jnp.arange(HEAD_DIM) ^ 1
    scale = HEAD_DIM ** -0.5
    w1 = jnp.concatenate([wqa * scale, wqb, wqb[:, swap_q], wva, wvb], axis=1).astype(BF16)
    w2t = jnp.concatenate([wka.T, wkb.T, wkb[:, swap_k].T], axis=0).astype(BF16)
    hd = jnp.arange(D_MIX) // HEAD_DIM
    bd = (hd[:, None] == hd[None, :]).astype(BF16)
    gq = gq_b[0] * scale
    gk = gk_b[0]
    wr = jnp.pad(w_router[0], ((0, 0), (0, LANES - N_EXPERTS)))
    wrh = wr.astype(BF16)
    wrl = (wr - wrh.astype(F32)).astype(BF16)
    return dict(
        g_attn=g_attn[0][None], w1=w1, w2t=w2t, bd=bd,
        gq=jnp.tile(gq, N_HEADS)[None], gqs=jnp.tile(gq[swap_h], N_HEADS)[None],
        gk=jnp.tile(gk, N_KV)[:, None], gks=jnp.tile(gk[swap_h], N_KV)[:, None],
        wo=w_out[0].astype(BF16), wrh=wrh, wrl=wrl,
        wg=w_gate[0].astype(BF16), wu=w_up[0].astype(BF16), wd=w_down[0].astype(BF16),
        wpg=w_ple_gate[0].astype(BF16), wpp=w_ple_proj[0].astype(BF16))


TOKEN_TILE = 512
ATTN_Q_TILE = 256
ATTN_KEY_CHUNK = 512
SELECT_SLOT_CHUNK = 1024
FFN_ROWS = 512
COMBINE_WINDOW = 128


def _trunk(x, p, prm, sink, rel_bias, g_out_a, g_out_b, g_ffn, g_ple, g_final):
    B, T, _ = x.shape
    N = B * T
    cap = CAPACITY_FACTOR * N // N_EXPERTS
    cq, sq, ck, sk = _rope_tables(T)
    qa, qb, va, vb, kat, kbt = _qkv(x, prm["g_attn"], prm["w1"], prm["w2t"], prm["bd"], prm["gq"], prm["gqs"],
                                    prm["gk"], prm["gks"], cq, sq, ck, sk, tn=TOKEN_TILE)
    tq = min(ATTN_Q_TILE, T)
    oa = _attn_a(qa, kat, va, sink, rel_bias, _bucket_table(tq), tq=tq)
    ob = _attn_b(qb, kbt, vb, tq=tq, kc=ATTN_KEY_CHUNK)
    x1, h2s, lg = _post(oa.reshape(N, D_MIX), ob.reshape(N, D_MIX), x.reshape(N, D_MODEL),
                        g_out_a, g_out_b, prm["wo"], g_ffn, prm["wrh"], prm["wrl"], tn=TOKEN_TILE)
    lg3 = lg[:, :N_EXPERTS].T.reshape(N_EXPERTS, N // LANES, LANES)
    idx, gate, pos, bex = _select(lg3, cap=cap, sc=SELECT_SLOT_CHUNK)
    rows = min(FFN_ROWS, cap)
    ye = _ffn(idx.reshape(-1, 1, rows), gate.reshape(-1, 1, rows), h2s, prm["wg"], prm["wu"], prm["wd"], cap=cap)
    tt = min(TOKEN_TILE, N)
    win = min(COMBINE_WINDOW, cap)
    tile_start = bex.reshape(N_EXPERTS, N // LANES)[:, ::tt // LANES]
    tab = jnp.concatenate([tile_start, jnp.full((N_EXPERTS, 1), cap, jnp.int32)], axis=1)
    y = _combine(tab, x1, pos.reshape(N_EXPERTS, N), p.reshape(N, D_PLE), ye, prm["wpg"], prm["wpp"],
                 g_ple, g_final, cap=cap, tt=tt, win=win)
    return y.reshape(B, T, D_MODEL)


def kernel(x_prompt, x_sample, p_prompt, p_sample, g_attn, w_in, sink_a, rel_bias, gq_b, gk_b, g_out_a, g_out_b, w_out, g_ffn, w_router, w_gate, w_up, w_down, g_ple, w_ple_gate, w_ple_proj, g_final):
    prm = _prep_params(g_attn, w_in, gq_b, gk_b, w_out, w_router, w_gate, w_up, w_down, w_ple_gate, w_ple_proj)
    args = (prm, sink_a[0], rel_bias, g_out_a[0][None], g_out_b[0][None], g_ffn[0][None], g_ple[0][None],
            g_final[None])
    y_prompt = _trunk(x_prompt, p_prompt[0], *args)
    y_sample = _trunk(x_sample, p_sample[0], *args)
    return (y_prompt, y_sample)
```

```python
import functools
import math

import jax
import jax.numpy as jnp
import numpy as np
from jax import lax
from jax.experimental import pallas as pl
from jax.experimental.pallas import tpu as pltpu

D_MODEL = 1024
HEAD_DIM = 64
N_HEADS = 8
N_KV = 2
GROUP = N_HEADS // N_KV
D_MIX = N_HEADS * HEAD_DIM
D_KV = N_KV * HEAD_DIM
D_GRP = GROUP * HEAD_DIM
WINDOW = 128
NUM_BUCKETS = 32
MAX_DISTANCE = 128
GRID_W = 64
ROPE_THETA = 10000.0
AX_PAIRS = HEAD_DIM // 4
N_EXPERTS = 16
CAPACITY_FACTOR = 2
D_EXPERT = 512
D_PLE = 256
EPS = 1e-6
NEG = -1e30
LOG2E = math.log2(math.e)
QSCALE = HEAD_DIM ** -0.5 * LOG2E

LANES = 128
SUBLANES = 8
ROW_SLABS = D_MODEL // LANES

F32 = jnp.float32
BF16 = jnp.bfloat16

_NT = (((1,), (1,)), ((), ()))
_TN = (((0,), (0,)), ((), ()))


def _cparams(mib, n_axes):
    return pltpu.CompilerParams(vmem_limit_bytes=mib * 1024 * 1024,
                                dimension_semantics=("arbitrary",) * n_axes)


def _const_spec(shape):
    zeros = (0,) * len(shape)
    return pl.BlockSpec(shape, lambda *_: zeros)


def _rms(x, g):
    r = lax.rsqrt(jnp.mean(x * x, axis=-1, keepdims=True) + EPS)
    return (x * r) * g


def _split_bf16(x):
    hi = x.astype(BF16)
    lo = (x - hi.astype(F32)).astype(BF16)
    return hi, lo


_Q_ROWS = 3 * D_MIX


def _qkv_kernel(x_ref, g_ref, w1_ref, w2t_ref, bd_ref, gq_ref, gqs_ref, gk_ref, gks_ref,
                ct_ref, st_ref, ck_ref, sk_ref,
                qat_ref, qbt_ref, vat_ref, vbt_ref, ka_ref, kb_ref):
    tn = x_ref.shape[1]
    h = _rms(x_ref[0], g_ref[...]).astype(BF16)
    p1 = jnp.dot(h, w1_ref[...], preferred_element_type=F32)
    p2 = lax.dot_general(w2t_ref[...], h, _NT, preferred_element_type=F32)

    qat_ref[0] = (p2[0:D_MIX] * QSCALE).astype(BF16)
    vat_ref[0] = p2[_Q_ROWS:_Q_ROWS + D_KV].astype(BF16)
    vbt_ref[0] = p2[_Q_ROWS + D_KV:_Q_ROWS + 2 * D_KV].astype(BF16)
    ka_ref[0] = p1[:, 0:D_KV].astype(BF16)

    q_raw = p2[D_MIX:2 * D_MIX].reshape(N_HEADS, HEAD_DIM, tn)
    q_swp = p2[2 * D_MIX:3 * D_MIX].reshape(N_HEADS, HEAD_DIM, tn)
    rq = lax.rsqrt(jnp.sum(q_raw * q_raw, axis=1, keepdims=True) * (1.0 / HEAD_DIM) + EPS)
    lane_reps = tn // LANES
    gq = jnp.concatenate([gq_ref[...]] * lane_reps, axis=1)[None]
    gqs = jnp.concatenate([gqs_ref[...]] * lane_reps, axis=1)[None]
    qn = (q_raw * rq) * gq
    qs = (q_swp * rq) * gqs
    qbt_ref[0] = (qn * ct_ref[...][None] + qs * st_ref[...][None]).reshape(D_MIX, tn).astype(BF16)

    k_raw = p1[:, D_KV:2 * D_KV]
    k_swp = p1[:, 2 * D_KV:3 * D_KV]
    hi, lo = _split_bf16(k_raw * k_raw)
    ssq = (jnp.dot(hi, bd_ref[...], preferred_element_type=F32)
           + jnp.dot(lo, bd_ref[...], preferred_element_type=F32))
    rk = lax.rsqrt(ssq * (1.0 / HEAD_DIM) + EPS)
    kn = (k_raw * rk) * gk_ref[...]
    ks = (k_swp * rk) * gks_ref[...]
    kb_ref[0] = (kn * ck_ref[...] + ks * sk_ref[...]).astype(BF16)


def _qkv(x, prm, ct, st, ck, sk, *, tn):
    B, T, _ = x.shape
    tn = min(tn, T)
    grid = (B, T // tn)
    tok = lambda w: pl.BlockSpec((1, tn, w), lambda b, j: (b, j, 0))
    feat = lambda r: pl.BlockSpec((1, r, tn), lambda b, j: (b, 0, j))
    return pl.pallas_call(
        _qkv_kernel,
        grid=grid,
        in_specs=[tok(D_MODEL), _const_spec((1, D_MODEL)), _const_spec((D_MODEL, 3 * D_KV)),
                  _const_spec((_Q_ROWS + 2 * D_KV, D_MODEL)), _const_spec((D_KV, D_KV)),
                  _const_spec((HEAD_DIM, LANES)), _const_spec((HEAD_DIM, LANES)),
                  _const_spec((1, D_KV)), _const_spec((1, D_KV)),
                  pl.BlockSpec((HEAD_DIM, tn), lambda b, j: (0, j)),
                  pl.BlockSpec((HEAD_DIM, tn), lambda b, j: (0, j)),
                  pl.BlockSpec((tn, D_KV), lambda b, j: (j, 0)),
                  pl.BlockSpec((tn, D_KV), lambda b, j: (j, 0))],
        out_specs=[feat(D_MIX), feat(D_MIX), feat(D_KV), feat(D_KV), tok(D_KV), tok(D_KV)],
        out_shape=[jax.ShapeDtypeStruct((B, D_MIX, T), BF16), jax.ShapeDtypeStruct((B, D_MIX, T), BF16),
                   jax.ShapeDtypeStruct((B, D_KV, T), BF16), jax.ShapeDtypeStruct((B, D_KV, T), BF16),
                   jax.ShapeDtypeStruct((B, T, D_KV), BF16), jax.ShapeDtypeStruct((B, T, D_KV), BF16)],
        compiler_params=_cparams(48, 2),
        name="qkv",
    )(x, prm["g_attn"], prm["w1"], prm["w2t"], prm["bd"], prm["gq"], prm["gqs"], prm["gk"], prm["gks"],
      ct, st, ck, sk)


def _own_kv_lanes(k, kvh):
    lane_head = lax.shift_right_logical(lax.broadcasted_iota(jnp.int32, (1, D_KV), 1), 6)
    return k * (lane_head == kvh).astype(k.dtype)


def _head_queries(qt_ref, g):
    qg = qt_ref[0, g * HEAD_DIM:(g + 1) * HEAD_DIM, :]
    return jnp.concatenate([qg, qg], axis=0)


def _attn_b_kernel(qt_ref, k_ref, vt_ref, o_ref, km_ref, p_ref, acc_ref, m_ref, l_ref, a_ref, *, kc):
    T = k_ref.shape[1]
    n_chunks = T // kc

    @pl.when(pl.program_id(2) == 0)
    def _():
        km_ref[...] = _own_kv_lanes(k_ref[0], pl.program_id(1))

    acc_ref[...] = jnp.zeros(acc_ref.shape, F32)
    m_ref[...] = jnp.full(m_ref.shape, NEG, F32)
    l_ref[...] = jnp.zeros(l_ref.shape, F32)

    def scores(c):
        k0 = pl.multiple_of(c * kc, kc)
        kblk = km_ref[pl.ds(k0, kc), :]
        for g in range(GROUP):
            s = jnp.dot(kblk, _head_queries(qt_ref, g), preferred_element_type=F32)
            m_old = m_ref[g]
            m_new = jnp.maximum(m_old, jnp.max(s, axis=0, keepdims=True))
            p = jnp.exp2(s - m_new)
            alpha = jnp.exp2(m_old - m_new)
            l_ref[g] = alpha * l_ref[g] + jnp.sum(p, axis=0, keepdims=True)
            m_ref[g] = m_new
            a_ref[g] = alpha
            p_ref[g] = p.astype(BF16)

    def values(c):
        k0 = pl.multiple_of(c * kc, kc)
        vblk = vt_ref[0, :, pl.ds(k0, kc)]
        for g in range(GROUP):
            rows = slice(g * HEAD_DIM, (g + 1) * HEAD_DIM)
            acc_ref[rows, :] = a_ref[g] * acc_ref[rows, :] + jnp.dot(vblk, p_ref[g], preferred_element_type=F32)

    scores(0)

    def body(c, carry):
        values(c - 1)
        scores(c)
        return carry

    lax.fori_loop(1, n_chunks, body, 0)
    values(n_chunks - 1)
    out = jnp.concatenate([acc_ref[g * HEAD_DIM:(g + 1) * HEAD_DIM, :] * (1.0 / l_ref[g]) for g in range(GROUP)],
                          axis=0)
    o_ref[0] = out.T


def _attn_b(qt, k, vt, *, tq, kc):
    B, _, T = qt.shape
    tq = min(tq, T)
    kc = min(kc, T)
    grid = (B, N_KV, T // tq)
    return pl.pallas_call(
        functools.partial(_attn_b_kernel, kc=kc),
        grid=grid,
        in_specs=[pl.BlockSpec((1, D_GRP, tq), lambda b, h, i: (b, h, i)),
                  pl.BlockSpec((1, T, D_KV), lambda b, h, i: (b, 0, 0)),
                  pl.BlockSpec((1, HEAD_DIM, T), lambda b, h, i: (b, h, 0))],
        out_specs=pl.BlockSpec((1, tq, D_GRP), lambda b, h, i: (b, i, h)),
        out_shape=jax.ShapeDtypeStruct((B, T, D_MIX), F32),
        scratch_shapes=[pltpu.VMEM((T, D_KV), BF16), pltpu.VMEM((GROUP, kc, tq), BF16),
                        pltpu.VMEM((D_GRP, tq), F32), pltpu.VMEM((GROUP, 1, tq), F32),
                        pltpu.VMEM((GROUP, 1, tq), F32), pltpu.VMEM((GROUP, 1, tq), F32)],
        compiler_params=_cparams(48, 3),
        name="attn_b",
    )(qt, k, vt)


def _attn_a_kernel(sink_ref, rb_ref, bucket_ref, qt_ref, k_ref, vt_ref, o_ref, km_ref, vp_ref, bias_ref):
    T = k_ref.shape[1]
    tq = qt_ref.shape[2]
    kw = tq + 2 * WINDOW
    kvh = pl.program_id(1)
    qi = pl.program_id(2)

    @pl.when((pl.program_id(0) == 0) & (kvh == 0) & (qi == 0))
    def _():
        km_ref[...] = jnp.zeros(km_ref.shape, BF16)
        vp_ref[...] = jnp.zeros(vp_ref.shape, BF16)
        bucket = bucket_ref[...]
        for h in range(N_HEADS):
            bias = jnp.full((kw, tq), NEG, F32)
            for k in range(NUM_BUCKETS):
                bias = jnp.where(bucket == k, rb_ref[k, h], bias)
            bias_ref[h] = bias * LOG2E

    @pl.when(qi == 0)
    def _():
        km_ref[WINDOW:WINDOW + T, :] = _own_kv_lanes(k_ref[0], kvh)
        vp_ref[:, WINDOW:WINDOW + T] = vt_ref[0]

    start = pl.multiple_of(qi * tq, tq)
    kpos = start - WINDOW + lax.broadcasted_iota(jnp.int32, (kw, tq), 0)
    off_seq = jnp.where((kpos >= 0) & (kpos < T), 0.0, NEG).astype(F32)
    kblk = km_ref[pl.ds(start, kw), :]
    vblk = vp_ref[:, pl.ds(start, kw)]
    outs = []
    for g in range(GROUP):
        head = kvh * GROUP + g
        s = jnp.dot(kblk, _head_queries(qt_ref, g), preferred_element_type=F32) + (bias_ref[head] + off_seq)
        sink = jnp.full((1, tq), sink_ref[head], F32) * LOG2E
        m = jnp.maximum(jnp.max(s, axis=0, keepdims=True), sink)
        e = jnp.exp2(s - m)
        den = jnp.sum(e, axis=0, keepdims=True) + jnp.exp2(sink - m)
        pv = jnp.dot(vblk, e.astype(BF16), preferred_element_type=F32)
        outs.append(pv * (1.0 / den))
    o_ref[0] = jnp.concatenate(outs, axis=0).T


def _attn_a(qt, k, vt, sink, rel_bias, bucket):
    B, _, T = qt.shape
    kw, tq = bucket.shape
    grid = (B, N_KV, T // tq)
    smem = pl.BlockSpec(memory_space=pltpu.SMEM)
    return pl.pallas_call(
        _attn_a_kernel,
        grid=grid,
        in_specs=[smem, smem, _const_spec((kw, tq)),
                  pl.BlockSpec((1, D_GRP, tq), lambda b, h, i: (b, h, i)),
                  pl.BlockSpec((1, T, D_KV), lambda b, h, i: (b, 0, 0)),
                  pl.BlockSpec((1, HEAD_DIM, T), lambda b, h, i: (b, h, 0))],
        out_specs=pl.BlockSpec((1, tq, D_GRP), lambda b, h, i: (b, i, h)),
        out_shape=jax.ShapeDtypeStruct((B, T, D_MIX), F32),
        scratch_shapes=[pltpu.VMEM((T + 2 * WINDOW, D_KV), BF16),
                        pltpu.VMEM((HEAD_DIM, T + 2 * WINDOW), BF16),
                        pltpu.VMEM((N_HEADS, kw, tq), F32)],
        compiler_params=_cparams(48, 3),
        name="attn_a",
    )(sink, rel_bias, bucket, qt, k, vt)


def _post_kernel(oa_ref, ob_ref, x_ref, ga_ref, gb_ref, wo_ref, gf_ref, wrh_ref, wrl_ref,
                 x1_ref, h2s_ref, lg_ref):
    tn = x_ref.shape[0]
    mixed = jnp.concatenate([_rms(oa_ref[...], ga_ref[...]), _rms(ob_ref[...], gb_ref[...])], axis=1)
    x1 = x_ref[...] + jnp.dot(mixed.astype(BF16), wo_ref[...], preferred_element_type=F32)
    x1_ref[...] = x1
    h2 = _rms(x1, gf_ref[...])
    for s in range(ROW_SLABS):
        h2s_ref[pl.ds(s, tn, stride=ROW_SLABS), :] = h2[:, s * LANES:(s + 1) * LANES]
    hi, lo = _split_bf16(h2)
    lg_ref[...] = (jnp.dot(hi, wrh_ref[...], preferred_element_type=F32)
                   + jnp.dot(lo, wrh_ref[...], preferred_element_type=F32)
                   + jnp.dot(hi, wrl_ref[...], preferred_element_type=F32))


def _post(oa, ob, x, ga, gb, wo, gf, wrh, wrl, *, tn):
    N = x.shape[0]
    tn = min(tn, N)
    tok = lambda w: pl.BlockSpec((tn, w), lambda j: (j, 0))
    return pl.pallas_call(
        _post_kernel,
        grid=(N // tn,),
        in_specs=[tok(D_MIX), tok(D_MIX), tok(D_MODEL), _const_spec((1, D_MIX)), _const_spec((1, D_MIX)),
                  _const_spec((D_MODEL, D_MODEL)), _const_spec((1, D_MODEL)),
                  _const_spec((D_MODEL, LANES)), _const_spec((D_MODEL, LANES))],
        out_specs=[tok(D_MODEL), pl.BlockSpec((ROW_SLABS * tn, LANES), lambda j: (j, 0)), tok(LANES)],
        out_shape=[jax.ShapeDtypeStruct((N, D_MODEL), F32),
                   jax.ShapeDtypeStruct((ROW_SLABS * N, LANES), F32),
                   jax.ShapeDtypeStruct((N, LANES), F32)],
        compiler_params=_cparams(48, 1),
        name="post",
    )(oa, ob, x, ga, gb, wo, gf, wrh, wrl)


def _select_kernel(lg_ref, idx_ref, gate_ref, pos_ref, bex_ref, *, cap, sc):
    e = pl.program_id(0)
    nb = lg_ref.shape[1]
    lg = lg_ref[...]
    mx = jnp.max(lg, axis=0)
    den = jnp.sum(jnp.exp(lg - mx), axis=0)
    aff = jnp.exp(lg_ref[e] - mx) / den

    def bisect(_, lohi):
        lo, hi = lohi
        mid = lo + jnp.right_shift(hi - lo, 1)
        thr = lax.bitcast_convert_type(mid, F32)
        cnt = jnp.sum((aff >= thr).astype(jnp.int32))
        ok = cnt >= cap
        return jnp.where(ok, mid, lo), jnp.where(ok, hi, mid)

    lo0 = jnp.zeros((1, 1), jnp.int32)
    hi0 = jnp.full((1, 1), 0x7F800000, jnp.int32)
    lo, hi = lax.fori_loop(0, 31, bisect, (lo0, hi0))
    above = aff >= lax.bitcast_convert_type(hi, F32)
    tie = (aff >= lax.bitcast_convert_type(lo, F32)) & jnp.logical_not(above)
    need = cap - jnp.sum(above.astype(jnp.int32))

    li = lax.broadcasted_iota(jnp.int32, (LANES, LANES), 0)
    lj = lax.broadcasted_iota(jnp.int32, (LANES, LANES), 1)
    ut_incl = (li <= lj).astype(BF16)
    ut_excl = (li < lj).astype(BF16)
    bi = lax.broadcasted_iota(jnp.int32, (nb, nb), 0)
    bj = lax.broadcasted_iota(jnp.int32, (nb, nb), 1)
    lt_incl = (bj <= bi).astype(BF16)
    lt_excl = (bj < bi).astype(BF16)

    def last(x):
        return x[:, LANES - 1:LANES]

    tie_b = tie.astype(BF16)
    tie_excl = jnp.dot(tie_b, ut_excl, preferred_element_type=F32)
    tie_incl = jnp.dot(tie_b, ut_incl, preferred_element_type=F32)
    tie_before = last(jnp.dot(lt_excl, tie_incl.astype(BF16), preferred_element_type=F32))
    tie_rank = (tie_before + tie_excl).astype(jnp.int32)
    sel = above | (tie & (tie_rank < need))

    lcum = jnp.dot(sel.astype(BF16), ut_incl, preferred_element_type=F32)
    lcum_b = lcum.astype(BF16)
    bcum_incl = last(jnp.dot(lt_incl, lcum_b, preferred_element_type=F32))
    bcum_excl = bcum_incl - last(lcum)
    pos_ref[0] = jnp.where(sel, (bcum_excl + lcum).astype(jnp.int32) - 1, -1)
    bex_ref[0] = bcum_excl.astype(jnp.int32)

    a1 = aff.astype(BF16)
    a2 = (aff - a1.astype(F32)).astype(BF16)
    a3 = (aff - a1.astype(F32) - a2.astype(F32)).astype(BF16)
    blk = lax.broadcasted_iota(jnp.int32, (nb, sc), 0)
    row = lax.broadcasted_iota(jnp.int32, (LANES, sc), 0)
    for c in range(cap // sc):
        slot = (c * sc + lax.broadcasted_iota(jnp.int32, (1, sc), 1)).astype(F32)
        hit = (bcum_excl <= slot) & (slot < bcum_incl)
        hit_b = hit.astype(BF16)
        base = jnp.sum(jnp.where(hit, bcum_excl, 0.0), axis=0, keepdims=True)
        blk_of = jnp.sum(jnp.where(hit, blk, 0), axis=0, keepdims=True)
        lc_t = lax.dot_general(lcum_b, hit_b, _TN, preferred_element_type=F32)
        k_in = slot - base
        j_of = jnp.sum((lc_t <= k_in).astype(jnp.int32), axis=0, keepdims=True)
        idx_ref[0, :, c * sc:(c + 1) * sc] = blk_of * LANES + j_of
        aff_t = (lax.dot_general(a1, hit_b, _TN, preferred_element_type=F32)
                 + lax.dot_general(a2, hit_b, _TN, preferred_element_type=F32)
                 + lax.dot_general(a3, hit_b, _TN, preferred_element_type=F32))
        gate_ref[0, :, c * sc:(c + 1) * sc] = jnp.sum(jnp.where(row == j_of, aff_t, 0.0), axis=0, keepdims=True)


def _select(lg3, *, cap, sc):
    E, nb, _ = lg3.shape
    sc = min(sc, cap)
    per_e = lambda r, c, dt: (pl.BlockSpec((1, r, c), lambda e: (e, 0, 0)), jax.ShapeDtypeStruct((E, r, c), dt))
    specs = [per_e(1, cap, jnp.int32), per_e(1, cap, F32), per_e(nb, LANES, jnp.int32), per_e(nb, 1, jnp.int32)]
    return pl.pallas_call(
        functools.partial(_select_kernel, cap=cap, sc=sc),
        grid=(E,),
        in_specs=[_const_spec((E, nb, LANES))],
        out_specs=[s for s, _ in specs],
        out_shape=[o for _, o in specs],
        compiler_params=_cparams(48, 1),
        name="select",
    )(lg3)


def _ffn_kernel(idx_ref, idxn_ref, gate_ref, h2s_ref, wg_ref, wu_ref, wd_ref, ye_ref, xbuf_ref, sem_ref):
    rows = ye_ref.shape[0]
    n = pl.program_id(0)
    n_steps = pl.num_programs(0)
    slot = n % 2

    def row_copy(ids_ref, r, to_slot):
        src = pl.multiple_of(ids_ref[0, 0, r] * ROW_SLABS, ROW_SLABS)
        return pltpu.make_async_copy(h2s_ref.at[pl.ds(src, ROW_SLABS), :],
                                     xbuf_ref.at[to_slot, pl.ds(r * ROW_SLABS, ROW_SLABS), :],
                                     sem_ref.at[to_slot])

    def start_gather(ids_ref, to_slot):
        def body(r, carry):
            row_copy(ids_ref, r, to_slot).start()
            return carry
        lax.fori_loop(0, rows, body, 0)

    @pl.when(n == 0)
    def _():
        start_gather(idx_ref, 0)

    @pl.when(n + 1 < n_steps)
    def _():
        start_gather(idxn_ref, 1 - slot)

    pltpu.make_async_copy(h2s_ref.at[pl.ds(0, rows * ROW_SLABS), :], xbuf_ref.at[slot], sem_ref.at[slot]).wait()

    x = jnp.concatenate(
        [xbuf_ref[slot, pl.ds(s, rows, stride=ROW_SLABS), :].astype(BF16) for s in range(ROW_SLABS)], axis=1)
    a = jnp.dot(x, wg_ref[0], preferred_element_type=F32)
    u = jnp.dot(x, wu_ref[0], preferred_element_type=F32)
    hmid = (jax.nn.silu(a) * u).astype(BF16)
    y = jnp.dot(hmid, wd_ref[0], preferred_element_type=F32)
    ri = lax.broadcasted_iota(jnp.int32, (rows, rows), 0)
    ci = lax.broadcasted_iota(jnp.int32, (rows, rows), 1)
    gate_col = jnp.sum(jnp.where(ri == ci, gate_ref[0], 0.0), axis=1, keepdims=True)
    ye_ref[...] = (gate_col * y).astype(BF16)


def _ffn(idx3, gate3, h2s, wg, wu, wd, *, cap):
    n_steps, _, rows = idx3.shape
    per_e = cap // rows
    last = n_steps - 1
    return pl.pallas_call(
        _ffn_kernel,
        grid=(n_steps,),
        in_specs=[pl.BlockSpec((1, 1, rows), lambda n: (n, 0, 0), memory_space=pltpu.SMEM),
                  pl.BlockSpec((1, 1, rows), lambda n: (jnp.minimum(n + 1, last), 0, 0), memory_space=pltpu.SMEM),
                  pl.BlockSpec((1, 1, rows), lambda n: (n, 0, 0)),
                  pl.BlockSpec(memory_space=pl.ANY),
                  pl.BlockSpec((1, D_MODEL, D_EXPERT), lambda n: (n // per_e, 0, 0)),
                  pl.BlockSpec((1, D_MODEL, D_EXPERT), lambda n: (n // per_e, 0, 0)),
                  pl.BlockSpec((1, D_EXPERT, D_MODEL), lambda n: (n // per_e, 0, 0))],
        out_specs=pl.BlockSpec((rows, D_MODEL), lambda n: (n, 0)),
        out_shape=jax.ShapeDtypeStruct((n_steps * rows, D_MODEL), BF16),
        scratch_shapes=[pltpu.VMEM((2, rows * ROW_SLABS, LANES), F32), pltpu.SemaphoreType.DMA((2,))],
        compiler_params=_cparams(48, 1),
        name="ffn",
    )(idx3, idx3, gate3, h2s, wg, wu, wd)


def _combine_kernel(tab_ref, x1_ref, pos_ref, p_ref, ye_ref, wpg_ref, wpp_ref, gp_ref, gfin_ref,
                    y_ref, win_ref, sem_ref, acc_ref, *, cap, win):
    tt = x1_ref.shape[0]
    i = pl.program_id(0)
    total = N_EXPERTS * cap
    wi = lax.broadcasted_iota(jnp.int32, (win, tt), 0)

    def window(e):
        first = e * cap + tab_ref[e, i]
        end = e * cap + tab_ref[e, i + 1]
        a0 = jnp.minimum((first // 16) * 16, total - win)
        n_chunks = jnp.maximum((end - a0 + win - 1) // win, 1)
        return a0, n_chunks

    def copy(a0, to_slot):
        a0 = pl.multiple_of(a0, 16)
        return pltpu.make_async_copy(ye_ref.at[pl.ds(a0, win), :], win_ref.at[to_slot], sem_ref.at[to_slot])

    def add_chunk(e, a0, from_slot):
        onehot = ((pos_ref[e:e + 1, :] + (e * cap - a0)) == wi) & (pos_ref[e:e + 1, :] >= 0)
        acc_ref[...] += lax.dot_general(onehot.astype(BF16), win_ref[from_slot], _TN,
                                        preferred_element_type=F32)

    acc_ref[...] = jnp.zeros(acc_ref.shape, F32)
    a0_first, _ = window(0)
    copy(a0_first, 0).start()
    for e in range(N_EXPERTS):
        slot = e % 2
        a0, n_chunks = window(e)
        if e + 1 < N_EXPERTS:
            a0_next, _ = window(e + 1)
            copy(a0_next, 1 - slot).start()
        copy(a0, slot).wait()
        add_chunk(e, a0, slot)

        def extra(c, carry, e=e, a0=a0, slot=slot):
            a = jnp.minimum(a0 + c * win, total - win)
            copy(a, slot).start()
            copy(a, slot).wait()
            add_chunk(e, a, slot)
            return carry
        lax.fori_loop(1, n_chunks, extra, 0)

    x2 = x1_ref[...] + acc_ref[...]
    gate = jax.nn.sigmoid(jnp.dot(_rms(x2, gp_ref[...]).astype(BF16), wpg_ref[...], preferred_element_type=F32))
    x3 = x2 + gate * jnp.dot(p_ref[...].astype(BF16), wpp_ref[...], preferred_element_type=F32)
    y_ref[...] = _rms(x3, gfin_ref[...])


def _combine(tab, x1, pos, p, ye, wpg, wpp, gp, gfin, *, cap, tt, win):
    N = x1.shape[0]
    grid_spec = pltpu.PrefetchScalarGridSpec(
        num_scalar_prefetch=1,
        grid=(N // tt,),
        in_specs=[pl.BlockSpec((tt, D_MODEL), lambda i, t: (i, 0)),
                  pl.BlockSpec((N_EXPERTS, tt), lambda i, t: (0, i)),
                  pl.BlockSpec((tt, D_PLE), lambda i, t: (i, 0)),
                  pl.BlockSpec(memory_space=pl.ANY),
                  pl.BlockSpec((D_MODEL, D_MODEL), lambda i, t: (0, 0)),
                  pl.BlockSpec((D_PLE, D_MODEL), lambda i, t: (0, 0)),
                  pl.BlockSpec((1, D_MODEL), lambda i, t: (0, 0)),
                  pl.BlockSpec((1, D_MODEL), lambda i, t: (0, 0))],
        out_specs=pl.BlockSpec((tt, D_MODEL), lambda i, t: (i, 0)),
        scratch_shapes=[pltpu.VMEM((2, win, D_MODEL), BF16), pltpu.SemaphoreType.DMA((2,)),
                        pltpu.VMEM((tt, D_MODEL), F32)],
    )
    return pl.pallas_call(
        functools.partial(_combine_kernel, cap=cap, win=win),
        grid_spec=grid_spec,
        out_shape=jax.ShapeDtypeStruct((N, D_MODEL), F32),
        compiler_params=_cparams(48, 1),
        name="combine",
    )(tab, x1, pos, p, ye, wpg, wpp, gp, gfin)


def _t5_bucket(rel):
    nb = NUM_BUCKETS // 2
    ret = jnp.where(rel > 0, nb, 0)
    n = jnp.abs(rel)
    max_exact = nb // 2
    large = max_exact + (jnp.log(jnp.maximum(n, 1).astype(F32) / max_exact)
                         / math.log(MAX_DISTANCE / max_exact) * (nb - max_exact)).astype(jnp.int32)
    large = jnp.minimum(large, nb - 1)
    return ret + jnp.where(n < max_exact, n, large)


def _bucket_table(tq):
    kw = tq + 2 * WINDOW
    rel = jnp.arange(kw)[:, None] - WINDOW - jnp.arange(tq)[None, :]
    return jnp.where(jnp.abs(rel) <= WINDOW, _t5_bucket(rel), -1).astype(jnp.int32)


def _rope_tables(T):
    rows = T // GRID_W
    row = jnp.repeat(jnp.arange(rows), GRID_W).astype(F32)
    col = jnp.tile(jnp.arange(GRID_W), rows).astype(F32)
    freqs = ROPE_THETA ** (-jnp.arange(AX_PAIRS, dtype=F32) / AX_PAIRS)
    ang = jnp.concatenate([row[:, None] * freqs, col[:, None] * freqs], axis=-1)
    cos = jnp.repeat(jnp.cos(ang), 2, axis=1)
    sin = jnp.repeat(jnp.sin(ang), 2, axis=1) * jnp.tile(jnp.array([-1.0, 1.0], F32), HEAD_DIM // 2)
    return cos.T, sin.T, jnp.tile(cos, (1, N_KV)), jnp.tile(sin, (1, N_KV))


def _prep_params(g_attn, w_in, gq_b, gk_b, w_out, w_router, w_gate, w_up, w_down, w_ple_gate, w_ple_proj):
    w = w_in[0]
    o = np.cumsum([0, D_MIX, D_KV, D_KV, D_MIX, D_KV, D_KV])
    wqa, wka, wva, wqb, wkb, wvb = (w[:, o[i]:o[i + 1]] for i in range(6))
    swap_q = jnp.arange(D_MIX) ^ 1
    swap_k = jnp.arange(D_KV) ^ 1
    swap_h = jnp.arange(HEAD_DIM) ^ 1
    w1 = jnp.concatenate([wka, wkb, wkb[:, swap_k]], axis=1).astype(BF16)
    w2t = jnp.concatenate([wqa.T, wqb.T, wqb[:, swap_q].T, wva.T, wvb.T], axis=0).astype(BF16)
    hd = jnp.arange(D_KV) // HEAD_DIM
    bd = (hd[:, None] == hd[None, :]).astype(BF16)
    gq = gq_b[0] * QSCALE
    gk = gk_b[0]
    wr = jnp.pad(w_router[0], ((0, 0), (0, LANES - N_EXPERTS)))
    wrh = wr.astype(BF16)
    wrl = (wr - wrh.astype(F32)).astype(BF16)
    return dict(
        g_attn=g_attn[0][None], w1=w1, w2t=w2t, bd=bd,
        gq=jnp.broadcast_to(gq[:, None], (HEAD_DIM, LANES)),
        gqs=jnp.broadcast_to(gq[swap_h][:, None], (HEAD_DIM, LANES)),
        gk=jnp.tile(gk, N_KV)[None], gks=jnp.tile(gk[swap_h], N_KV)[None],
        wo=w_out[0].astype(BF16), wrh=wrh, wrl=wrl,
        wg=w_gate[0].astype(BF16), wu=w_up[0].astype(BF16), wd=w_down[0].astype(BF16),
        wpg=w_ple_gate[0].astype(BF16), wpp=w_ple_proj[0].astype(BF16))


TOKEN_TILE = 512
WINDOW_Q_TILE = 256
DENSE_Q_TILE = 512
DENSE_KEY_CHUNK = 512
SELECT_SLOT_CHUNK = 1024
FFN_ROWS = 512
COMBINE_WINDOW = 128


def _trunk(x, p, prm, sink, rel_bias, g_out_a, g_out_b, g_ffn, g_ple, g_final):
    B, T, _ = x.shape
    N = B * T
    cap = CAPACITY_FACTOR * N // N_EXPERTS
    qat, qbt, vat, vbt, ka, kb = _qkv(x, prm, *_rope_tables(T), tn=TOKEN_TILE)
    oa = _attn_a(qat, ka, vat, sink, rel_bias, _bucket_table(min(WINDOW_Q_TILE, T)))
    ob = _attn_b(qbt, kb, vbt, tq=DENSE_Q_TILE, kc=DENSE_KEY_CHUNK)
    x1, h2s, lg = _post(oa.reshape(N, D_MIX), ob.reshape(N, D_MIX), x.reshape(N, D_MODEL),
                        g_out_a, g_out_b, prm["wo"], g_ffn, prm["wrh"], prm["wrl"], tn=TOKEN_TILE)
    lg3 = lg[:, :N_EXPERTS].T.reshape(N_EXPERTS, N // LANES, LANES)
    idx, gate, pos, bex = _select(lg3, cap=cap, sc=SELECT_SLOT_CHUNK)
    rows = min(FFN_ROWS, cap)
    ye = _ffn(idx.reshape(-1, 1, rows), gate.reshape(-1, 1, rows), h2s, prm["wg"], prm["wu"], prm["wd"], cap=cap)
    tt = min(TOKEN_TILE, N)
    win = min(COMBINE_WINDOW, cap)
    tile_start = bex.reshape(N_EXPERTS, N // LANES)[:, ::tt // LANES]
    tab = jnp.concatenate([tile_start, jnp.full((N_EXPERTS, 1), cap, jnp.int32)], axis=1)
    y = _combine(tab, x1, pos.reshape(N_EXPERTS, N), p.reshape(N, D_PLE), ye, prm["wpg"], prm["wpp"],
                 g_ple, g_final, cap=cap, tt=tt, win=win)
    return y.reshape(B, T, D_MODEL)


def kernel(x_prompt, x_sample, p_prompt, p_sample, g_attn, w_in, sink_a, rel_bias, gq_b, gk_b, g_out_a, g_out_b, w_out, g_ffn, w_router, w_gate, w_up, w_down, g_ple, w_ple_gate, w_ple_proj, g_final):
    prm = _prep_params(g_attn, w_in, gq_b, gk_b, w_out, w_router, w_gate, w_up, w_down, w_ple_gate, w_ple_proj)
    args = (prm, sink_a[0], rel_bias, g_out_a[0][None], g_out_b[0][None], g_ffn[0][None], g_ple[0][None],
            g_final[None])
    y_prompt = _trunk(x_prompt, p_prompt[0], *args)
    y_sample = _trunk(x_sample, p_sample[0], *args)
    return (y_prompt, y_sample)
```

```python
import functools
import math

import jax
import jax.numpy as jnp
import numpy as np
from jax import lax
from jax.experimental import pallas as pl
from jax.experimental.pallas import tpu as pltpu

D_MODEL = 1024
HEAD_DIM = 64
N_HEADS = 8
N_KV = 2
GROUP = N_HEADS // N_KV
D_MIX = N_HEADS * HEAD_DIM
D_KV = N_KV * HEAD_DIM
D_GRP = GROUP * HEAD_DIM
WINDOW = 128
NUM_BUCKETS = 32
MAX_DISTANCE = 128
GRID_W = 64
ROPE_THETA = 10000.0
AX_PAIRS = HEAD_DIM // 4
N_EXPERTS = 16
CAPACITY_FACTOR = 2
D_EXPERT = 512
D_PLE = 256
EPS = 1e-6
NEG = -1e30
LOG2E = math.log2(math.e)
QSCALE = HEAD_DIM ** -0.5 * LOG2E

LANES = 128
SUBLANES = 8
ROW_SLABS = D_MODEL // LANES

F32 = jnp.float32
BF16 = jnp.bfloat16

_NT = (((1,), (1,)), ((), ()))
_TN = (((0,), (0,)), ((), ()))


def _cparams(mib, n_axes, flags=None):
    return pltpu.CompilerParams(vmem_limit_bytes=mib * 1024 * 1024,
                                dimension_semantics=("arbitrary",) * n_axes, flags=flags)


def _const_spec(shape):
    zeros = (0,) * len(shape)
    return pl.BlockSpec(shape, lambda *_: zeros)


def _rms(x, g):
    r = lax.rsqrt(jnp.mean(x * x, axis=-1, keepdims=True) + EPS)
    return (x * r) * g


def _split_bf16(x):
    hi = x.astype(BF16)
    lo = (x - hi.astype(F32)).astype(BF16)
    return hi, lo


_Q_ROWS = 3 * D_MIX
D_KX = N_KV * LANES


def _qkv_kernel(x_ref, g_ref, w1_ref, w2t_ref, bd_ref, gq_ref, gqs_ref, gk_ref, gks_ref,
                ct_ref, st_ref, ck_ref, sk_ref,
                qat_ref, qbt_ref, vat_ref, vbt_ref, ka_ref, kb_ref):
    tn = x_ref.shape[1]
    h = _rms(x_ref[0], g_ref[...]).astype(BF16)
    p1 = jnp.dot(h, w1_ref[...], preferred_element_type=F32)
    p2 = lax.dot_general(w2t_ref[...], h, _NT, preferred_element_type=F32)

    qat_ref[0] = (p2[0:D_MIX] * QSCALE).astype(BF16)
    vat_ref[0] = p2[_Q_ROWS:_Q_ROWS + D_KV].astype(BF16)
    vbt_ref[0] = p2[_Q_ROWS + D_KV:_Q_ROWS + 2 * D_KV].astype(BF16)
    ka_ref[0] = p1[:, 0:D_KX].astype(BF16)

    q_raw = p2[D_MIX:2 * D_MIX].reshape(N_HEADS, HEAD_DIM, tn)
    q_swp = p2[2 * D_MIX:3 * D_MIX].reshape(N_HEADS, HEAD_DIM, tn)
    rq = lax.rsqrt(jnp.sum(q_raw * q_raw, axis=1, keepdims=True) * (1.0 / HEAD_DIM) + EPS)
    lane_reps = tn // LANES
    gq = jnp.concatenate([gq_ref[...]] * lane_reps, axis=1)[None]
    gqs = jnp.concatenate([gqs_ref[...]] * lane_reps, axis=1)[None]
    qn = (q_raw * rq) * gq
    qs = (q_swp * rq) * gqs
    qbt_ref[0] = (qn * ct_ref[...][None] + qs * st_ref[...][None]).reshape(D_MIX, tn).astype(BF16)

    k_raw = p1[:, D_KX:2 * D_KX]
    k_swp = p1[:, 2 * D_KX:3 * D_KX]
    hi, lo = _split_bf16(k_raw * k_raw)
    ssq = (jnp.dot(hi, bd_ref[...], preferred_element_type=F32)
           + jnp.dot(lo, bd_ref[...], preferred_element_type=F32))
    rk = lax.rsqrt(ssq * (1.0 / HEAD_DIM) + EPS)
    ck = jnp.concatenate([ck_ref[...]] * N_KV, axis=1)
    sk = jnp.concatenate([sk_ref[...]] * N_KV, axis=1)
    kn = (k_raw * rk) * gk_ref[...]
    ks = (k_swp * rk) * gks_ref[...]
    kb_ref[0] = (kn * ck + ks * sk).astype(BF16)


def _qkv(x, prm, ct, st, ck, sk, *, tn):
    B, T, _ = x.shape
    tn = min(tn, T)
    grid = (B, T // tn)
    tok = lambda w: pl.BlockSpec((1, tn, w), lambda b, j: (b, j, 0))
    feat = lambda r: pl.BlockSpec((1, r, tn), lambda b, j: (b, 0, j))
    return pl.pallas_call(
        _qkv_kernel,
        grid=grid,
        in_specs=[tok(D_MODEL), _const_spec((1, D_MODEL)), _const_spec((D_MODEL, 3 * D_KX)),
                  _const_spec((_Q_ROWS + 2 * D_KV, D_MODEL)), _const_spec((D_KX, D_KX)),
                  _const_spec((HEAD_DIM, LANES)), _const_spec((HEAD_DIM, LANES)),
                  _const_spec((1, D_KX)), _const_spec((1, D_KX)),
                  pl.BlockSpec((HEAD_DIM, tn), lambda b, j: (0, j)),
                  pl.BlockSpec((HEAD_DIM, tn), lambda b, j: (0, j)),
                  pl.BlockSpec((tn, LANES), lambda b, j: (j, 0)),
                  pl.BlockSpec((tn, LANES), lambda b, j: (j, 0))],
        out_specs=[feat(D_MIX), feat(D_MIX), feat(D_KV), feat(D_KV), tok(D_KX), tok(D_KX)],
        out_shape=[jax.ShapeDtypeStruct((B, D_MIX, T), BF16), jax.ShapeDtypeStruct((B, D_MIX, T), BF16),
                   jax.ShapeDtypeStruct((B, D_KV, T), BF16), jax.ShapeDtypeStruct((B, D_KV, T), BF16),
                   jax.ShapeDtypeStruct((B, T, D_KX), BF16), jax.ShapeDtypeStruct((B, T, D_KX), BF16)],
        compiler_params=_cparams(48, 2),
        name="qkv",
    )(x, prm["g_attn"], prm["w1"], prm["w2t"], prm["bd"], prm["gq"], prm["gqs"], prm["gk"], prm["gks"],
      ct, st, ck, sk)


ONES_ROWS = 16
V1_ROWS = HEAD_DIM + ONES_ROWS


def _head_weights(qt_ref, g, extra):
    return jnp.concatenate([qt_ref[0, g * HEAD_DIM:(g + 1) * HEAD_DIM, :], extra], axis=0)


def _normalised(acc):
    return acc[0:HEAD_DIM] * (1.0 / acc[HEAD_DIM:HEAD_DIM + 1])


def _attn_b_kernel(qt_ref, k_ref, vt_ref, o_ref, v1_ref, s0_ref, s1_ref, p0_ref, p1_ref, a0_ref, a1_ref,
                   acc_ref, m_ref, *, kc):
    s_refs, p_refs, a_refs = (s0_ref, s1_ref), (p0_ref, p1_ref), (a0_ref, a1_ref)
    T = k_ref.shape[1]
    tq = qt_ref.shape[2]
    n_chunks = T // kc

    @pl.when(pl.program_id(2) == 0)
    def _():
        v1_ref[0:HEAD_DIM, :] = vt_ref[0]
        v1_ref[HEAD_DIM:V1_ROWS, :] = jnp.ones((ONES_ROWS, T), BF16)

    acc_ref[...] = jnp.zeros(acc_ref.shape, F32)
    m_ref[...] = jnp.full(m_ref.shape, NEG, F32)
    spare = jnp.zeros((HEAD_DIM, tq), BF16)

    heads = range(GROUP)

    def scores(c, slot, hs=heads):
        kblk = k_ref[0, pl.ds(pl.multiple_of(c * kc, kc), kc), :]
        for g in hs:
            s_refs[slot][g] = jnp.dot(kblk, _head_weights(qt_ref, g, spare), preferred_element_type=F32)

    def softmax(slot, hs=heads):
        for g in hs:
            s = s_refs[slot][g]
            m_old = m_ref[g]
            m_new = jnp.maximum(m_old, jnp.max(s, axis=0, keepdims=True))
            p_refs[slot][g] = jnp.exp2(s - m_new).astype(BF16)
            a_refs[slot][g] = jnp.exp2(m_old - m_new)
            m_ref[g] = m_new

    def values(c, slot, hs=heads):
        v1 = v1_ref[:, pl.ds(pl.multiple_of(c * kc, kc), kc)]
        for g in hs:
            acc_ref[g] = (a_refs[slot][g] * acc_ref[g]
                          + jnp.dot(v1, p_refs[slot][g], preferred_element_type=F32))

    scores(0, 0)
    softmax(0)
    scores(1, 1)

    def body(j, carry):
        c = 2 * j
        for g in heads:
            scores(c, 0, (g,))
            softmax(1, (g,))
            values(c - 2, 0, (g,))
        for g in heads:
            scores(c + 1, 1, (g,))
            softmax(0, (g,))
            values(c - 1, 1, (g,))
        return carry

    lax.fori_loop(1, n_chunks // 2, body, 0)
    values(n_chunks - 2, 0)
    softmax(1)
    values(n_chunks - 1, 1)
    o_ref[0] = jnp.concatenate([_normalised(acc_ref[g]) for g in range(GROUP)], axis=0).T


def _attn_b(qt, k, vt, *, tq, kc):
    B, _, T = qt.shape
    tq = min(tq, T)
    kc = min(kc, T // 2)
    assert T % (2 * kc) == 0
    grid = (B, N_KV, T // tq)
    return pl.pallas_call(
        functools.partial(_attn_b_kernel, kc=kc),
        grid=grid,
        in_specs=[pl.BlockSpec((1, D_GRP, tq), lambda b, h, i: (b, h, i)),
                  pl.BlockSpec((1, T, LANES), lambda b, h, i: (b, 0, h)),
                  pl.BlockSpec((1, HEAD_DIM, T), lambda b, h, i: (b, h, 0))],
        out_specs=pl.BlockSpec((1, tq, D_GRP), lambda b, h, i: (b, i, h)),
        out_shape=jax.ShapeDtypeStruct((B, T, D_MIX), F32),
        scratch_shapes=[pltpu.VMEM((V1_ROWS, T), BF16),
                        pltpu.VMEM((GROUP, kc, tq), F32), pltpu.VMEM((GROUP, kc, tq), F32),
                        pltpu.VMEM((GROUP, kc, tq), BF16), pltpu.VMEM((GROUP, kc, tq), BF16),
                        pltpu.VMEM((GROUP, 1, tq), F32), pltpu.VMEM((GROUP, 1, tq), F32),
                        pltpu.VMEM((GROUP, V1_ROWS, tq), F32),
                        pltpu.VMEM((GROUP, 1, tq), F32)],
        compiler_params=_cparams(48, 3),
        name="attn_b",
    )(qt, k, vt)


def _attn_a_kernel(sink_ref, rb_ref, bucket_ref, qt_ref, k_ref, vt_ref, o_ref, kp_ref, v1_ref, bias_ref):
    T = k_ref.shape[1]
    tq = qt_ref.shape[2]
    kw = tq + 2 * WINDOW
    kvh = pl.program_id(1)
    qi = pl.program_id(2)

    @pl.when((pl.program_id(0) == 0) & (kvh == 0) & (qi == 0))
    def _():
        lane = lax.broadcasted_iota(jnp.int32, (WINDOW, LANES), 1)
        off_seq = (lane == HEAD_DIM).astype(BF16)
        kp_ref[0:WINDOW, :] = off_seq
        kp_ref[WINDOW + T:, :] = off_seq
        v1_ref[0:HEAD_DIM, 0:WINDOW] = jnp.zeros((HEAD_DIM, WINDOW), BF16)
        v1_ref[0:HEAD_DIM, WINDOW + T:] = jnp.zeros((HEAD_DIM, WINDOW), BF16)
        v1_ref[HEAD_DIM:V1_ROWS, :] = jnp.ones((ONES_ROWS, T + 2 * WINDOW), BF16)
        bucket = bucket_ref[...]
        for h in range(N_HEADS):
            bias = jnp.full((kw, tq), NEG, F32)
            for k in range(NUM_BUCKETS):
                bias = jnp.where(bucket == k, rb_ref[k, h], bias)
            bias_ref[h] = bias * LOG2E

    @pl.when(qi == 0)
    def _():
        kp_ref[WINDOW:WINDOW + T, :] = k_ref[0]
        v1_ref[0:HEAD_DIM, WINDOW:WINDOW + T] = vt_ref[0]

    start = pl.multiple_of(qi * tq, tq)
    kblk = kp_ref[pl.ds(start, kw), :]
    v1 = v1_ref[:, pl.ds(start, kw)]
    spare_row = lax.broadcasted_iota(jnp.int32, (HEAD_DIM, tq), 0)
    spare = jnp.where(spare_row == 0, NEG, 0.0).astype(BF16)
    def scores(g):
        return (jnp.dot(kblk, _head_weights(qt_ref, g, spare), preferred_element_type=F32)
                + bias_ref[kvh * GROUP + g])

    outs = []
    s_next = scores(0)
    for g in range(GROUP):
        s = s_next
        if g + 1 < GROUP:
            s_next = scores(g + 1)
        sink = jnp.full((1, tq), sink_ref[kvh * GROUP + g], F32) * LOG2E
        m = jnp.maximum(jnp.max(s, axis=0, keepdims=True), sink)
        pv = jnp.dot(v1, jnp.exp2(s - m).astype(BF16), preferred_element_type=F32)
        den = pv[HEAD_DIM:HEAD_DIM + 1] + jnp.exp2(sink - m)
        outs.append(pv[0:HEAD_DIM] * (1.0 / den))
    o_ref[0] = jnp.concatenate(outs, axis=0).T


def _attn_a(qt, k, vt, sink, rel_bias, bucket):
    B, _, T = qt.shape
    kw, tq = bucket.shape
    grid = (B, N_KV, T // tq)
    smem = pl.BlockSpec(memory_space=pltpu.SMEM)
    return pl.pallas_call(
        _attn_a_kernel,
        grid=grid,
        in_specs=[smem, smem, _const_spec((kw, tq)),
                  pl.BlockSpec((1, D_GRP, tq), lambda b, h, i: (b, h, i)),
                  pl.BlockSpec((1, T, LANES), lambda b, h, i: (b, 0, h)),
                  pl.BlockSpec((1, HEAD_DIM, T), lambda b, h, i: (b, h, 0))],
        out_specs=pl.BlockSpec((1, tq, D_GRP), lambda b, h, i: (b, i, h)),
        out_shape=jax.ShapeDtypeStruct((B, T, D_MIX), F32),
        scratch_shapes=[pltpu.VMEM((T + 2 * WINDOW, LANES), BF16),
                        pltpu.VMEM((V1_ROWS, T + 2 * WINDOW), BF16),
                        pltpu.VMEM((N_HEADS, kw, tq), F32)],
        compiler_params=_cparams(48, 3),
        name="attn_a",
    )(sink, rel_bias, bucket, qt, k, vt)


def _post_kernel(oa_ref, ob_ref, x_ref, ga_ref, gb_ref, wo_ref, gf_ref, wrh_ref, wrl_ref,
                 x1_ref, h2s_ref, lg_ref):
    tn = x_ref.shape[0]
    mixed = jnp.concatenate([_rms(oa_ref[...], ga_ref[...]), _rms(ob_ref[...], gb_ref[...])], axis=1)
    x1 = x_ref[...] + jnp.dot(mixed.astype(BF16), wo_ref[...], preferred_element_type=F32)
    x1_ref[...] = x1
    h2 = _rms(x1, gf_ref[...])
    for s in range(ROW_SLABS):
        h2s_ref[pl.ds(s, tn, stride=ROW_SLABS), :] = h2[:, s * LANES:(s + 1) * LANES]
    hi, lo = _split_bf16(h2)
    lg_ref[...] = (jnp.dot(hi, wrh_ref[...], preferred_element_type=F32)
                   + jnp.dot(lo, wrh_ref[...], preferred_element_type=F32)
                   + jnp.dot(hi, wrl_ref[...], preferred_element_type=F32))


def _post(oa, ob, x, ga, gb, wo, gf, wrh, wrl, *, tn):
    N = x.shape[0]
    tn = min(tn, N)
    tok = lambda w: pl.BlockSpec((tn, w), lambda j: (j, 0))
    return pl.pallas_call(
        _post_kernel,
        grid=(N // tn,),
        in_specs=[tok(D_MIX), tok(D_MIX), tok(D_MODEL), _const_spec((1, D_MIX)), _const_spec((1, D_MIX)),
                  _const_spec((D_MODEL, D_MODEL)), _const_spec((1, D_MODEL)),
                  _const_spec((D_MODEL, LANES)), _const_spec((D_MODEL, LANES))],
        out_specs=[tok(D_MODEL), pl.BlockSpec((ROW_SLABS * tn, LANES), lambda j: (j, 0)), tok(LANES)],
        out_shape=[jax.ShapeDtypeStruct((N, D_MODEL), F32),
                   jax.ShapeDtypeStruct((ROW_SLABS * N, LANES), F32),
                   jax.ShapeDtypeStruct((N, LANES), F32)],
        compiler_params=_cparams(48, 1),
        name="post",
    )(oa, ob, x, ga, gb, wo, gf, wrh, wrl)


def _select_kernel(lg_ref, idx_ref, gate_ref, pos_ref, bex_ref, *, cap, sc):
    e = pl.program_id(0)
    nb = lg_ref.shape[1]
    lg = lg_ref[...]
    mx = jnp.max(lg, axis=0)
    den = jnp.sum(jnp.exp(lg - mx), axis=0)
    aff = jnp.exp(lg_ref[e] - mx) / den

    def bisect(_, lohi):
        lo, hi = lohi
        mid = lo + jnp.right_shift(hi - lo, 1)
        thr = lax.bitcast_convert_type(mid, F32)
        cnt = jnp.sum((aff >= thr).astype(jnp.int32))
        ok = cnt >= cap
        return jnp.where(ok, mid, lo), jnp.where(ok, hi, mid)

    lo0 = jnp.zeros((1, 1), jnp.int32)
    hi0 = jnp.full((1, 1), 0x7F800000, jnp.int32)
    lo, hi = lax.fori_loop(0, 31, bisect, (lo0, hi0))
    above = aff >= lax.bitcast_convert_type(hi, F32)
    tie = (aff >= lax.bitcast_convert_type(lo, F32)) & jnp.logical_not(above)
    need = cap - jnp.sum(above.astype(jnp.int32))

    li = lax.broadcasted_iota(jnp.int32, (LANES, LANES), 0)
    lj = lax.broadcasted_iota(jnp.int32, (LANES, LANES), 1)
    ut_incl = (li <= lj).astype(BF16)
    ut_excl = (li < lj).astype(BF16)
    bi = lax.broadcasted_iota(jnp.int32, (nb, nb), 0)
    bj = lax.broadcasted_iota(jnp.int32, (nb, nb), 1)
    lt_incl = (bj <= bi).astype(BF16)
    lt_excl = (bj < bi).astype(BF16)

    def last(x):
        return x[:, LANES - 1:LANES]

    tie_b = tie.astype(BF16)
    tie_excl = jnp.dot(tie_b, ut_excl, preferred_element_type=F32)
    tie_incl = jnp.dot(tie_b, ut_incl, preferred_element_type=F32)
    tie_before = last(jnp.dot(lt_excl, tie_incl.astype(BF16), preferred_element_type=F32))
    tie_rank = (tie_before + tie_excl).astype(jnp.int32)
    sel = above | (tie & (tie_rank < need))

    lcum = jnp.dot(sel.astype(BF16), ut_incl, preferred_element_type=F32)
    lcum_b = lcum.astype(BF16)
    bcum_incl = last(jnp.dot(lt_incl, lcum_b, preferred_element_type=F32))
    bcum_excl = bcum_incl - last(lcum)
    pos_ref[0] = jnp.where(sel, (bcum_excl + lcum).astype(jnp.int32) - 1, -1)
    bex_ref[0] = bcum_excl.astype(jnp.int32)

    a1 = aff.astype(BF16)
    a2 = (aff - a1.astype(F32)).astype(BF16)
    a3 = (aff - a1.astype(F32) - a2.astype(F32)).astype(BF16)
    blk = lax.broadcasted_iota(jnp.int32, (nb, sc), 0)
    row = lax.broadcasted_iota(jnp.int32, (LANES, sc), 0)
    for c in range(cap // sc):
        slot = (c * sc + lax.broadcasted_iota(jnp.int32, (1, sc), 1)).astype(F32)
        hit = (bcum_excl <= slot) & (slot < bcum_incl)
        hit_b = hit.astype(BF16)
        base = jnp.sum(jnp.where(hit, bcum_excl, 0.0), axis=0, keepdims=True)
        blk_of = jnp.sum(jnp.where(hit, blk, 0), axis=0, keepdims=True)
        lc_t = lax.dot_general(lcum_b, hit_b, _TN, preferred_element_type=F32)
        k_in = slot - base
        j_of = jnp.sum((lc_t <= k_in).astype(jnp.int32), axis=0, keepdims=True)
        idx_ref[0, :, c * sc:(c + 1) * sc] = blk_of * LANES + j_of
        aff_t = (lax.dot_general(a1, hit_b, _TN, preferred_element_type=F32)
                 + lax.dot_general(a2, hit_b, _TN, preferred_element_type=F32)
                 + lax.dot_general(a3, hit_b, _TN, preferred_element_type=F32))
        gate_ref[0, :, c * sc:(c + 1) * sc] = jnp.sum(jnp.where(row == j_of, aff_t, 0.0), axis=0, keepdims=True)


def _select(lg3, *, cap, sc):
    E, nb, _ = lg3.shape
    sc = min(sc, cap)
    per_e = lambda r, c, dt: (pl.BlockSpec((1, r, c), lambda e: (e, 0, 0)), jax.ShapeDtypeStruct((E, r, c), dt))
    specs = [per_e(1, cap, jnp.int32), per_e(1, cap, F32), per_e(nb, LANES, jnp.int32), per_e(nb, 1, jnp.int32)]
    return pl.pallas_call(
        functools.partial(_select_kernel, cap=cap, sc=sc),
        grid=(E,),
        in_specs=[_const_spec((E, nb, LANES))],
        out_specs=[s for s, _ in specs],
        out_shape=[o for _, o in specs],
        compiler_params=_cparams(48, 1),
        name="select",
    )(lg3)


GATHER_UNROLL = 8


def _ffn_kernel(idx_ref, idxn_ref, gate_ref, h2s_ref, wg_ref, wu_ref, wd_ref, ye_ref, xbuf_ref, sem_ref):
    rows = ye_ref.shape[0]
    n = pl.program_id(0)
    n_steps = pl.num_programs(0)
    slot = n % 2

    def row_copy(ids_ref, r, to_slot):
        src = pl.multiple_of(ids_ref[0, 0, r] * ROW_SLABS, ROW_SLABS)
        return pltpu.make_async_copy(h2s_ref.at[pl.ds(src, ROW_SLABS), :],
                                     xbuf_ref.at[to_slot, pl.ds(r * ROW_SLABS, ROW_SLABS), :],
                                     sem_ref.at[to_slot])

    def start_gather(ids_ref, to_slot):
        def body(r0, carry):
            for u in range(GATHER_UNROLL):
                row_copy(ids_ref, r0 * GATHER_UNROLL + u, to_slot).start()
            return carry
        lax.fori_loop(0, rows // GATHER_UNROLL, body, 0)

    @pl.when(n == 0)
    def _():
        start_gather(idx_ref, 0)

    @pl.when(n + 1 < n_steps)
    def _():
        start_gather(idxn_ref, 1 - slot)

    pltpu.make_async_copy(h2s_ref.at[pl.ds(0, rows * ROW_SLABS), :], xbuf_ref.at[slot], sem_ref.at[slot]).wait()

    x = jnp.concatenate(
        [xbuf_ref[slot, pl.ds(s, rows, stride=ROW_SLABS), :].astype(BF16) for s in range(ROW_SLABS)], axis=1)
    a = jnp.dot(x, wg_ref[0], preferred_element_type=F32)
    u = jnp.dot(x, wu_ref[0], preferred_element_type=F32)
    hmid = (jax.nn.silu(a) * u).astype(BF16)
    y = jnp.dot(hmid, wd_ref[0], preferred_element_type=F32)
    ri = lax.broadcasted_iota(jnp.int32, (rows, rows), 0)
    ci = lax.broadcasted_iota(jnp.int32, (rows, rows), 1)
    gate_col = jnp.sum(jnp.where(ri == ci, gate_ref[0], 0.0), axis=1, keepdims=True)
    ye_ref[...] = (gate_col * y).astype(BF16)


def _ffn(idx3, gate3, h2s, wg, wu, wd, *, cap):
    n_steps, _, rows = idx3.shape
    per_e = cap // rows
    last = n_steps - 1
    return pl.pallas_call(
        _ffn_kernel,
        grid=(n_steps,),
        in_specs=[pl.BlockSpec((1, 1, rows), lambda n: (n, 0, 0), memory_space=pltpu.SMEM),
                  pl.BlockSpec((1, 1, rows), lambda n: (jnp.minimum(n + 1, last), 0, 0), memory_space=pltpu.SMEM),
                  pl.BlockSpec((1, 1, rows), lambda n: (n, 0, 0)),
                  pl.BlockSpec(memory_space=pl.ANY),
                  pl.BlockSpec((1, D_MODEL, D_EXPERT), lambda n: (n // per_e, 0, 0)),
                  pl.BlockSpec((1, D_MODEL, D_EXPERT), lambda n: (n // per_e, 0, 0)),
                  pl.BlockSpec((1, D_EXPERT, D_MODEL), lambda n: (n // per_e, 0, 0))],
        out_specs=pl.BlockSpec((rows, D_MODEL), lambda n: (n, 0)),
        out_shape=jax.ShapeDtypeStruct((n_steps * rows, D_MODEL), BF16),
        scratch_shapes=[pltpu.VMEM((2, rows * ROW_SLABS, LANES), F32), pltpu.SemaphoreType.DMA((2,))],
        compiler_params=_cparams(48, 1),
        name="ffn",
    )(idx3, idx3, gate3, h2s, wg, wu, wd)


BF16_ROWS = 16
MXU_DEPTH = 256


def _combine_kernel(tab_ref, x1_ref, pos_ref, p_ref, ye_ref, wpg_ref, wpp_ref, gp_ref, gfin_ref,
                    y_ref, win_ref, sem_ref, more_ref, msem_ref, acc_ref, *, cap, win):
    tt = x1_ref.shape[0]
    i = pl.program_id(0)
    n_tiles = pl.num_programs(0)
    slot = i % 2
    total = N_EXPERTS * cap
    per_dot = MXU_DEPTH // win
    wi = lax.broadcasted_iota(jnp.int32, (win, tt), 0)

    def first_row(e, tile):
        first = e * cap + tab_ref[e, tile]
        aligned = lax.shift_left(lax.shift_right_logical(first, 4), 4)
        return jnp.minimum(aligned, total - win)

    def fetch_windows(tile, to_slot):
        for e in range(N_EXPERTS):
            a0 = pl.multiple_of(first_row(e, tile), BF16_ROWS)
            pltpu.make_async_copy(ye_ref.at[pl.ds(a0, win), :], win_ref.at[to_slot, pl.ds(e * win, win), :],
                                  sem_ref.at[to_slot]).start()

    @pl.when(i == 0)
    def _():
        fetch_windows(0, 0)

    @pl.when(i + 1 < n_tiles)
    def _():
        fetch_windows(i + 1, 1 - slot)

    pltpu.make_async_copy(ye_ref.at[pl.ds(0, N_EXPERTS * win), :], win_ref.at[slot], sem_ref.at[slot]).wait()

    def onehot(e, a0):
        pos = pos_ref[e:e + 1, :]
        return (((pos + (e * cap - a0)) == wi) & (pos >= 0)).astype(BF16)

    acc = None
    for j in range(N_EXPERTS // per_dot):
        es = range(j * per_dot, (j + 1) * per_dot)
        hot = jnp.concatenate([onehot(e, first_row(e, i)) for e in es], axis=0)
        part = lax.dot_general(hot, win_ref[slot, j * MXU_DEPTH:(j + 1) * MXU_DEPTH, :], _TN,
                               preferred_element_type=F32)
        acc = part if acc is None else acc + part
    acc_ref[...] = acc

    for e in range(N_EXPERTS):
        a0 = first_row(e, i)
        end = e * cap + tab_ref[e, i + 1]
        n_windows = jnp.maximum((end - a0 + win - 1) // win, 1)

        def more(c, carry, e=e, a0=a0):
            a = pl.multiple_of(jnp.minimum(a0 + c * win, total - win), BF16_ROWS)
            copy = pltpu.make_async_copy(ye_ref.at[pl.ds(a, win), :], more_ref, msem_ref.at[0])
            copy.start()
            copy.wait()
            acc_ref[...] += lax.dot_general(onehot(e, a), more_ref[...], _TN, preferred_element_type=F32)
            return carry
        lax.fori_loop(1, n_windows, more, 0)

    x2 = x1_ref[...] + acc_ref[...]
    gate = jax.nn.sigmoid(jnp.dot(_rms(x2, gp_ref[...]).astype(BF16), wpg_ref[...], preferred_element_type=F32))
    x3 = x2 + gate * jnp.dot(p_ref[...].astype(BF16), wpp_ref[...], preferred_element_type=F32)
    y_ref[...] = _rms(x3, gfin_ref[...])


def _combine(tab, x1, pos, p, ye, wpg, wpp, gp, gfin, *, cap, tt, win):
    N = x1.shape[0]
    grid_spec = pltpu.PrefetchScalarGridSpec(
        num_scalar_prefetch=1,
        grid=(N // tt,),
        in_specs=[pl.BlockSpec((tt, D_MODEL), lambda i, t: (i, 0)),
                  pl.BlockSpec((N_EXPERTS, tt), lambda i, t: (0, i)),
                  pl.BlockSpec((tt, D_PLE), lambda i, t: (i, 0)),
                  pl.BlockSpec(memory_space=pl.ANY),
                  pl.BlockSpec((D_MODEL, D_MODEL), lambda i, t: (0, 0)),
                  pl.BlockSpec((D_PLE, D_MODEL), lambda i, t: (0, 0)),
                  pl.BlockSpec((1, D_MODEL), lambda i, t: (0, 0)),
                  pl.BlockSpec((1, D_MODEL), lambda i, t: (0, 0))],
        out_specs=pl.BlockSpec((tt, D_MODEL), lambda i, t: (i, 0)),
        scratch_shapes=[pltpu.VMEM((2, N_EXPERTS * win, D_MODEL), BF16), pltpu.SemaphoreType.DMA((2,)),
                        pltpu.VMEM((win, D_MODEL), BF16), pltpu.SemaphoreType.DMA((1,)),
                        pltpu.VMEM((tt, D_MODEL), F32)],
    )
    return pl.pallas_call(
        functools.partial(_combine_kernel, cap=cap, win=win),
        grid_spec=grid_spec,
        out_shape=jax.ShapeDtypeStruct((N, D_MODEL), F32),
        compiler_params=_cparams(48, 1),
        name="combine",
    )(tab, x1, pos, p, ye, wpg, wpp, gp, gfin)


def _t5_bucket(rel):
    nb = NUM_BUCKETS // 2
    ret = jnp.where(rel > 0, nb, 0)
    n = jnp.abs(rel)
    max_exact = nb // 2
    large = max_exact + (jnp.log(jnp.maximum(n, 1).astype(F32) / max_exact)
                         / math.log(MAX_DISTANCE / max_exact) * (nb - max_exact)).astype(jnp.int32)
    large = jnp.minimum(large, nb - 1)
    return ret + jnp.where(n < max_exact, n, large)


def _bucket_table(tq):
    kw = tq + 2 * WINDOW
    rel = jnp.arange(kw)[:, None] - WINDOW - jnp.arange(tq)[None, :]
    return jnp.where(jnp.abs(rel) <= WINDOW, _t5_bucket(rel), -1).astype(jnp.int32)


def _rope_tables(T):
    rows = T // GRID_W
    row = jnp.repeat(jnp.arange(rows), GRID_W).astype(F32)
    col = jnp.tile(jnp.arange(GRID_W), rows).astype(F32)
    freqs = ROPE_THETA ** (-jnp.arange(AX_PAIRS, dtype=F32) / AX_PAIRS)
    ang = jnp.concatenate([row[:, None] * freqs, col[:, None] * freqs], axis=-1)
    cos = jnp.repeat(jnp.cos(ang), 2, axis=1)
    sin = jnp.repeat(jnp.sin(ang), 2, axis=1) * jnp.tile(jnp.array([-1.0, 1.0], F32), HEAD_DIM // 2)
    return cos.T, sin.T, jnp.tile(cos, (1, N_KV)), jnp.tile(sin, (1, N_KV))


def _prep_params(g_attn, w_in, gq_b, gk_b, w_out, w_router, w_gate, w_up, w_down, w_ple_gate, w_ple_proj):
    w = w_in[0]
    o = np.cumsum([0, D_MIX, D_KV, D_KV, D_MIX, D_KV, D_KV])
    wqa, wka, wva, wqb, wkb, wvb = (w[:, o[i]:o[i + 1]] for i in range(6))
    swap_q = jnp.arange(D_MIX) ^ 1
    swap_k = jnp.arange(D_KV) ^ 1
    swap_h = jnp.arange(HEAD_DIM) ^ 1
    def spread(a):
        z = jnp.zeros(a.shape[:-1] + (LANES - HEAD_DIM,), a.dtype)
        return jnp.concatenate([a[..., :HEAD_DIM], z, a[..., HEAD_DIM:], z], axis=-1)

    w1 = jnp.concatenate([spread(wka), spread(wkb), spread(wkb[:, swap_k])], axis=1).astype(BF16)
    w2t = jnp.concatenate([wqa.T, wqb.T, wqb[:, swap_q].T, wva.T, wvb.T], axis=0).astype(BF16)
    hd = jnp.arange(D_KX) // HEAD_DIM
    bd = (hd[:, None] == hd[None, :]).astype(BF16)
    gq = gq_b[0] * QSCALE
    gk = gk_b[0]
    wr = jnp.pad(w_router[0], ((0, 0), (0, LANES - N_EXPERTS)))
    wrh = wr.astype(BF16)
    wrl = (wr - wrh.astype(F32)).astype(BF16)
    return dict(
        g_attn=g_attn[0][None], w1=w1, w2t=w2t, bd=bd,
        gq=jnp.broadcast_to(gq[:, None], (HEAD_DIM, LANES)),
        gqs=jnp.broadcast_to(gq[swap_h][:, None], (HEAD_DIM, LANES)),
        gk=spread(jnp.tile(gk, N_KV))[None], gks=spread(jnp.tile(gk[swap_h], N_KV))[None],
        wo=w_out[0].astype(BF16), wrh=wrh, wrl=wrl,
        wg=w_gate[0].astype(BF16), wu=w_up[0].astype(BF16), wd=w_down[0].astype(BF16),
        wpg=w_ple_gate[0].astype(BF16), wpp=w_ple_proj[0].astype(BF16))


TOKEN_TILE = 512
WINDOW_Q_TILE = 256
DENSE_Q_TILE = 512
DENSE_KEY_CHUNK = 512
SELECT_SLOT_CHUNK = 1024
FFN_ROWS = 512
COMBINE_TILE = 256
COMBINE_WINDOW = 64


def _trunk(x, p, prm, sink, rel_bias, g_out_a, g_out_b, g_ffn, g_ple, g_final):
    B, T, _ = x.shape
    N = B * T
    cap = CAPACITY_FACTOR * N // N_EXPERTS
    qat, qbt, vat, vbt, ka, kb = _qkv(x, prm, *_rope_tables(T), tn=TOKEN_TILE)
    oa = _attn_a(qat, ka, vat, sink, rel_bias, _bucket_table(min(WINDOW_Q_TILE, T)))
    ob = _attn_b(qbt, kb, vbt, tq=DENSE_Q_TILE, kc=DENSE_KEY_CHUNK)
    x1, h2s, lg = _post(oa.reshape(N, D_MIX), ob.reshape(N, D_MIX), x.reshape(N, D_MODEL),
                        g_out_a, g_out_b, prm["wo"], g_ffn, prm["wrh"], prm["wrl"], tn=TOKEN_TILE)
    lg3 = lg[:, :N_EXPERTS].T.reshape(N_EXPERTS, N // LANES, LANES)
    idx, gate, pos, bex = _select(lg3, cap=cap, sc=SELECT_SLOT_CHUNK)
    rows = min(FFN_ROWS, cap)
    ye = _ffn(idx.reshape(-1, 1, rows), gate.reshape(-1, 1, rows), h2s, prm["wg"], prm["wu"], prm["wd"], cap=cap)
    tt = min(COMBINE_TILE, N)
    win = min(COMBINE_WINDOW, cap)
    tile_start = bex.reshape(N_EXPERTS, N // LANES)[:, ::tt // LANES]
    tab = jnp.concatenate([tile_start, jnp.full((N_EXPERTS, 1), cap, jnp.int32)], axis=1)
    y = _combine(tab, x1, pos.reshape(N_EXPERTS, N), p.reshape(N, D_PLE), ye, prm["wpg"], prm["wpp"],
                 g_ple, g_final, cap=cap, tt=tt, win=win)
    return y.reshape(B, T, D_MODEL)


def kernel(x_prompt, x_sample, p_prompt, p_sample, g_attn, w_in, sink_a, rel_bias, gq_b, gk_b, g_out_a, g_out_b, w_out, g_ffn, w_router, w_gate, w_up, w_down, g_ple, w_ple_gate, w_ple_proj, g_final):
    prm = _prep_params(g_attn, w_in, gq_b, gk_b, w_out, w_router, w_gate, w_up, w_down, w_ple_gate, w_ple_proj)
    args = (prm, sink_a[0], rel_bias, g_out_a[0][None], g_out_b[0][None], g_ffn[0][None], g_ple[0][None],
            g_final[None])
    y_prompt = _trunk(x_prompt, p_prompt[0], *args)
    y_sample = _trunk(x_sample, p_sample[0], *args)
    return (y_prompt, y_sample)
```

```python
import functools
import math

import jax
import jax.numpy as jnp
import numpy as np
from jax import lax
from jax.experimental import pallas as pl
from jax.experimental.pallas import tpu as pltpu

D_MODEL = 1024
HEAD_DIM = 64
N_HEADS = 8
N_KV = 2
GROUP = N_HEADS // N_KV
D_MIX = N_HEADS * HEAD_DIM
D_KV = N_KV * HEAD_DIM
D_GRP = GROUP * HEAD_DIM
WINDOW = 128
NUM_BUCKETS = 32
MAX_DISTANCE = 128
GRID_W = 64
ROPE_THETA = 10000.0
AX_PAIRS = HEAD_DIM // 4
N_EXPERTS = 16
CAPACITY_FACTOR = 2
D_EXPERT = 512
D_PLE = 256
EPS = 1e-6
NEG = -1e30
LOG2E = math.log2(math.e)
QSCALE = HEAD_DIM ** -0.5 * LOG2E

LANES = 128
SUBLANES = 8
ROW_SLABS = D_MODEL // LANES

F32 = jnp.float32
BF16 = jnp.bfloat16

_NT = (((1,), (1,)), ((), ()))
_TN = (((0,), (0,)), ((), ()))


def _cparams(mib, n_axes, flags=None):
    return pltpu.CompilerParams(vmem_limit_bytes=mib * 1024 * 1024,
                                dimension_semantics=("arbitrary",) * n_axes, flags=flags)


def _const_spec(shape):
    zeros = (0,) * len(shape)
    return pl.BlockSpec(shape, lambda *_: zeros)


def _rms(x, g):
    r = lax.rsqrt(jnp.mean(x * x, axis=-1, keepdims=True) + EPS)
    return (x * r) * g


def _split_bf16(x):
    hi = x.astype(BF16)
    lo = (x - hi.astype(F32)).astype(BF16)
    return hi, lo


_Q_ROWS = 3 * D_MIX
D_KX = N_KV * LANES


def _qkv_kernel(x_ref, g_ref, w1_ref, w2t_ref, bd_ref, gq_ref, gqs_ref, gk_ref, gks_ref,
                ct_ref, st_ref, ck_ref, sk_ref,
                qat_ref, qbt_ref, vat_ref, vbt_ref, ka_ref, kb_ref):
    tn = x_ref.shape[1]
    h = _rms(x_ref[0], g_ref[...]).astype(BF16)
    p1 = jnp.dot(h, w1_ref[...], preferred_element_type=F32)
    p2 = lax.dot_general(w2t_ref[...], h, _NT, preferred_element_type=F32)

    qat_ref[0] = (p2[0:D_MIX] * QSCALE).astype(BF16)
    vat_ref[0] = p2[_Q_ROWS:_Q_ROWS + D_KV].astype(BF16)
    vbt_ref[0] = p2[_Q_ROWS + D_KV:_Q_ROWS + 2 * D_KV].astype(BF16)
    ka_ref[0] = p1[:, 0:D_KX].astype(BF16)

    q_raw = p2[D_MIX:2 * D_MIX].reshape(N_HEADS, HEAD_DIM, tn)
    q_swp = p2[2 * D_MIX:3 * D_MIX].reshape(N_HEADS, HEAD_DIM, tn)
    rq = lax.rsqrt(jnp.sum(q_raw * q_raw, axis=1, keepdims=True) * (1.0 / HEAD_DIM) + EPS)
    lane_reps = tn // LANES
    gq = jnp.concatenate([gq_ref[...]] * lane_reps, axis=1)[None]
    gqs = jnp.concatenate([gqs_ref[...]] * lane_reps, axis=1)[None]
    qn = (q_raw * rq) * gq
    qs = (q_swp * rq) * gqs
    qbt_ref[0] = (qn * ct_ref[...][None] + qs * st_ref[...][None]).reshape(D_MIX, tn).astype(BF16)

    k_raw = p1[:, D_KX:2 * D_KX]
    k_swp = p1[:, 2 * D_KX:3 * D_KX]
    hi, lo = _split_bf16(k_raw * k_raw)
    ssq = (jnp.dot(hi, bd_ref[...], preferred_element_type=F32)
           + jnp.dot(lo, bd_ref[...], preferred_element_type=F32))
    rk = lax.rsqrt(ssq * (1.0 / HEAD_DIM) + EPS)
    ck = jnp.concatenate([ck_ref[...]] * N_KV, axis=1)
    sk = jnp.concatenate([sk_ref[...]] * N_KV, axis=1)
    kn = (k_raw * rk) * gk_ref[...]
    ks = (k_swp * rk) * gks_ref[...]
    kb_ref[0] = (kn * ck + ks * sk).astype(BF16)


def _qkv(x, prm, ct, st, ck, sk, *, tn):
    B, T, _ = x.shape
    tn = min(tn, T)
    grid = (B, T // tn)
    tok = lambda w: pl.BlockSpec((1, tn, w), lambda b, j: (b, j, 0))
    feat = lambda r: pl.BlockSpec((1, r, tn), lambda b, j: (b, 0, j))
    return pl.pallas_call(
        _qkv_kernel,
        grid=grid,
        in_specs=[tok(D_MODEL), _const_spec((1, D_MODEL)), _const_spec((D_MODEL, 3 * D_KX)),
                  _const_spec((_Q_ROWS + 2 * D_KV, D_MODEL)), _const_spec((D_KX, D_KX)),
                  _const_spec((HEAD_DIM, LANES)), _const_spec((HEAD_DIM, LANES)),
                  _const_spec((1, D_KX)), _const_spec((1, D_KX)),
                  pl.BlockSpec((HEAD_DIM, tn), lambda b, j: (0, j)),
                  pl.BlockSpec((HEAD_DIM, tn), lambda b, j: (0, j)),
                  pl.BlockSpec((tn, LANES), lambda b, j: (j, 0)),
                  pl.BlockSpec((tn, LANES), lambda b, j: (j, 0))],
        out_specs=[feat(D_MIX), feat(D_MIX), feat(D_KV), feat(D_KV), tok(D_KX), tok(D_KX)],
        out_shape=[jax.ShapeDtypeStruct((B, D_MIX, T), BF16), jax.ShapeDtypeStruct((B, D_MIX, T), BF16),
                   jax.ShapeDtypeStruct((B, D_KV, T), BF16), jax.ShapeDtypeStruct((B, D_KV, T), BF16),
                   jax.ShapeDtypeStruct((B, T, D_KX), BF16), jax.ShapeDtypeStruct((B, T, D_KX), BF16)],
        compiler_params=_cparams(48, 2),
        name="qkv",
    )(x, prm["g_attn"], prm["w1"], prm["w2t"], prm["bd"], prm["gq"], prm["gqs"], prm["gk"], prm["gks"],
      ct, st, ck, sk)


ONES_ROWS = 16
V1_ROWS = HEAD_DIM + ONES_ROWS


def _head_weights(qt_ref, g, extra):
    return jnp.concatenate([qt_ref[0, g * HEAD_DIM:(g + 1) * HEAD_DIM, :], extra], axis=0)


def _normalised(acc):
    return acc[0:HEAD_DIM] * (1.0 / acc[HEAD_DIM:HEAD_DIM + 1])


def _attn_b_kernel(qt_ref, k_ref, vt_ref, o_ref, v1_ref, s0_ref, s1_ref, p0_ref, p1_ref, a0_ref, a1_ref,
                   acc_ref, m_ref, *, kc):
    s_refs, p_refs, a_refs = (s0_ref, s1_ref), (p0_ref, p1_ref), (a0_ref, a1_ref)
    T = k_ref.shape[1]
    tq = qt_ref.shape[2]
    n_chunks = T // kc

    @pl.when(pl.program_id(2) == 0)
    def _():
        v1_ref[0:HEAD_DIM, :] = vt_ref[0]
        v1_ref[HEAD_DIM:V1_ROWS, :] = jnp.ones((ONES_ROWS, T), BF16)

    acc_ref[...] = jnp.zeros(acc_ref.shape, F32)
    m_ref[...] = jnp.full(m_ref.shape, NEG, F32)
    spare = jnp.zeros((HEAD_DIM, tq), BF16)

    heads = range(GROUP)

    def scores(c, slot, hs=heads):
        kblk = k_ref[0, pl.ds(pl.multiple_of(c * kc, kc), kc), :]
        for g in hs:
            s_refs[slot][g] = jnp.dot(kblk, _head_weights(qt_ref, g, spare), preferred_element_type=F32)

    def softmax(slot, hs=heads):
        for g in hs:
            s = s_refs[slot][g]
            m_old = m_ref[g]
            m_new = jnp.maximum(m_old, jnp.max(s, axis=0, keepdims=True))
            p_refs[slot][g] = jnp.exp2(s - m_new).astype(BF16)
            a_refs[slot][g] = jnp.exp2(m_old - m_new)
            m_ref[g] = m_new

    def values(c, slot, hs=heads):
        v1 = v1_ref[:, pl.ds(pl.multiple_of(c * kc, kc), kc)]
        for g in hs:
            acc_ref[g] = (a_refs[slot][g] * acc_ref[g]
                          + jnp.dot(v1, p_refs[slot][g], preferred_element_type=F32))

    scores(0, 0)
    softmax(0)
    scores(1, 1)

    def body(j, carry):
        c = 2 * j
        for g in heads:
            scores(c, 0, (g,))
            softmax(1, (g,))
            values(c - 2, 0, (g,))
        for g in heads:
            scores(c + 1, 1, (g,))
            softmax(0, (g,))
            values(c - 1, 1, (g,))
        return carry

    lax.fori_loop(1, n_chunks // 2, body, 0)
    values(n_chunks - 2, 0)
    softmax(1)
    values(n_chunks - 1, 1)
    o_ref[0] = jnp.concatenate([_normalised(acc_ref[g]) for g in range(GROUP)], axis=0).T


def _attn_b(qt, k, vt, *, tq, kc):
    B, _, T = qt.shape
    tq = min(tq, T)
    kc = min(kc, T // 2)
    assert T % (2 * kc) == 0
    grid = (B, N_KV, T // tq)
    return pl.pallas_call(
        functools.partial(_attn_b_kernel, kc=kc),
        grid=grid,
        in_specs=[pl.BlockSpec((1, D_GRP, tq), lambda b, h, i: (b, h, i)),
                  pl.BlockSpec((1, T, LANES), lambda b, h, i: (b, 0, h)),
                  pl.BlockSpec((1, HEAD_DIM, T), lambda b, h, i: (b, h, 0))],
        out_specs=pl.BlockSpec((1, tq, D_GRP), lambda b, h, i: (b, i, h)),
        out_shape=jax.ShapeDtypeStruct((B, T, D_MIX), F32),
        scratch_shapes=[pltpu.VMEM((V1_ROWS, T), BF16),
                        pltpu.VMEM((GROUP, kc, tq), F32), pltpu.VMEM((GROUP, kc, tq), F32),
                        pltpu.VMEM((GROUP, kc, tq), BF16), pltpu.VMEM((GROUP, kc, tq), BF16),
                        pltpu.VMEM((GROUP, 1, tq), F32), pltpu.VMEM((GROUP, 1, tq), F32),
                        pltpu.VMEM((GROUP, V1_ROWS, tq), F32),
                        pltpu.VMEM((GROUP, 1, tq), F32)],
        compiler_params=_cparams(48, 3),
        name="attn_b",
    )(qt, k, vt)


def _attn_a_kernel(sink_ref, rb_ref, bucket_ref, qt_ref, k_ref, vt_ref, o_ref, kp_ref, v1_ref, bias_ref):
    T = k_ref.shape[1]
    tq = qt_ref.shape[2]
    kw = tq + 2 * WINDOW
    kvh = pl.program_id(1)
    qi = pl.program_id(2)

    @pl.when((pl.program_id(0) == 0) & (kvh == 0) & (qi == 0))
    def _():
        lane = lax.broadcasted_iota(jnp.int32, (WINDOW, LANES), 1)
        off_seq = (lane == HEAD_DIM).astype(BF16)
        kp_ref[0:WINDOW, :] = off_seq
        kp_ref[WINDOW + T:, :] = off_seq
        v1_ref[0:HEAD_DIM, 0:WINDOW] = jnp.zeros((HEAD_DIM, WINDOW), BF16)
        v1_ref[0:HEAD_DIM, WINDOW + T:] = jnp.zeros((HEAD_DIM, WINDOW), BF16)
        v1_ref[HEAD_DIM:V1_ROWS, :] = jnp.ones((ONES_ROWS, T + 2 * WINDOW), BF16)
        bucket = bucket_ref[...]
        for h in range(N_HEADS):
            bias = jnp.full((kw, tq), NEG, F32)
            for k in range(NUM_BUCKETS):
                bias = jnp.where(bucket == k + 1, rb_ref[k, h], bias)
            bias_ref[h] = bias * LOG2E

    @pl.when(qi == 0)
    def _():
        kp_ref[WINDOW:WINDOW + T, :] = k_ref[0]
        v1_ref[0:HEAD_DIM, WINDOW:WINDOW + T] = vt_ref[0]

    start = pl.multiple_of(qi * tq, tq)
    kblk = kp_ref[pl.ds(start, kw), :]
    v1 = v1_ref[:, pl.ds(start, kw)]
    spare_row = lax.broadcasted_iota(jnp.int32, (HEAD_DIM, tq), 0)
    spare = jnp.where(spare_row == 0, NEG, 0.0).astype(BF16)
    def scores(g):
        return (jnp.dot(kblk, _head_weights(qt_ref, g, spare), preferred_element_type=F32)
                + bias_ref[kvh * GROUP + g])

    outs = []
    s_next = scores(0)
    for g in range(GROUP):
        s = s_next
        if g + 1 < GROUP:
            s_next = scores(g + 1)
        sink = jnp.full((1, tq), sink_ref[kvh * GROUP + g], F32) * LOG2E
        m = jnp.maximum(jnp.max(s, axis=0, keepdims=True), sink)
        pv = jnp.dot(v1, jnp.exp2(s - m).astype(BF16), preferred_element_type=F32)
        den = pv[HEAD_DIM:HEAD_DIM + 1] + jnp.exp2(sink - m)
        outs.append(pv[0:HEAD_DIM] * (1.0 / den))
    o_ref[0] = jnp.concatenate(outs, axis=0).T


def _attn_a(qt, k, vt, sink, rel_bias, bucket):
    B, _, T = qt.shape
    kw, tq = bucket.shape
    grid = (B, N_KV, T // tq)
    smem = pl.BlockSpec(memory_space=pltpu.SMEM)
    return pl.pallas_call(
        _attn_a_kernel,
        grid=grid,
        in_specs=[smem, smem, _const_spec((kw, tq)),
                  pl.BlockSpec((1, D_GRP, tq), lambda b, h, i: (b, h, i)),
                  pl.BlockSpec((1, T, LANES), lambda b, h, i: (b, 0, h)),
                  pl.BlockSpec((1, HEAD_DIM, T), lambda b, h, i: (b, h, 0))],
        out_specs=pl.BlockSpec((1, tq, D_GRP), lambda b, h, i: (b, i, h)),
        out_shape=jax.ShapeDtypeStruct((B, T, D_MIX), F32),
        scratch_shapes=[pltpu.VMEM((T + 2 * WINDOW, LANES), BF16),
                        pltpu.VMEM((V1_ROWS, T + 2 * WINDOW), BF16),
                        pltpu.VMEM((N_HEADS, kw, tq), F32)],
        compiler_params=_cparams(48, 3),
        name="attn_a",
    )(sink, rel_bias, bucket, qt, k, vt)


def _post_kernel(oa_ref, ob_ref, x_ref, ga_ref, gb_ref, wo_ref, gf_ref, wr_ref,
                 x1_ref, h2s_ref, lg_ref):
    tn = x_ref.shape[0]
    mixed = jnp.concatenate([_rms(oa_ref[...], ga_ref[...]), _rms(ob_ref[...], gb_ref[...])], axis=1)
    x1 = x_ref[...] + jnp.dot(mixed.astype(BF16), wo_ref[...], preferred_element_type=F32)
    x1_ref[...] = x1
    h2 = _rms(x1, gf_ref[...])
    for s in range(ROW_SLABS):
        h2s_ref[pl.ds(s, tn, stride=ROW_SLABS), :] = h2[:, s * LANES:(s + 1) * LANES]
    lg_ref[...] = jnp.dot(h2.astype(BF16), wr_ref[...], preferred_element_type=F32)


def _post(oa, ob, x, ga, gb, wo, gf, wr, *, tn):
    N = x.shape[0]
    tn = min(tn, N)
    tok = lambda w: pl.BlockSpec((tn, w), lambda j: (j, 0))
    return pl.pallas_call(
        _post_kernel,
        grid=(N // tn,),
        in_specs=[tok(D_MIX), tok(D_MIX), tok(D_MODEL), _const_spec((1, D_MIX)), _const_spec((1, D_MIX)),
                  _const_spec((D_MODEL, D_MODEL)), _const_spec((1, D_MODEL)),
                  _const_spec((D_MODEL, LANES))],
        out_specs=[tok(D_MODEL), pl.BlockSpec((ROW_SLABS * tn, LANES), lambda j: (j, 0)), tok(LANES)],
        out_shape=[jax.ShapeDtypeStruct((N, D_MODEL), F32),
                   jax.ShapeDtypeStruct((ROW_SLABS * N, LANES), F32),
                   jax.ShapeDtypeStruct((N, LANES), F32)],
        compiler_params=_cparams(48, 1),
        name="post",
    )(oa, ob, x, ga, gb, wo, gf, wr)


def _select_kernel(lg_ref, idx_ref, gate_ref, pos_ref, bex_ref, *, cap, sc):
    e = pl.program_id(0)
    nb = lg_ref.shape[1]
    lg = lg_ref[...]
    mx = jnp.max(lg, axis=0)
    den = jnp.sum(jnp.exp(lg - mx), axis=0)
    aff = jnp.exp(lg_ref[e] - mx) / den

    def bisect(_, lohi):
        lo, hi = lohi
        mid = lo + jnp.right_shift(hi - lo, 1)
        thr = lax.bitcast_convert_type(mid, F32)
        cnt = jnp.sum((aff >= thr).astype(jnp.int32))
        ok = cnt >= cap
        return jnp.where(ok, mid, lo), jnp.where(ok, hi, mid)

    lo0 = jnp.zeros((1, 1), jnp.int32)
    hi0 = jnp.full((1, 1), 0x7F800000, jnp.int32)
    lo, hi = lax.fori_loop(0, 31, bisect, (lo0, hi0))
    above = aff >= lax.bitcast_convert_type(hi, F32)
    tie = (aff >= lax.bitcast_convert_type(lo, F32)) & jnp.logical_not(above)
    need = cap - jnp.sum(above.astype(jnp.int32))

    li = lax.broadcasted_iota(jnp.int32, (LANES, LANES), 0)
    lj = lax.broadcasted_iota(jnp.int32, (LANES, LANES), 1)
    ut_incl = (li <= lj).astype(BF16)
    ut_excl = (li < lj).astype(BF16)
    bi = lax.broadcasted_iota(jnp.int32, (nb, nb), 0)
    bj = lax.broadcasted_iota(jnp.int32, (nb, nb), 1)
    lt_incl = (bj <= bi).astype(BF16)
    lt_excl = (bj < bi).astype(BF16)

    def last(x):
        return x[:, LANES - 1:LANES]

    tie_b = tie.astype(BF16)
    tie_excl = jnp.dot(tie_b, ut_excl, preferred_element_type=F32)
    tie_incl = jnp.dot(tie_b, ut_incl, preferred_element_type=F32)
    tie_before = last(jnp.dot(lt_excl, tie_incl.astype(BF16), preferred_element_type=F32))
    tie_rank = (tie_before + tie_excl).astype(jnp.int32)
    sel = above | (tie & (tie_rank < need))

    lcum = jnp.dot(sel.astype(BF16), ut_incl, preferred_element_type=F32)
    lcum_b = lcum.astype(BF16)
    bcum_incl = last(jnp.dot(lt_incl, lcum_b, preferred_element_type=F32))
    bcum_excl = bcum_incl - last(lcum)
    pos_ref[0] = jnp.where(sel, (bcum_excl + lcum).astype(jnp.int32) - 1, -1)
    bex_ref[0] = bcum_excl.astype(jnp.int32)

    a1 = aff.astype(BF16)
    a2 = (aff - a1.astype(F32)).astype(BF16)
    a3 = (aff - a1.astype(F32) - a2.astype(F32)).astype(BF16)
    blk = lax.broadcasted_iota(jnp.int32, (nb, sc), 0)
    row = lax.broadcasted_iota(jnp.int32, (LANES, sc), 0)
    for c in range(cap // sc):
        slot = (c * sc + lax.broadcasted_iota(jnp.int32, (1, sc), 1)).astype(F32)
        hit = (bcum_excl <= slot) & (slot < bcum_incl)
        hit_b = hit.astype(BF16)
        base = jnp.sum(jnp.where(hit, bcum_excl, 0.0), axis=0, keepdims=True)
        blk_of = jnp.sum(jnp.where(hit, blk, 0), axis=0, keepdims=True)
        lc_t = lax.dot_general(lcum_b, hit_b, _TN, preferred_element_type=F32)
        k_in = slot - base
        j_of = jnp.sum((lc_t <= k_in).astype(jnp.int32), axis=0, keepdims=True)
        idx_ref[0, :, c * sc:(c + 1) * sc] = blk_of * LANES + j_of
        aff_t = (lax.dot_general(a1, hit_b, _TN, preferred_element_type=F32)
                 + lax.dot_general(a2, hit_b, _TN, preferred_element_type=F32)
                 + lax.dot_general(a3, hit_b, _TN, preferred_element_type=F32))
        gate_ref[0, :, c * sc:(c + 1) * sc] = jnp.sum(jnp.where(row == j_of, aff_t, 0.0), axis=0, keepdims=True)


def _select(lg3, *, cap, sc):
    E, nb, _ = lg3.shape
    sc = min(sc, cap)
    per_e = lambda r, c, dt: (pl.BlockSpec((1, r, c), lambda e: (e, 0, 0)), jax.ShapeDtypeStruct((E, r, c), dt))
    specs = [per_e(1, cap, jnp.int32), per_e(1, cap, F32), per_e(nb, LANES, jnp.int32), per_e(nb, 1, jnp.int32)]
    return pl.pallas_call(
        functools.partial(_select_kernel, cap=cap, sc=sc),
        grid=(E,),
        in_specs=[_const_spec((E, nb, LANES))],
        out_specs=[s for s, _ in specs],
        out_shape=[o for _, o in specs],
        compiler_params=_cparams(48, 1),
        name="select",
    )(lg3)


GATHER_UNROLL = 8
GATHER_GROUPS = 8


def _ffn_kernel(idx_ref, idxn_ref, gate_ref, h2s_ref, wg_ref, wu_ref, wd_ref, ye_ref, xbuf_ref, sem_ref):
    rows = ye_ref.shape[0]
    n = pl.program_id(0)
    n_steps = pl.num_programs(0)
    slot = n % 2

    def row_copy(ids_ref, r, to_slot):
        src = pl.multiple_of(ids_ref[0, 0, r] * ROW_SLABS, ROW_SLABS)
        return pltpu.make_async_copy(h2s_ref.at[pl.ds(src, ROW_SLABS), :],
                                     xbuf_ref.at[to_slot, pl.ds(r * ROW_SLABS, ROW_SLABS), :],
                                     sem_ref.at[to_slot])

    def start_gather(ids_ref, to_slot):
        def body(r0, carry):
            for u in range(GATHER_UNROLL):
                row_copy(ids_ref, r0 * GATHER_UNROLL + u, to_slot).start()
            return carry
        lax.fori_loop(0, rows // GATHER_UNROLL, body, 0)

    def wait_slot(s):
        pltpu.make_async_copy(h2s_ref.at[pl.ds(0, rows * ROW_SLABS), :], xbuf_ref.at[s], sem_ref.at[s]).wait()

    @pl.when(n == 0)
    def _():
        start_gather(idx_ref, 0)

    wait_slot(slot)

    per_group = rows // GATHER_GROUPS

    def prefetch(k):
        for r in range(k * per_group, (k + 1) * per_group):
            row_copy(idxn_ref, r, 1 - slot).start()

    half_e, half_m = D_EXPERT // 2, D_MODEL // 2
    prefetch(0)
    x = jnp.concatenate(
        [xbuf_ref[slot, pl.ds(s, rows, stride=ROW_SLABS), :].astype(BF16) for s in range(ROW_SLABS)], axis=1)
    prefetch(1)
    a0 = jnp.dot(x, wg_ref[0, :, :half_e], preferred_element_type=F32)
    prefetch(2)
    a1 = jnp.dot(x, wg_ref[0, :, half_e:], preferred_element_type=F32)
    prefetch(3)
    u0 = jnp.dot(x, wu_ref[0, :, :half_e], preferred_element_type=F32)
    prefetch(4)
    u1 = jnp.dot(x, wu_ref[0, :, half_e:], preferred_element_type=F32)
    prefetch(5)
    hmid = jnp.concatenate([jax.nn.silu(a0) * u0, jax.nn.silu(a1) * u1], axis=1).astype(BF16)
    ri = lax.broadcasted_iota(jnp.int32, (rows, rows), 0)
    ci = lax.broadcasted_iota(jnp.int32, (rows, rows), 1)
    gate_col = jnp.sum(jnp.where(ri == ci, gate_ref[0], 0.0), axis=1, keepdims=True)
    y0 = jnp.dot(hmid, wd_ref[0, :, :half_m], preferred_element_type=F32)
    prefetch(6)
    ye_ref[:, :half_m] = (gate_col * y0).astype(BF16)
    y1 = jnp.dot(hmid, wd_ref[0, :, half_m:], preferred_element_type=F32)
    prefetch(7)
    ye_ref[:, half_m:] = (gate_col * y1).astype(BF16)

    @pl.when(n == n_steps - 1)
    def _():
        wait_slot(1 - slot)


def _ffn(idx3, gate3, h2s, wg, wu, wd, *, cap):
    n_steps, _, rows = idx3.shape
    per_e = cap // rows
    last = n_steps - 1
    return pl.pallas_call(
        _ffn_kernel,
        grid=(n_steps,),
        in_specs=[pl.BlockSpec((1, 1, rows), lambda n: (n, 0, 0), memory_space=pltpu.SMEM),
                  pl.BlockSpec((1, 1, rows), lambda n: (jnp.minimum(n + 1, last), 0, 0), memory_space=pltpu.SMEM),
                  pl.BlockSpec((1, 1, rows), lambda n: (n, 0, 0)),
                  pl.BlockSpec(memory_space=pl.ANY),
                  pl.BlockSpec((1, D_MODEL, D_EXPERT), lambda n: (n // per_e, 0, 0)),
                  pl.BlockSpec((1, D_MODEL, D_EXPERT), lambda n: (n // per_e, 0, 0)),
                  pl.BlockSpec((1, D_EXPERT, D_MODEL), lambda n: (n // per_e, 0, 0))],
        out_specs=pl.BlockSpec((rows, D_MODEL), lambda n: (n, 0)),
        out_shape=jax.ShapeDtypeStruct((n_steps * rows, D_MODEL), BF16),
        scratch_shapes=[pltpu.VMEM((2, rows * ROW_SLABS, LANES), F32), pltpu.SemaphoreType.DMA((2,))],
        compiler_params=_cparams(48, 1),
        name="ffn",
    )(idx3, idx3, gate3, h2s, wg, wu, wd)


BF16_ROWS = 16
MXU_DEPTH = 256


def _combine_kernel(tab_ref, x1_ref, pos_ref, p_ref, ye_ref, wpg_ref, wpp_ref, gp_ref, gfin_ref,
                    y_ref, win_ref, sem_ref, more_ref, msem_ref, acc_ref, *, cap, win):
    tt = x1_ref.shape[0]
    i = pl.program_id(0)
    n_tiles = pl.num_programs(0)
    slot = i % 2
    total = N_EXPERTS * cap
    per_dot = MXU_DEPTH // win
    wi = lax.broadcasted_iota(jnp.int32, (win, tt), 0)

    def first_row(e, tile):
        first = e * cap + tab_ref[e, tile]
        aligned = lax.shift_left(lax.shift_right_logical(first, 4), 4)
        return jnp.minimum(aligned, total - win)

    def fetch_windows(tile, to_slot):
        for e in range(N_EXPERTS):
            a0 = pl.multiple_of(first_row(e, tile), BF16_ROWS)
            pltpu.make_async_copy(ye_ref.at[pl.ds(a0, win), :], win_ref.at[to_slot, pl.ds(e * win, win), :],
                                  sem_ref.at[to_slot]).start()

    @pl.when(i == 0)
    def _():
        fetch_windows(0, 0)

    @pl.when(i + 1 < n_tiles)
    def _():
        fetch_windows(i + 1, 1 - slot)

    pltpu.make_async_copy(ye_ref.at[pl.ds(0, N_EXPERTS * win), :], win_ref.at[slot], sem_ref.at[slot]).wait()

    def onehot(e, a0):
        pos = pos_ref[pl.ds(e, 1), :]
        return (((pos + (e * cap - a0)) == wi) & (pos >= 0)).astype(BF16)

    def rows_from(e, a0):
        return e * cap + tab_ref[e, i + 1] - a0

    acc = None
    for j in range(N_EXPERTS // per_dot):
        es = range(j * per_dot, (j + 1) * per_dot)
        hot = jnp.concatenate([onehot(e, first_row(e, i)) for e in es], axis=0)
        part = lax.dot_general(hot, win_ref[slot, j * MXU_DEPTH:(j + 1) * MXU_DEPTH, :], _TN,
                               preferred_element_type=F32)
        acc = part if acc is None else acc + part
    acc_ref[...] = acc

    longest = rows_from(0, first_row(0, i))
    for e in range(1, N_EXPERTS):
        longest = jnp.maximum(longest, rows_from(e, first_row(e, i)))

    @pl.when(longest > win)
    def _():
        def per_expert(e, carry):
            a0 = first_row(e, i)
            n_windows = lax.shift_right_logical(rows_from(e, a0) + (win - 1), win.bit_length() - 1)

            def more(c, carry):
                a = pl.multiple_of(jnp.minimum(a0 + c * win, total - win), BF16_ROWS)
                copy = pltpu.make_async_copy(ye_ref.at[pl.ds(a, win), :], more_ref, msem_ref.at[0])
                copy.start()
                copy.wait()
                acc_ref[...] += lax.dot_general(onehot(e, a), more_ref[...], _TN, preferred_element_type=F32)
                return carry
            return lax.fori_loop(1, n_windows, more, carry)
        lax.fori_loop(0, N_EXPERTS, per_expert, 0)

    x2 = x1_ref[...] + acc_ref[...]
    gate = jax.nn.sigmoid(jnp.dot(_rms(x2, gp_ref[...]).astype(BF16), wpg_ref[...], preferred_element_type=F32))
    x3 = x2 + gate * jnp.dot(p_ref[...].astype(BF16), wpp_ref[...], preferred_element_type=F32)
    y_ref[...] = _rms(x3, gfin_ref[...])


def _combine(tab, x1, pos, p, ye, wpg, wpp, gp, gfin, *, cap, tt, win):
    N = x1.shape[0]
    grid_spec = pltpu.PrefetchScalarGridSpec(
        num_scalar_prefetch=1,
        grid=(N // tt,),
        in_specs=[pl.BlockSpec((tt, D_MODEL), lambda i, t: (i, 0)),
                  pl.BlockSpec((N_EXPERTS, tt), lambda i, t: (0, i)),
                  pl.BlockSpec((tt, D_PLE), lambda i, t: (i, 0)),
                  pl.BlockSpec(memory_space=pl.ANY),
                  pl.BlockSpec((D_MODEL, D_MODEL), lambda i, t: (0, 0)),
                  pl.BlockSpec((D_PLE, D_MODEL), lambda i, t: (0, 0)),
                  pl.BlockSpec((1, D_MODEL), lambda i, t: (0, 0)),
                  pl.BlockSpec((1, D_MODEL), lambda i, t: (0, 0))],
        out_specs=pl.BlockSpec((tt, D_MODEL), lambda i, t: (i, 0)),
        scratch_shapes=[pltpu.VMEM((2, N_EXPERTS * win, D_MODEL), BF16), pltpu.SemaphoreType.DMA((2,)),
                        pltpu.VMEM((win, D_MODEL), BF16), pltpu.SemaphoreType.DMA((1,)),
                        pltpu.VMEM((tt, D_MODEL), F32)],
    )
    return pl.pallas_call(
        functools.partial(_combine_kernel, cap=cap, win=win),
        grid_spec=grid_spec,
        out_shape=jax.ShapeDtypeStruct((N, D_MODEL), F32),
        compiler_params=_cparams(48, 1),
        name="combine",
    )(tab, x1, pos, p, ye, wpg, wpp, gp, gfin)


def _t5_bucket(rel):
    nb = NUM_BUCKETS // 2
    ret = jnp.where(rel > 0, nb, 0)
    n = jnp.abs(rel)
    max_exact = nb // 2
    large = max_exact + (jnp.log(jnp.maximum(n, 1).astype(F32) / max_exact)
                         / math.log(MAX_DISTANCE / max_exact) * (nb - max_exact)).astype(jnp.int32)
    large = jnp.minimum(large, nb - 1)
    return ret + jnp.where(n < max_exact, n, large)


def _bucket_table(tq):
    kw = tq + 2 * WINDOW
    rel = jnp.arange(kw)[:, None] - WINDOW - jnp.arange(tq)[None, :]
    code = jnp.where(jnp.abs(rel) <= WINDOW, _t5_bucket(rel) + 1, 0).astype(jnp.int32)
    return code & (2 * NUM_BUCKETS - 1)


def _rope_tables(T):
    rows = T // GRID_W
    row = jnp.repeat(jnp.arange(rows), GRID_W).astype(F32)
    col = jnp.tile(jnp.arange(GRID_W), rows).astype(F32)
    freqs = ROPE_THETA ** (-jnp.arange(AX_PAIRS, dtype=F32) / AX_PAIRS)
    ang = jnp.concatenate([row[:, None] * freqs, col[:, None] * freqs], axis=-1)
    cos = jnp.repeat(jnp.cos(ang), 2, axis=1)
    sin = jnp.repeat(jnp.sin(ang), 2, axis=1) * jnp.tile(jnp.array([-1.0, 1.0], F32), HEAD_DIM // 2)
    return cos.T, sin.T, jnp.tile(cos, (1, N_KV)), jnp.tile(sin, (1, N_KV))


def _prep_params(g_attn, w_in, gq_b, gk_b, w_out, w_router, w_gate, w_up, w_down, w_ple_gate, w_ple_proj):
    w = w_in[0]
    o = np.cumsum([0, D_MIX, D_KV, D_KV, D_MIX, D_KV, D_KV])
    wqa, wka, wva, wqb, wkb, wvb = (w[:, o[i]:o[i + 1]] for i in range(6))
    swap_q = jnp.arange(D_MIX) ^ 1
    swap_k = jnp.arange(D_KV) ^ 1
    swap_h = jnp.arange(HEAD_DIM) ^ 1
    def spread(a):
        z = jnp.zeros(a.shape[:-1] + (LANES - HEAD_DIM,), a.dtype)
        return jnp.concatenate([a[..., :HEAD_DIM], z, a[..., HEAD_DIM:], z], axis=-1)

    w1 = jnp.concatenate([spread(wka), spread(wkb), spread(wkb[:, swap_k])], axis=1).astype(BF16)
    w2t = jnp.concatenate([wqa.T, wqb.T, wqb[:, swap_q].T, wva.T, wvb.T], axis=0).astype(BF16)
    hd = jnp.arange(D_KX) // HEAD_DIM
    bd = (hd[:, None] == hd[None, :]).astype(BF16)
    gq = gq_b[0] * QSCALE
    gk = gk_b[0]
    wr = jnp.pad(w_router[0], ((0, 0), (0, LANES - N_EXPERTS)))
    return dict(
        g_attn=g_attn[0][None], w1=w1, w2t=w2t, bd=bd,
        gq=jnp.broadcast_to(gq[:, None], (HEAD_DIM, LANES)),
        gqs=jnp.broadcast_to(gq[swap_h][:, None], (HEAD_DIM, LANES)),
        gk=spread(jnp.tile(gk, N_KV))[None], gks=spread(jnp.tile(gk[swap_h], N_KV))[None],
        wo=w_out[0].astype(BF16), wr=wr.astype(BF16),
        wg=w_gate[0].astype(BF16), wu=w_up[0].astype(BF16), wd=w_down[0].astype(BF16),
        wpg=w_ple_gate[0].astype(BF16), wpp=w_ple_proj[0].astype(BF16))


TOKEN_TILE = 512
WINDOW_Q_TILE = 256
DENSE_Q_TILE = 512
DENSE_KEY_CHUNK = 512
SELECT_SLOT_CHUNK = 1024
FFN_ROWS = 512
COMBINE_TILE = 256
COMBINE_WINDOW = 64


def _trunk(x, p, prm, sink, rel_bias, g_out_a, g_out_b, g_ffn, g_ple, g_final):
    B, T, _ = x.shape
    N = B * T
    cap = CAPACITY_FACTOR * N // N_EXPERTS
    qat, qbt, vat, vbt, ka, kb = _qkv(x, prm, *_rope_tables(T), tn=TOKEN_TILE)
    oa = _attn_a(qat, ka, vat, sink, rel_bias, _bucket_table(min(WINDOW_Q_TILE, T)))
    ob = _attn_b(qbt, kb, vbt, tq=DENSE_Q_TILE, kc=DENSE_KEY_CHUNK)
    x1, h2s, lg = _post(oa.reshape(N, D_MIX), ob.reshape(N, D_MIX), x.reshape(N, D_MODEL),
                        g_out_a, g_out_b, prm["wo"], g_ffn, prm["wr"], tn=TOKEN_TILE)
    lg3 = lg[:, :N_EXPERTS].T.reshape(N_EXPERTS, N // LANES, LANES)
    idx, gate, pos, bex = _select(lg3, cap=cap, sc=SELECT_SLOT_CHUNK)
    rows = min(FFN_ROWS, cap)
    ye = _ffn(idx.reshape(-1, 1, rows), gate.reshape(-1, 1, rows), h2s, prm["wg"], prm["wu"], prm["wd"], cap=cap)
    tt = min(COMBINE_TILE, N)
    win = min(COMBINE_WINDOW, cap)
    tile_start = bex.reshape(N_EXPERTS, N // LANES)[:, ::tt // LANES]
    tab = jnp.concatenate([tile_start, jnp.full((N_EXPERTS, 1), cap, jnp.int32)], axis=1)
    y = _combine(tab, x1, pos.reshape(N_EXPERTS, N), p.reshape(N, D_PLE), ye, prm["wpg"], prm["wpp"],
                 g_ple, g_final, cap=cap, tt=tt, win=win)
    return y.reshape(B, T, D_MODEL)


def kernel(x_prompt, x_sample, p_prompt, p_sample, g_attn, w_in, sink_a, rel_bias, gq_b, gk_b, g_out_a, g_out_b, w_out, g_ffn, w_router, w_gate, w_up, w_down, g_ple, w_ple_gate, w_ple_proj, g_final):
    prm = _prep_params(g_attn, w_in, gq_b, gk_b, w_out, w_router, w_gate, w_up, w_down, w_ple_gate, w_ple_proj)
    args = (prm, sink_a[0], rel_bias, g_out_a[0][None], g_out_b[0][None], g_ffn[0][None], g_ple[0][None],
            g_final[None])
    y_prompt = _trunk(x_prompt, p_prompt[0], *args)
    y_sample = _trunk(x_sample, p_sample[0], *args)
    return (y_prompt, y_sample)
```

```python
import functools
import math

import jax
import jax.numpy as jnp
import numpy as np
from jax import lax
from jax.experimental import pallas as pl
from jax.experimental.pallas import tpu as pltpu

D_MODEL = 1024
HEAD_DIM = 64
N_HEADS = 8
N_KV = 2
GROUP = N_HEADS // N_KV
D_MIX = N_HEADS * HEAD_DIM
D_KV = N_KV * HEAD_DIM
D_GRP = GROUP * HEAD_DIM
WINDOW = 128
NUM_BUCKETS = 32
MAX_DISTANCE = 128
GRID_W = 64
ROPE_THETA = 10000.0
AX_PAIRS = HEAD_DIM // 4
N_EXPERTS = 16
CAPACITY_FACTOR = 2
D_EXPERT = 512
D_PLE = 256
EPS = 1e-6
NEG = -1e30
LOG2E = math.log2(math.e)
QSCALE = HEAD_DIM ** -0.5 * LOG2E

LANES = 128
SUBLANES = 8
ROW_SLABS = D_MODEL // LANES

F32 = jnp.float32
BF16 = jnp.bfloat16

_NT = (((1,), (1,)), ((), ()))
_TN = (((0,), (0,)), ((), ()))


def _cparams(mib, n_axes, flags=None):
    return pltpu.CompilerParams(vmem_limit_bytes=mib * 1024 * 1024,
                                dimension_semantics=("arbitrary",) * n_axes, flags=flags)


def _const_spec(shape):
    zeros = (0,) * len(shape)
    return pl.BlockSpec(shape, lambda *_: zeros)


def _rms(x, g):
    r = lax.rsqrt(jnp.mean(x * x, axis=-1, keepdims=True) + EPS)
    return (x * r) * g


def _split_bf16(x):
    hi = x.astype(BF16)
    lo = (x - hi.astype(F32)).astype(BF16)
    return hi, lo


_Q_ROWS = 3 * D_MIX
D_KX = N_KV * LANES


def _qkv_kernel(x_ref, g_ref, w1_ref, w2t_ref, bd_ref, gq_ref, gqs_ref, gk_ref, gks_ref,
                ct_ref, st_ref, ck_ref, sk_ref,
                qat_ref, qbt_ref, vat_ref, vbt_ref, ka_ref, kb_ref):
    tn = x_ref.shape[1]
    h = _rms(x_ref[0], g_ref[...]).astype(BF16)
    p1 = jnp.dot(h, w1_ref[...], preferred_element_type=F32)
    p2 = lax.dot_general(w2t_ref[...], h, _NT, preferred_element_type=F32)

    qat_ref[0] = (p2[0:D_MIX] * QSCALE).astype(BF16)
    vat_ref[0] = p2[_Q_ROWS:_Q_ROWS + D_KV].astype(BF16)
    vbt_ref[0] = p2[_Q_ROWS + D_KV:_Q_ROWS + 2 * D_KV].astype(BF16)
    ka_ref[0] = p1[:, 0:D_KX].astype(BF16)

    q_raw = p2[D_MIX:2 * D_MIX].reshape(N_HEADS, HEAD_DIM, tn)
    q_swp = p2[2 * D_MIX:3 * D_MIX].reshape(N_HEADS, HEAD_DIM, tn)
    rq = lax.rsqrt(jnp.sum(q_raw * q_raw, axis=1, keepdims=True) * (1.0 / HEAD_DIM) + EPS)
    lane_reps = tn // LANES
    gq = jnp.concatenate([gq_ref[...]] * lane_reps, axis=1)[None]
    gqs = jnp.concatenate([gqs_ref[...]] * lane_reps, axis=1)[None]
    qn = (q_raw * rq) * gq
    qs = (q_swp * rq) * gqs
    qbt_ref[0] = (qn * ct_ref[...][None] + qs * st_ref[...][None]).reshape(D_MIX, tn).astype(BF16)

    k_raw = p1[:, D_KX:2 * D_KX]
    k_swp = p1[:, 2 * D_KX:3 * D_KX]
    hi, lo = _split_bf16(k_raw * k_raw)
    ssq = (jnp.dot(hi, bd_ref[...], preferred_element_type=F32)
           + jnp.dot(lo, bd_ref[...], preferred_element_type=F32))
    rk = lax.rsqrt(ssq * (1.0 / HEAD_DIM) + EPS)
    ck = jnp.concatenate([ck_ref[...]] * N_KV, axis=1)
    sk = jnp.concatenate([sk_ref[...]] * N_KV, axis=1)
    kn = (k_raw * rk) * gk_ref[...]
    ks = (k_swp * rk) * gks_ref[...]
    kb_ref[0] = (kn * ck + ks * sk).astype(BF16)


def _qkv(x, prm, ct, st, ck, sk, *, tn):
    B, T, _ = x.shape
    tn = min(tn, T)
    grid = (B, T // tn)
    tok = lambda w: pl.BlockSpec((1, tn, w), lambda b, j: (b, j, 0))
    feat = lambda r: pl.BlockSpec((1, r, tn), lambda b, j: (b, 0, j))
    return pl.pallas_call(
        _qkv_kernel,
        grid=grid,
        in_specs=[tok(D_MODEL), _const_spec((1, D_MODEL)), _const_spec((D_MODEL, 3 * D_KX)),
                  _const_spec((_Q_ROWS + 2 * D_KV, D_MODEL)), _const_spec((D_KX, D_KX)),
                  _const_spec((HEAD_DIM, LANES)), _const_spec((HEAD_DIM, LANES)),
                  _const_spec((1, D_KX)), _const_spec((1, D_KX)),
                  pl.BlockSpec((HEAD_DIM, tn), lambda b, j: (0, j)),
                  pl.BlockSpec((HEAD_DIM, tn), lambda b, j: (0, j)),
                  pl.BlockSpec((tn, LANES), lambda b, j: (j, 0)),
                  pl.BlockSpec((tn, LANES), lambda b, j: (j, 0))],
        out_specs=[feat(D_MIX), feat(D_MIX), feat(D_KV), feat(D_KV), tok(D_KX), tok(D_KX)],
        out_shape=[jax.ShapeDtypeStruct((B, D_MIX, T), BF16), jax.ShapeDtypeStruct((B, D_MIX, T), BF16),
                   jax.ShapeDtypeStruct((B, D_KV, T), BF16), jax.ShapeDtypeStruct((B, D_KV, T), BF16),
                   jax.ShapeDtypeStruct((B, T, D_KX), BF16), jax.ShapeDtypeStruct((B, T, D_KX), BF16)],
        compiler_params=_cparams(48, 2),
        name="qkv",
    )(x, prm["g_attn"], prm["w1"], prm["w2t"], prm["bd"], prm["gq"], prm["gqs"], prm["gk"], prm["gks"],
      ct, st, ck, sk)


ONES_ROWS = 16
V1_ROWS = HEAD_DIM + ONES_ROWS


def _head_weights(qt_ref, g, extra):
    return jnp.concatenate([qt_ref[0, g * HEAD_DIM:(g + 1) * HEAD_DIM, :], extra], axis=0)


def _normalised(acc):
    return acc[0:HEAD_DIM] * (1.0 / acc[HEAD_DIM:HEAD_DIM + 1])


def _attn_b_kernel(qt_ref, k_ref, vt_ref, o_ref, v1_ref, s0_ref, s1_ref, p0_ref, p1_ref, a0_ref, a1_ref,
                   acc_ref, m_ref, *, kc):
    s_refs, p_refs, a_refs = (s0_ref, s1_ref), (p0_ref, p1_ref), (a0_ref, a1_ref)
    T = k_ref.shape[1]
    tq = qt_ref.shape[2]
    n_chunks = T // kc

    @pl.when(pl.program_id(2) == 0)
    def _():
        v1_ref[0:HEAD_DIM, :] = vt_ref[0]
        v1_ref[HEAD_DIM:V1_ROWS, :] = jnp.ones((ONES_ROWS, T), BF16)

    acc_ref[...] = jnp.zeros(acc_ref.shape, F32)
    m_ref[...] = jnp.full(m_ref.shape, NEG, F32)
    spare = jnp.zeros((HEAD_DIM, tq), BF16)

    heads = range(GROUP)

    def scores(c, slot, hs=heads):
        kblk = k_ref[0, pl.ds(pl.multiple_of(c * kc, kc), kc), :]
        for g in hs:
            s_refs[slot][g] = jnp.dot(kblk, _head_weights(qt_ref, g, spare), preferred_element_type=F32)

    def softmax(slot, hs=heads):
        for g in hs:
            s = s_refs[slot][g]
            m_old = m_ref[g]
            m_new = jnp.maximum(m_old, jnp.max(s, axis=0, keepdims=True))
            p_refs[slot][g] = jnp.exp2(s - m_new).astype(BF16)
            a_refs[slot][g] = jnp.exp2(m_old - m_new)
            m_ref[g] = m_new

    def values(c, slot, hs=heads):
        v1 = v1_ref[:, pl.ds(pl.multiple_of(c * kc, kc), kc)]
        for g in hs:
            acc_ref[g] = (a_refs[slot][g] * acc_ref[g]
                          + jnp.dot(v1, p_refs[slot][g], preferred_element_type=F32))

    scores(0, 0)
    softmax(0)
    scores(1, 1)

    def body(j, carry):
        c = 2 * j
        for g in heads:
            scores(c, 0, (g,))
            softmax(1, (g,))
            values(c - 2, 0, (g,))
        for g in heads:
            scores(c + 1, 1, (g,))
            softmax(0, (g,))
            values(c - 1, 1, (g,))
        return carry

    lax.fori_loop(1, n_chunks // 2, body, 0)
    values(n_chunks - 2, 0)
    softmax(1)
    values(n_chunks - 1, 1)
    o_ref[0] = jnp.concatenate([_normalised(acc_ref[g]) for g in range(GROUP)], axis=0).T


def _attn_b(qt, k, vt, *, tq, kc):
    B, _, T = qt.shape
    tq = min(tq, T)
    kc = min(kc, T // 2)
    assert T % (2 * kc) == 0
    grid = (B, N_KV, T // tq)
    return pl.pallas_call(
        functools.partial(_attn_b_kernel, kc=kc),
        grid=grid,
        in_specs=[pl.BlockSpec((1, D_GRP, tq), lambda b, h, i: (b, h, i)),
                  pl.BlockSpec((1, T, LANES), lambda b, h, i: (b, 0, h)),
                  pl.BlockSpec((1, HEAD_DIM, T), lambda b, h, i: (b, h, 0))],
        out_specs=pl.BlockSpec((1, tq, D_GRP), lambda b, h, i: (b, i, h)),
        out_shape=jax.ShapeDtypeStruct((B, T, D_MIX), F32),
        scratch_shapes=[pltpu.VMEM((V1_ROWS, T), BF16),
                        pltpu.VMEM((GROUP, kc, tq), F32), pltpu.VMEM((GROUP, kc, tq), F32),
                        pltpu.VMEM((GROUP, kc, tq), BF16), pltpu.VMEM((GROUP, kc, tq), BF16),
                        pltpu.VMEM((GROUP, 1, tq), F32), pltpu.VMEM((GROUP, 1, tq), F32),
                        pltpu.VMEM((GROUP, V1_ROWS, tq), F32),
                        pltpu.VMEM((GROUP, 1, tq), F32)],
        compiler_params=_cparams(48, 3),
        name="attn_b",
    )(qt, k, vt)


def _attn_a_kernel(sink_ref, rb_ref, bucket_ref, qt_ref, k_ref, vt_ref, o_ref, kp_ref, v1_ref, bias_ref):
    T = k_ref.shape[1]
    tq = qt_ref.shape[2]
    kw = tq + 2 * WINDOW
    kvh = pl.program_id(1)
    qi = pl.program_id(2)

    @pl.when((pl.program_id(0) == 0) & (kvh == 0) & (qi == 0))
    def _():
        lane = lax.broadcasted_iota(jnp.int32, (WINDOW, LANES), 1)
        off_seq = (lane == HEAD_DIM).astype(BF16)
        kp_ref[0:WINDOW, :] = off_seq
        kp_ref[WINDOW + T:, :] = off_seq
        v1_ref[0:HEAD_DIM, 0:WINDOW] = jnp.zeros((HEAD_DIM, WINDOW), BF16)
        v1_ref[0:HEAD_DIM, WINDOW + T:] = jnp.zeros((HEAD_DIM, WINDOW), BF16)
        v1_ref[HEAD_DIM:V1_ROWS, :] = jnp.ones((ONES_ROWS, T + 2 * WINDOW), BF16)
        bucket = bucket_ref[...]
        for h in range(N_HEADS):
            bias = jnp.full((kw, tq), NEG, F32)
            for k in range(NUM_BUCKETS):
                bias = jnp.where(bucket == k + 1, rb_ref[k, h], bias)
            bias_ref[h] = bias * LOG2E

    @pl.when(qi == 0)
    def _():
        kp_ref[WINDOW:WINDOW + T, :] = k_ref[0]
        v1_ref[0:HEAD_DIM, WINDOW:WINDOW + T] = vt_ref[0]

    start = pl.multiple_of(qi * tq, tq)
    kblk = kp_ref[pl.ds(start, kw), :]
    v1 = v1_ref[:, pl.ds(start, kw)]
    spare_row = lax.broadcasted_iota(jnp.int32, (HEAD_DIM, tq), 0)
    spare = jnp.where(spare_row == 0, NEG, 0.0).astype(BF16)
    def scores(g):
        return (jnp.dot(kblk, _head_weights(qt_ref, g, spare), preferred_element_type=F32)
                + bias_ref[kvh * GROUP + g])

    outs = []
    s_next = scores(0)
    for g in range(GROUP):
        s = s_next
        if g + 1 < GROUP:
            s_next = scores(g + 1)
        sink = jnp.full((1, tq), sink_ref[kvh * GROUP + g], F32) * LOG2E
        m = jnp.maximum(jnp.max(s, axis=0, keepdims=True), sink)
        pv = jnp.dot(v1, jnp.exp2(s - m).astype(BF16), preferred_element_type=F32)
        den = pv[HEAD_DIM:HEAD_DIM + 1] + jnp.exp2(sink - m)
        outs.append(pv[0:HEAD_DIM] * (1.0 / den))
    o_ref[0] = jnp.concatenate(outs, axis=0).T


def _attn_a(qt, k, vt, sink, rel_bias, bucket):
    B, _, T = qt.shape
    kw, tq = bucket.shape
    grid = (B, N_KV, T // tq)
    smem = pl.BlockSpec(memory_space=pltpu.SMEM)
    return pl.pallas_call(
        _attn_a_kernel,
        grid=grid,
        in_specs=[smem, smem, _const_spec((kw, tq)),
                  pl.BlockSpec((1, D_GRP, tq), lambda b, h, i: (b, h, i)),
                  pl.BlockSpec((1, T, LANES), lambda b, h, i: (b, 0, h)),
                  pl.BlockSpec((1, HEAD_DIM, T), lambda b, h, i: (b, h, 0))],
        out_specs=pl.BlockSpec((1, tq, D_GRP), lambda b, h, i: (b, i, h)),
        out_shape=jax.ShapeDtypeStruct((B, T, D_MIX), F32),
        scratch_shapes=[pltpu.VMEM((T + 2 * WINDOW, LANES), BF16),
                        pltpu.VMEM((V1_ROWS, T + 2 * WINDOW), BF16),
                        pltpu.VMEM((N_HEADS, kw, tq), F32)],
        compiler_params=_cparams(48, 3),
        name="attn_a",
    )(sink, rel_bias, bucket, qt, k, vt)


def _post_kernel(oa_ref, ob_ref, x_ref, ga_ref, gb_ref, wo_ref, gf_ref, wr_ref,
                 x1_ref, h2s_ref, lg_ref):
    tn = x_ref.shape[0]
    mixed = jnp.concatenate([_rms(oa_ref[...], ga_ref[...]), _rms(ob_ref[...], gb_ref[...])], axis=1)
    x1 = x_ref[...] + jnp.dot(mixed.astype(BF16), wo_ref[...], preferred_element_type=F32)
    x1_ref[...] = x1
    h2 = _rms(x1, gf_ref[...])
    for s in range(ROW_SLABS):
        h2s_ref[pl.ds(s, tn, stride=ROW_SLABS), :] = h2[:, s * LANES:(s + 1) * LANES]
    lg_ref[...] = jnp.dot(h2.astype(BF16), wr_ref[...], preferred_element_type=F32)


def _post(oa, ob, x, ga, gb, wo, gf, wr, *, tn):
    N = x.shape[0]
    tn = min(tn, N)
    tok = lambda w: pl.BlockSpec((tn, w), lambda j: (j, 0))
    return pl.pallas_call(
        _post_kernel,
        grid=(N // tn,),
        in_specs=[tok(D_MIX), tok(D_MIX), tok(D_MODEL), _const_spec((1, D_MIX)), _const_spec((1, D_MIX)),
                  _const_spec((D_MODEL, D_MODEL)), _const_spec((1, D_MODEL)),
                  _const_spec((D_MODEL, LANES))],
        out_specs=[tok(D_MODEL), pl.BlockSpec((ROW_SLABS * tn, LANES), lambda j: (j, 0)), tok(LANES)],
        out_shape=[jax.ShapeDtypeStruct((N, D_MODEL), F32),
                   jax.ShapeDtypeStruct((ROW_SLABS * N, LANES), F32),
                   jax.ShapeDtypeStruct((N, LANES), F32)],
        compiler_params=_cparams(48, 1),
        name="post",
    )(oa, ob, x, ga, gb, wo, gf, wr)


def _select_kernel(lg_ref, idx_ref, gate_ref, pos_ref, bex_ref, *, cap, sc):
    e = pl.program_id(0)
    nb = lg_ref.shape[1]
    lg = lg_ref[...]
    mx = jnp.max(lg, axis=0)
    den = jnp.sum(jnp.exp(lg - mx), axis=0)
    aff = jnp.exp(lg_ref[e] - mx) / den

    def bisect(_, lohi):
        lo, hi = lohi
        mid = lo + jnp.right_shift(hi - lo, 1)
        thr = lax.bitcast_convert_type(mid, F32)
        cnt = jnp.sum((aff >= thr).astype(jnp.int32))
        ok = cnt >= cap
        return jnp.where(ok, mid, lo), jnp.where(ok, hi, mid)

    lo0 = jnp.zeros((1, 1), jnp.int32)
    hi0 = jnp.full((1, 1), 0x7F800000, jnp.int32)
    lo, hi = lax.fori_loop(0, 31, bisect, (lo0, hi0))
    above = aff >= lax.bitcast_convert_type(hi, F32)
    tie = (aff >= lax.bitcast_convert_type(lo, F32)) & jnp.logical_not(above)
    need = cap - jnp.sum(above.astype(jnp.int32))

    li = lax.broadcasted_iota(jnp.int32, (LANES, LANES), 0)
    lj = lax.broadcasted_iota(jnp.int32, (LANES, LANES), 1)
    ut_incl = (li <= lj).astype(BF16)
    ut_excl = (li < lj).astype(BF16)
    bi = lax.broadcasted_iota(jnp.int32, (nb, nb), 0)
    bj = lax.broadcasted_iota(jnp.int32, (nb, nb), 1)
    lt_incl = (bj <= bi).astype(BF16)
    lt_excl = (bj < bi).astype(BF16)

    def last(x):
        return x[:, LANES - 1:LANES]

    tie_b = tie.astype(BF16)
    tie_excl = jnp.dot(tie_b, ut_excl, preferred_element_type=F32)
    tie_incl = jnp.dot(tie_b, ut_incl, preferred_element_type=F32)
    tie_before = last(jnp.dot(lt_excl, tie_incl.astype(BF16), preferred_element_type=F32))
    tie_rank = (tie_before + tie_excl).astype(jnp.int32)
    sel = above | (tie & (tie_rank < need))

    lcum = jnp.dot(sel.astype(BF16), ut_incl, preferred_element_type=F32)
    lcum_b = lcum.astype(BF16)
    bcum_incl = last(jnp.dot(lt_incl, lcum_b, preferred_element_type=F32))
    bcum_excl = bcum_incl - last(lcum)
    pos_ref[0] = jnp.where(sel, (bcum_excl + lcum).astype(jnp.int32) - 1, -1)
    bex_ref[0] = bcum_excl.astype(jnp.int32)

    a1 = aff.astype(BF16)
    a2 = (aff - a1.astype(F32)).astype(BF16)
    a3 = (aff - a1.astype(F32) - a2.astype(F32)).astype(BF16)
    blk = lax.broadcasted_iota(jnp.int32, (nb, sc), 0)
    row = lax.broadcasted_iota(jnp.int32, (LANES, sc), 0)
    for c in range(cap // sc):
        slot = (c * sc + lax.broadcasted_iota(jnp.int32, (1, sc), 1)).astype(F32)
        hit = (bcum_excl <= slot) & (slot < bcum_incl)
        hit_b = hit.astype(BF16)
        base = jnp.sum(jnp.where(hit, bcum_excl, 0.0), axis=0, keepdims=True)
        blk_of = jnp.sum(jnp.where(hit, blk, 0), axis=0, keepdims=True)
        lc_t = lax.dot_general(lcum_b, hit_b, _TN, preferred_element_type=F32)
        k_in = slot - base
        j_of = jnp.sum((lc_t <= k_in).astype(jnp.int32), axis=0, keepdims=True)
        idx_ref[0, :, c * sc:(c + 1) * sc] = blk_of * LANES + j_of
        aff_t = (lax.dot_general(a1, hit_b, _TN, preferred_element_type=F32)
                 + lax.dot_general(a2, hit_b, _TN, preferred_element_type=F32)
                 + lax.dot_general(a3, hit_b, _TN, preferred_element_type=F32))
        gate_ref[0, :, c * sc:(c + 1) * sc] = jnp.sum(jnp.where(row == j_of, aff_t, 0.0), axis=0, keepdims=True)


def _select(lg3, *, cap, sc):
    E, nb, _ = lg3.shape
    sc = min(sc, cap)
    per_e = lambda r, c, dt: (pl.BlockSpec((1, r, c), lambda e: (e, 0, 0)), jax.ShapeDtypeStruct((E, r, c), dt))
    specs = [per_e(1, cap, jnp.int32), per_e(1, cap, F32), per_e(nb, LANES, jnp.int32), per_e(nb, 1, jnp.int32)]
    return pl.pallas_call(
        functools.partial(_select_kernel, cap=cap, sc=sc),
        grid=(E,),
        in_specs=[_const_spec((E, nb, LANES))],
        out_specs=[s for s, _ in specs],
        out_shape=[o for _, o in specs],
        compiler_params=_cparams(48, 1),
        name="select",
    )(lg3)


GATHER_UNROLL = 8


def _ffn_kernel(idx_ref, idxn_ref, gate_ref, h2s_ref, wg_ref, wu_ref, wd_ref, ye_ref, xbuf_ref, sem_ref):
    rows = ye_ref.shape[0]
    n = pl.program_id(0)
    n_steps = pl.num_programs(0)
    slot = n % 2

    def row_copy(ids_ref, r, to_slot):
        src = pl.multiple_of(ids_ref[0, 0, r] * ROW_SLABS, ROW_SLABS)
        return pltpu.make_async_copy(h2s_ref.at[pl.ds(src, ROW_SLABS), :],
                                     xbuf_ref.at[to_slot, pl.ds(r * ROW_SLABS, ROW_SLABS), :],
                                     sem_ref.at[to_slot])

    def start_gather(ids_ref, to_slot):
        def body(r0, carry):
            for u in range(GATHER_UNROLL):
                row_copy(ids_ref, r0 * GATHER_UNROLL + u, to_slot).start(priority=u % 2)
            return carry
        lax.fori_loop(0, rows // GATHER_UNROLL, body, 0)

    def wait_slot(s):
        pltpu.make_async_copy(h2s_ref.at[pl.ds(0, rows * ROW_SLABS), :], xbuf_ref.at[s], sem_ref.at[s]).wait()

    @pl.when(n == 0)
    def _():
        start_gather(idx_ref, 0)

    @pl.when(n + 1 < n_steps)
    def _():
        start_gather(idxn_ref, 1 - slot)

    wait_slot(slot)
    x = jnp.concatenate(
        [xbuf_ref[slot, pl.ds(s, rows, stride=ROW_SLABS), :].astype(BF16) for s in range(ROW_SLABS)], axis=1)
    a = jnp.dot(x, wg_ref[0], preferred_element_type=F32)
    u = jnp.dot(x, wu_ref[0], preferred_element_type=F32)
    hmid = (jax.nn.silu(a) * u).astype(BF16)
    y = jnp.dot(hmid, wd_ref[0], preferred_element_type=F32)
    ri = lax.broadcasted_iota(jnp.int32, (rows, rows), 0)
    ci = lax.broadcasted_iota(jnp.int32, (rows, rows), 1)
    gate_col = jnp.sum(jnp.where(ri == ci, gate_ref[0], 0.0), axis=1, keepdims=True)
    ye_ref[...] = (gate_col * y).astype(BF16)


def _ffn(idx3, gate3, h2s, wg, wu, wd, *, cap):
    n_steps, _, rows = idx3.shape
    per_e = cap // rows
    last = n_steps - 1
    return pl.pallas_call(
        _ffn_kernel,
        grid=(n_steps,),
        in_specs=[pl.BlockSpec((1, 1, rows), lambda n: (n, 0, 0), memory_space=pltpu.SMEM),
                  pl.BlockSpec((1, 1, rows), lambda n: (jnp.minimum(n + 1, last), 0, 0), memory_space=pltpu.SMEM),
                  pl.BlockSpec((1, 1, rows), lambda n: (n, 0, 0)),
                  pl.BlockSpec(memory_space=pl.ANY),
                  pl.BlockSpec((1, D_MODEL, D_EXPERT), lambda n: (n // per_e, 0, 0)),
                  pl.BlockSpec((1, D_MODEL, D_EXPERT), lambda n: (n // per_e, 0, 0)),
                  pl.BlockSpec((1, D_EXPERT, D_MODEL), lambda n: (n // per_e, 0, 0))],
        out_specs=pl.BlockSpec((rows, D_MODEL), lambda n: (n, 0)),
        out_shape=jax.ShapeDtypeStruct((n_steps * rows, D_MODEL), BF16),
        scratch_shapes=[pltpu.VMEM((2, rows * ROW_SLABS, LANES), F32), pltpu.SemaphoreType.DMA((2,))],
        compiler_params=_cparams(48, 1),
        name="ffn",
    )(idx3, idx3, gate3, h2s, wg, wu, wd)


BF16_ROWS = 16
MXU_DEPTH = 256


def _combine_kernel(tab_ref, x1_ref, pos_ref, p_ref, ye_ref, wpg_ref, wpp_ref, gp_ref, gfin_ref,
                    y_ref, win_ref, sem_ref, more_ref, msem_ref, acc_ref, *, cap, win):
    tt = x1_ref.shape[0]
    i = pl.program_id(0)
    n_tiles = pl.num_programs(0)
    slot = i % 2
    total = N_EXPERTS * cap
    per_dot = MXU_DEPTH // win
    wi = lax.broadcasted_iota(jnp.int32, (win, tt), 0)

    def first_row(e, tile):
        first = e * cap + tab_ref[e, tile]
        aligned = lax.shift_left(lax.shift_right_logical(first, 4), 4)
        return jnp.minimum(aligned, total - win)

    def fetch_windows(tile, to_slot):
        for e in range(N_EXPERTS):
            a0 = pl.multiple_of(first_row(e, tile), BF16_ROWS)
            pltpu.make_async_copy(ye_ref.at[pl.ds(a0, win), :], win_ref.at[to_slot, pl.ds(e * win, win), :],
                                  sem_ref.at[to_slot]).start()

    @pl.when(i == 0)
    def _():
        fetch_windows(0, 0)

    @pl.when(i + 1 < n_tiles)
    def _():
        fetch_windows(i + 1, 1 - slot)

    pltpu.make_async_copy(ye_ref.at[pl.ds(0, N_EXPERTS * win), :], win_ref.at[slot], sem_ref.at[slot]).wait()

    def onehot(e, a0):
        pos = pos_ref[pl.ds(e, 1), :]
        return (((pos + (e * cap - a0)) == wi) & (pos >= 0)).astype(BF16)

    def rows_from(e, a0):
        return e * cap + tab_ref[e, i + 1] - a0

    acc = None
    for j in range(N_EXPERTS // per_dot):
        es = range(j * per_dot, (j + 1) * per_dot)
        hot = jnp.concatenate([onehot(e, first_row(e, i)) for e in es], axis=0)
        part = lax.dot_general(hot, win_ref[slot, j * MXU_DEPTH:(j + 1) * MXU_DEPTH, :], _TN,
                               preferred_element_type=F32)
        acc = part if acc is None else acc + part
    acc_ref[...] = acc

    longest = rows_from(0, first_row(0, i))
    for e in range(1, N_EXPERTS):
        longest = jnp.maximum(longest, rows_from(e, first_row(e, i)))

    @pl.when(longest > win)
    def _():
        def per_expert(e, carry):
            a0 = first_row(e, i)
            n_windows = lax.shift_right_logical(rows_from(e, a0) + (win - 1), win.bit_length() - 1)

            def more(c, carry):
                a = pl.multiple_of(jnp.minimum(a0 + c * win, total - win), BF16_ROWS)
                copy = pltpu.make_async_copy(ye_ref.at[pl.ds(a, win), :], more_ref, msem_ref.at[0])
                copy.start()
                copy.wait()
                acc_ref[...] += lax.dot_general(onehot(e, a), more_ref[...], _TN, preferred_element_type=F32)
                return carry
            return lax.fori_loop(1, n_windows, more, carry)
        lax.fori_loop(0, N_EXPERTS, per_expert, 0)

    x2 = x1_ref[...] + acc_ref[...]
    gate = jax.nn.sigmoid(jnp.dot(_rms(x2, gp_ref[...]).astype(BF16), wpg_ref[...], preferred_element_type=F32))
    x3 = x2 + gate * jnp.dot(p_ref[...].astype(BF16), wpp_ref[...], preferred_element_type=F32)
    y_ref[...] = _rms(x3, gfin_ref[...])


def _combine(tab, x1, pos, p, ye, wpg, wpp, gp, gfin, *, cap, tt, win):
    N = x1.shape[0]
    grid_spec = pltpu.PrefetchScalarGridSpec(
        num_scalar_prefetch=1,
        grid=(N // tt,),
        in_specs=[pl.BlockSpec((tt, D_MODEL), lambda i, t: (i, 0)),
                  pl.BlockSpec((N_EXPERTS, tt), lambda i, t: (0, i)),
                  pl.BlockSpec((tt, D_PLE), lambda i, t: (i, 0)),
                  pl.BlockSpec(memory_space=pl.ANY),
                  pl.BlockSpec((D_MODEL, D_MODEL), lambda i, t: (0, 0)),
                  pl.BlockSpec((D_PLE, D_MODEL), lambda i, t: (0, 0)),
                  pl.BlockSpec((1, D_MODEL), lambda i, t: (0, 0)),
                  pl.BlockSpec((1, D_MODEL), lambda i, t: (0, 0))],
        out_specs=pl.BlockSpec((tt, D_MODEL), lambda i, t: (i, 0)),
        scratch_shapes=[pltpu.VMEM((2, N_EXPERTS * win, D_MODEL), BF16), pltpu.SemaphoreType.DMA((2,)),
                        pltpu.VMEM((win, D_MODEL), BF16), pltpu.SemaphoreType.DMA((1,)),
                        pltpu.VMEM((tt, D_MODEL), F32)],
    )
    return pl.pallas_call(
        functools.partial(_combine_kernel, cap=cap, win=win),
        grid_spec=grid_spec,
        out_shape=jax.ShapeDtypeStruct((N, D_MODEL), F32),
        compiler_params=_cparams(48, 1),
        name="combine",
    )(tab, x1, pos, p, ye, wpg, wpp, gp, gfin)


def _t5_bucket(rel):
    nb = NUM_BUCKETS // 2
    ret = jnp.where(rel > 0, nb, 0)
    n = jnp.abs(rel)
    max_exact = nb // 2
    large = max_exact + (jnp.log(jnp.maximum(n, 1).astype(F32) / max_exact)
                         / math.log(MAX_DISTANCE / max_exact) * (nb - max_exact)).astype(jnp.int32)
    large = jnp.minimum(large, nb - 1)
    return ret + jnp.where(n < max_exact, n, large)


def _bucket_table(tq):
    kw = tq + 2 * WINDOW
    rel = jnp.arange(kw)[:, None] - WINDOW - jnp.arange(tq)[None, :]
    code = jnp.where(jnp.abs(rel) <= WINDOW, _t5_bucket(rel) + 1, 0).astype(jnp.int32)
    return code & (2 * NUM_BUCKETS - 1)


def _rope_tables(T):
    rows = T // GRID_W
    row = jnp.repeat(jnp.arange(rows), GRID_W).astype(F32)
    col = jnp.tile(jnp.arange(GRID_W), rows).astype(F32)
    freqs = ROPE_THETA ** (-jnp.arange(AX_PAIRS, dtype=F32) / AX_PAIRS)
    ang = jnp.concatenate([row[:, None] * freqs, col[:, None] * freqs], axis=-1)
    cos = jnp.repeat(jnp.cos(ang), 2, axis=1)
    sin = jnp.repeat(jnp.sin(ang), 2, axis=1) * jnp.tile(jnp.array([-1.0, 1.0], F32), HEAD_DIM // 2)
    return cos.T, sin.T, jnp.tile(cos, (1, N_KV)), jnp.tile(sin, (1, N_KV))


def _prep_params(g_attn, w_in, gq_b, gk_b, w_out, w_router, w_gate, w_up, w_down, w_ple_gate, w_ple_proj):
    w = w_in[0]
    o = np.cumsum([0, D_MIX, D_KV, D_KV, D_MIX, D_KV, D_KV])
    wqa, wka, wva, wqb, wkb, wvb = (w[:, o[i]:o[i + 1]] for i in range(6))
    swap_q = jnp.arange(D_MIX) ^ 1
    swap_k = jnp.arange(D_KV) ^ 1
    swap_h = jnp.arange(HEAD_DIM) ^ 1
    def spread(a):
        z = jnp.zeros(a.shape[:-1] + (LANES - HEAD_DIM,), a.dtype)
        return jnp.concatenate([a[..., :HEAD_DIM], z, a[..., HEAD_DIM:], z], axis=-1)

    w1 = jnp.concatenate([spread(wka), spread(wkb), spread(wkb[:, swap_k])], axis=1).astype(BF16)
    w2t = jnp.concatenate([wqa.T, wqb.T, wqb[:, swap_q].T, wva.T, wvb.T], axis=0).astype(BF16)
    hd = jnp.arange(D_KX) // HEAD_DIM
    bd = (hd[:, None] == hd[None, :]).astype(BF16)
    gq = gq_b[0] * QSCALE
    gk = gk_b[0]
    wr = jnp.pad(w_router[0], ((0, 0), (0, LANES - N_EXPERTS)))
    return dict(
        g_attn=g_attn[0][None], w1=w1, w2t=w2t, bd=bd,
        gq=jnp.broadcast_to(gq[:, None], (HEAD_DIM, LANES)),
        gqs=jnp.broadcast_to(gq[swap_h][:, None], (HEAD_DIM, LANES)),
        gk=spread(jnp.tile(gk, N_KV))[None], gks=spread(jnp.tile(gk[swap_h], N_KV))[None],
        wo=w_out[0].astype(BF16), wr=wr.astype(BF16),
        wg=w_gate[0].astype(BF16), wu=w_up[0].astype(BF16), wd=w_down[0].astype(BF16),
        wpg=w_ple_gate[0].astype(BF16), wpp=w_ple_proj[0].astype(BF16))


TOKEN_TILE = 512
WINDOW_Q_TILE = 256
DENSE_Q_TILE = 512
DENSE_KEY_CHUNK = 512
SELECT_SLOT_CHUNK = 1024
FFN_ROWS = 512
COMBINE_TILE = 256
COMBINE_WINDOW = 128


def _trunk(x, p, prm, sink, rel_bias, g_out_a, g_out_b, g_ffn, g_ple, g_final):
    B, T, _ = x.shape
    N = B * T
    cap = CAPACITY_FACTOR * N // N_EXPERTS
    qat, qbt, vat, vbt, ka, kb = _qkv(x, prm, *_rope_tables(T), tn=TOKEN_TILE)
    oa = _attn_a(qat, ka, vat, sink, rel_bias, _bucket_table(min(WINDOW_Q_TILE, T)))
    ob = _attn_b(qbt, kb, vbt, tq=DENSE_Q_TILE, kc=DENSE_KEY_CHUNK)
    x1, h2s, lg = _post(oa.reshape(N, D_MIX), ob.reshape(N, D_MIX), x.reshape(N, D_MODEL),
                        g_out_a, g_out_b, prm["wo"], g_ffn, prm["wr"], tn=TOKEN_TILE)
    lg3 = lg[:, :N_EXPERTS].T.reshape(N_EXPERTS, N // LANES, LANES)
    idx, gate, pos, bex = _select(lg3, cap=cap, sc=SELECT_SLOT_CHUNK)
    rows = min(FFN_ROWS, cap)
    ye = _ffn(idx.reshape(-1, 1, rows), gate.reshape(-1, 1, rows), h2s, prm["wg"], prm["wu"], prm["wd"], cap=cap)
    tt = min(COMBINE_TILE, N)
    win = min(COMBINE_WINDOW, cap)
    tile_start = bex.reshape(N_EXPERTS, N // LANES)[:, ::tt // LANES]
    tab = jnp.concatenate([tile_start, jnp.full((N_EXPERTS, 1), cap, jnp.int32)], axis=1)
    y = _combine(tab, x1, pos.reshape(N_EXPERTS, N), p.reshape(N, D_PLE), ye, prm["wpg"], prm["wpp"],
                 g_ple, g_final, cap=cap, tt=tt, win=win)
    return y.reshape(B, T, D_MODEL)


def kernel(x_prompt, x_sample, p_prompt, p_sample, g_attn, w_in, sink_a, rel_bias, gq_b, gk_b, g_out_a, g_out_b, w_out, g_ffn, w_router, w_gate, w_up, w_down, g_ple, w_ple_gate, w_ple_proj, g_final):
    prm = _prep_params(g_attn, w_in, gq_b, gk_b, w_out, w_router, w_gate, w_up, w_down, w_ple_gate, w_ple_proj)
    args = (prm, sink_a[0], rel_bias, g_out_a[0][None], g_out_b[0][None], g_ffn[0][None], g_ple[0][None],
            g_final[None])
    y_prompt = _trunk(x_prompt, p_prompt[0], *args)
    y_sample = _trunk(x_sample, p_sample[0], *args)
    return (y_prompt, y_sample)
```

```python
import functools
import math

import jax
import jax.numpy as jnp
import numpy as np
from jax import lax
from jax.experimental import pallas as pl
from jax.experimental.pallas import tpu as pltpu

D_MODEL = 1024
HEAD_DIM = 64
N_HEADS = 8
N_KV = 2
GROUP = N_HEADS // N_KV
D_MIX = N_HEADS * HEAD_DIM
D_KV = N_KV * HEAD_DIM
D_GRP = GROUP * HEAD_DIM
WINDOW = 128
NUM_BUCKETS = 32
MAX_DISTANCE = 128
GRID_W = 64
ROPE_THETA = 10000.0
AX_PAIRS = HEAD_DIM // 4
N_EXPERTS = 16
CAPACITY_FACTOR = 2
D_EXPERT = 512
D_PLE = 256
EPS = 1e-6
NEG = -1e30
LOG2E = math.log2(math.e)
QSCALE = HEAD_DIM ** -0.5 * LOG2E

LANES = 128
SUBLANES = 8
ROW_SLABS = D_MODEL // LANES

F32 = jnp.float32
BF16 = jnp.bfloat16

_NT = (((1,), (1,)), ((), ()))
_TN = (((0,), (0,)), ((), ()))


def _cparams(mib, n_axes, flags=None):
    return pltpu.CompilerParams(vmem_limit_bytes=mib * 1024 * 1024,
                                dimension_semantics=("arbitrary",) * n_axes, flags=flags)


def _const_spec(shape):
    zeros = (0,) * len(shape)
    return pl.BlockSpec(shape, lambda *_: zeros)


def _rms(x, g):
    r = lax.rsqrt(jnp.mean(x * x, axis=-1, keepdims=True) + EPS)
    return (x * r) * g


def _split_bf16(x):
    hi = x.astype(BF16)
    lo = (x - hi.astype(F32)).astype(BF16)
    return hi, lo


_Q_ROWS = 2 * D_MIX
D_KX = N_KV * LANES
HALF = HEAD_DIM // 2


def _qkv_kernel(x_ref, g_ref, w1_ref, w2t_ref, bd_ref, gq_ref, gk_ref, ct_ref, st_ref, ck_ref, sk_ref,
                qat_ref, qbt_ref, vat_ref, vbt_ref, ka_ref, kb_ref):
    tn = x_ref.shape[1]
    h = _rms(x_ref[0], g_ref[...]).astype(BF16)
    p1 = jnp.dot(h, w1_ref[...], preferred_element_type=F32)
    p2 = lax.dot_general(w2t_ref[...], h, _NT, preferred_element_type=F32)

    qat_ref[0] = (p2[0:D_MIX] * QSCALE).astype(BF16)
    vat_ref[0] = p2[_Q_ROWS:_Q_ROWS + D_KV].astype(BF16)
    vbt_ref[0] = p2[_Q_ROWS + D_KV:_Q_ROWS + 2 * D_KV].astype(BF16)
    ka_ref[0] = p1[:, 0:D_KX].astype(BF16)

    q = p2[D_MIX:2 * D_MIX].reshape(N_HEADS, HEAD_DIM, tn)
    rq = lax.rsqrt(jnp.sum(q * q, axis=1, keepdims=True) * (1.0 / HEAD_DIM) + EPS)
    gq = jnp.concatenate([gq_ref[...]] * (tn // LANES), axis=1)[None]
    qn = (q * rq) * gq
    x0, x1 = qn[:, :HALF], qn[:, HALF:]
    c, s = ct_ref[...][None], st_ref[...][None]
    qbt_ref[0] = jnp.concatenate([x0 * c - x1 * s, x0 * s + x1 * c], axis=1).reshape(D_MIX, tn).astype(BF16)

    k = p1[:, D_KX:2 * D_KX]
    hi, lo = _split_bf16(k * k)
    ssq = (jnp.dot(hi, bd_ref[...], preferred_element_type=F32)
           + jnp.dot(lo, bd_ref[...], preferred_element_type=F32))
    kn = (k * lax.rsqrt(ssq * (1.0 / HEAD_DIM) + EPS)) * gk_ref[...]
    lane = lax.broadcasted_iota(jnp.int32, (tn, LANES), 1)
    outs = []
    for hk in range(N_KV):
        kh = kn[:, hk * LANES:(hk + 1) * LANES]
        partner = jnp.where(lane < HALF, pltpu.roll(kh, LANES - HALF, 1), pltpu.roll(kh, HALF, 1))
        outs.append(kh * ck_ref[...] + partner * sk_ref[...])
    kb_ref[0] = jnp.concatenate(outs, axis=1).astype(BF16)


def _qkv(x, prm, ct, st, ck, sk, *, tn):
    B, T, _ = x.shape
    tn = min(tn, T)
    grid = (B, T // tn)
    tok = lambda w: pl.BlockSpec((1, tn, w), lambda b, j: (b, j, 0))
    feat = lambda r: pl.BlockSpec((1, r, tn), lambda b, j: (b, 0, j))
    return pl.pallas_call(
        _qkv_kernel,
        grid=grid,
        in_specs=[tok(D_MODEL), _const_spec((1, D_MODEL)), _const_spec((D_MODEL, 2 * D_KX)),
                  _const_spec((_Q_ROWS + 2 * D_KV, D_MODEL)), _const_spec((D_KX, D_KX)),
                  _const_spec((HEAD_DIM, LANES)), _const_spec((1, D_KX)),
                  pl.BlockSpec((HALF, tn), lambda b, j: (0, j)),
                  pl.BlockSpec((HALF, tn), lambda b, j: (0, j)),
                  pl.BlockSpec((tn, LANES), lambda b, j: (j, 0)),
                  pl.BlockSpec((tn, LANES), lambda b, j: (j, 0))],
        out_specs=[feat(D_MIX), feat(D_MIX), feat(D_KV), feat(D_KV), tok(D_KX), tok(D_KX)],
        out_shape=[jax.ShapeDtypeStruct((B, D_MIX, T), BF16), jax.ShapeDtypeStruct((B, D_MIX, T), BF16),
                   jax.ShapeDtypeStruct((B, D_KV, T), BF16), jax.ShapeDtypeStruct((B, D_KV, T), BF16),
                   jax.ShapeDtypeStruct((B, T, D_KX), BF16), jax.ShapeDtypeStruct((B, T, D_KX), BF16)],
        compiler_params=_cparams(48, 2),
        name="qkv",
    )(x, prm["g_attn"], prm["w1"], prm["w2t"], prm["bd"], prm["gq"], prm["gk"], ct, st, ck, sk)


ONES_ROWS = 16
V1_ROWS = HEAD_DIM + ONES_ROWS


def _head_weights(qt_ref, g, extra):
    return jnp.concatenate([qt_ref[0, g * HEAD_DIM:(g + 1) * HEAD_DIM, :], extra], axis=0)


def _normalised(acc):
    return acc[0:HEAD_DIM] * (1.0 / acc[HEAD_DIM:HEAD_DIM + 1])


def _attn_b_kernel(qt_ref, k_ref, vt_ref, o_ref, v1_ref, s0_ref, s1_ref, p0_ref, p1_ref, a0_ref, a1_ref,
                   acc_ref, m_ref, *, kc):
    s_refs, p_refs, a_refs = (s0_ref, s1_ref), (p0_ref, p1_ref), (a0_ref, a1_ref)
    T = k_ref.shape[1]
    tq = qt_ref.shape[2]
    n_chunks = T // kc

    @pl.when(pl.program_id(2) == 0)
    def _():
        v1_ref[0:HEAD_DIM, :] = vt_ref[0]
        v1_ref[HEAD_DIM:V1_ROWS, :] = jnp.ones((ONES_ROWS, T), BF16)

    acc_ref[...] = jnp.zeros(acc_ref.shape, F32)
    m_ref[...] = jnp.full(m_ref.shape, NEG, F32)
    spare = jnp.zeros((HEAD_DIM, tq), BF16)

    heads = range(GROUP)

    def scores(c, slot, hs=heads):
        kblk = k_ref[0, pl.ds(pl.multiple_of(c * kc, kc), kc), :]
        for g in hs:
            s_refs[slot][g] = jnp.dot(kblk, _head_weights(qt_ref, g, spare), preferred_element_type=F32)

    def softmax(slot, hs=heads):
        for g in hs:
            s = s_refs[slot][g]
            m_old = m_ref[g]
            m_new = jnp.maximum(m_old, jnp.max(s, axis=0, keepdims=True))
            p_refs[slot][g] = jnp.exp2(s - m_new).astype(BF16)
            a_refs[slot][g] = jnp.exp2(m_old - m_new)
            m_ref[g] = m_new

    def values(c, slot, hs=heads):
        v1 = v1_ref[:, pl.ds(pl.multiple_of(c * kc, kc), kc)]
        for g in hs:
            acc_ref[g] = (a_refs[slot][g] * acc_ref[g]
                          + jnp.dot(v1, p_refs[slot][g], preferred_element_type=F32))

    scores(0, 0)
    softmax(0)
    scores(1, 1)

    def body(j, carry):
        c = 2 * j
        for g in heads:
            scores(c, 0, (g,))
            softmax(1, (g,))
            values(c - 2, 0, (g,))
        for g in heads:
            scores(c + 1, 1, (g,))
            softmax(0, (g,))
            values(c - 1, 1, (g,))
        return carry

    lax.fori_loop(1, n_chunks // 2, body, 0)
    values(n_chunks - 2, 0)
    softmax(1)
    values(n_chunks - 1, 1)
    o_ref[0] = jnp.concatenate([_normalised(acc_ref[g]) for g in range(GROUP)], axis=0).T


def _attn_b(qt, k, vt, *, tq, kc):
    B, _, T = qt.shape
    tq = min(tq, T)
    kc = min(kc, T // 2)
    assert T % (2 * kc) == 0
    grid = (B, N_KV, T // tq)
    return pl.pallas_call(
        functools.partial(_attn_b_kernel, kc=kc),
        grid=grid,
        in_specs=[pl.BlockSpec((1, D_GRP, tq), lambda b, h, i: (b, h, i)),
                  pl.BlockSpec((1, T, LANES), lambda b, h, i: (b, 0, h)),
                  pl.BlockSpec((1, HEAD_DIM, T), lambda b, h, i: (b, h, 0))],
        out_specs=pl.BlockSpec((1, tq, D_GRP), lambda b, h, i: (b, i, h)),
        out_shape=jax.ShapeDtypeStruct((B, T, D_MIX), F32),
        scratch_shapes=[pltpu.VMEM((V1_ROWS, T), BF16),
                        pltpu.VMEM((GROUP, kc, tq), F32), pltpu.VMEM((GROUP, kc, tq), F32),
                        pltpu.VMEM((GROUP, kc, tq), BF16), pltpu.VMEM((GROUP, kc, tq), BF16),
                        pltpu.VMEM((GROUP, 1, tq), F32), pltpu.VMEM((GROUP, 1, tq), F32),
                        pltpu.VMEM((GROUP, V1_ROWS, tq), F32),
                        pltpu.VMEM((GROUP, 1, tq), F32)],
        compiler_params=_cparams(48, 3),
        name="attn_b",
    )(qt, k, vt)


HEADS_PER_DOT = 2


def _attn_a_kernel(sink_ref, rb_ref, bucket_ref, qt_ref, k_ref, vt_ref, o_ref, kp_ref, v1_ref, bias_ref):
    T = k_ref.shape[1]
    tq = qt_ref.shape[2]
    kw = tq + 2 * WINDOW
    kvh = pl.program_id(1)
    qi = pl.program_id(2)

    @pl.when((pl.program_id(0) == 0) & (kvh == 0) & (qi == 0))
    def _():
        lane = lax.broadcasted_iota(jnp.int32, (WINDOW, LANES), 1)
        off_seq = (lane == HEAD_DIM).astype(BF16)
        kp_ref[0:WINDOW, :] = off_seq
        kp_ref[WINDOW + T:, :] = off_seq
        v1_ref[0:HEAD_DIM, 0:WINDOW] = jnp.zeros((HEAD_DIM, WINDOW), BF16)
        v1_ref[0:HEAD_DIM, WINDOW + T:] = jnp.zeros((HEAD_DIM, WINDOW), BF16)
        v1_ref[HEAD_DIM:V1_ROWS, :] = jnp.ones((ONES_ROWS, T + 2 * WINDOW), BF16)
        bucket = bucket_ref[...]
        for h in range(N_HEADS):
            bias = jnp.full((kw, tq), NEG, F32)
            for k in range(NUM_BUCKETS):
                bias = jnp.where(bucket == k + 1, rb_ref[k, h], bias)
            bias_ref[h] = bias * LOG2E

    @pl.when(qi == 0)
    def _():
        kp_ref[WINDOW:WINDOW + T, :] = k_ref[0]
        v1_ref[0:HEAD_DIM, WINDOW:WINDOW + T] = vt_ref[0]

    start = pl.multiple_of(qi * tq, tq)
    kblk = kp_ref[pl.ds(start, kw), :]
    v1 = v1_ref[:, pl.ds(start, kw)]
    spare_row = lax.broadcasted_iota(jnp.int32, (HEAD_DIM, tq), 0)
    spare = jnp.where(spare_row == 0, NEG, 0.0).astype(BF16)
    def scores(heads):
        w = jnp.concatenate([_head_weights(qt_ref, g, spare) for g in heads], axis=1)
        bias = jnp.concatenate([bias_ref[kvh * GROUP + g] for g in heads], axis=1)
        return jnp.dot(kblk, w, preferred_element_type=F32) + bias

    head_sets = [tuple(range(h, h + HEADS_PER_DOT)) for h in range(0, GROUP, HEADS_PER_DOT)]
    outs = []
    s_next = scores(head_sets[0])
    for i, heads in enumerate(head_sets):
        s = s_next
        if i + 1 < len(head_sets):
            s_next = scores(head_sets[i + 1])
        sink = jnp.concatenate([jnp.full((1, tq), sink_ref[kvh * GROUP + g], F32) for g in heads], axis=1) * LOG2E
        m = jnp.maximum(jnp.max(s, axis=0, keepdims=True), sink)
        pv = jnp.dot(v1, jnp.exp2(s - m).astype(BF16), preferred_element_type=F32)
        den = pv[HEAD_DIM:HEAD_DIM + 1] + jnp.exp2(sink - m)
        o = pv[0:HEAD_DIM] * (1.0 / den)
        outs.extend(o[:, j * tq:(j + 1) * tq] for j in range(len(heads)))
    o_ref[0] = jnp.concatenate(outs, axis=0).T


def _attn_a(qt, k, vt, sink, rel_bias, bucket):
    B, _, T = qt.shape
    kw, tq = bucket.shape
    grid = (B, N_KV, T // tq)
    smem = pl.BlockSpec(memory_space=pltpu.SMEM)
    return pl.pallas_call(
        _attn_a_kernel,
        grid=grid,
        in_specs=[smem, smem, _const_spec((kw, tq)),
                  pl.BlockSpec((1, D_GRP, tq), lambda b, h, i: (b, h, i)),
                  pl.BlockSpec((1, T, LANES), lambda b, h, i: (b, 0, h)),
                  pl.BlockSpec((1, HEAD_DIM, T), lambda b, h, i: (b, h, 0))],
        out_specs=pl.BlockSpec((1, tq, D_GRP), lambda b, h, i: (b, i, h)),
        out_shape=jax.ShapeDtypeStruct((B, T, D_MIX), F32),
        scratch_shapes=[pltpu.VMEM((T + 2 * WINDOW, LANES), BF16),
                        pltpu.VMEM((V1_ROWS, T + 2 * WINDOW), BF16),
                        pltpu.VMEM((N_HEADS, kw, tq), F32)],
        compiler_params=_cparams(48, 3),
        name="attn_a",
    )(sink, rel_bias, bucket, qt, k, vt)


def _post_kernel(oa_ref, ob_ref, x_ref, ga_ref, gb_ref, wo_ref, gf_ref, wr_ref,
                 x1_ref, h2s_ref, lg_ref):
    tn = x_ref.shape[0]
    mixed = jnp.concatenate([_rms(oa_ref[...], ga_ref[...]), _rms(ob_ref[...], gb_ref[...])], axis=1)
    x1 = x_ref[...] + jnp.dot(mixed.astype(BF16), wo_ref[...], preferred_element_type=F32)
    x1_ref[...] = x1
    h2 = _rms(x1, gf_ref[...])
    for s in range(ROW_SLABS):
        h2s_ref[pl.ds(s, tn, stride=ROW_SLABS), :] = h2[:, s * LANES:(s + 1) * LANES]
    lg_ref[...] = jnp.dot(h2.astype(BF16), wr_ref[...], preferred_element_type=F32)


def _post(oa, ob, x, ga, gb, wo, gf, wr, *, tn):
    N = x.shape[0]
    tn = min(tn, N)
    tok = lambda w: pl.BlockSpec((tn, w), lambda j: (j, 0))
    return pl.pallas_call(
        _post_kernel,
        grid=(N // tn,),
        in_specs=[tok(D_MIX), tok(D_MIX), tok(D_MODEL), _const_spec((1, D_MIX)), _const_spec((1, D_MIX)),
                  _const_spec((D_MODEL, D_MODEL)), _const_spec((1, D_MODEL)),
                  _const_spec((D_MODEL, LANES))],
        out_specs=[tok(D_MODEL), pl.BlockSpec((ROW_SLABS * tn, LANES), lambda j: (j, 0)), tok(LANES)],
        out_shape=[jax.ShapeDtypeStruct((N, D_MODEL), F32),
                   jax.ShapeDtypeStruct((ROW_SLABS * N, LANES), F32),
                   jax.ShapeDtypeStruct((N, LANES), F32)],
        compiler_params=_cparams(48, 1),
        name="post",
    )(oa, ob, x, ga, gb, wo, gf, wr)


def _select_kernel(lg_ref, idx_ref, gate_ref, pos_ref, bex_ref, *, cap, sc):
    e = pl.program_id(0)
    nb = lg_ref.shape[1]
    lg = lg_ref[...]
    mx = jnp.max(lg, axis=0)
    den = jnp.sum(jnp.exp(lg - mx), axis=0)
    aff = jnp.exp(lg_ref[e] - mx) / den

    def bisect(_, lohi):
        lo, hi = lohi
        mid = lo + jnp.right_shift(hi - lo, 1)
        thr = lax.bitcast_convert_type(mid, F32)
        cnt = jnp.sum((aff >= thr).astype(jnp.int32))
        ok = cnt >= cap
        return jnp.where(ok, mid, lo), jnp.where(ok, hi, mid)

    lo0 = jnp.zeros((1, 1), jnp.int32)
    hi0 = jnp.full((1, 1), 0x7F800000, jnp.int32)
    lo, hi = lax.fori_loop(0, 31, bisect, (lo0, hi0))
    above = aff >= lax.bitcast_convert_type(hi, F32)
    tie = (aff >= lax.bitcast_convert_type(lo, F32)) & jnp.logical_not(above)
    need = cap - jnp.sum(above.astype(jnp.int32))

    li = lax.broadcasted_iota(jnp.int32, (LANES, LANES), 0)
    lj = lax.broadcasted_iota(jnp.int32, (LANES, LANES), 1)
    ut_incl = (li <= lj).astype(BF16)
    ut_excl = (li < lj).astype(BF16)
    bi = lax.broadcasted_iota(jnp.int32, (nb, nb), 0)
    bj = lax.broadcasted_iota(jnp.int32, (nb, nb), 1)
    lt_incl = (bj <= bi).astype(BF16)
    lt_excl = (bj < bi).astype(BF16)

    def last(x):
        return x[:, LANES - 1:LANES]

    tie_b = tie.astype(BF16)
    tie_excl = jnp.dot(tie_b, ut_excl, preferred_element_type=F32)
    tie_incl = jnp.dot(tie_b, ut_incl, preferred_element_type=F32)
    tie_before = last(jnp.dot(lt_excl, tie_incl.astype(BF16), preferred_element_type=F32))
    tie_rank = (tie_before + tie_excl).astype(jnp.int32)
    sel = above | (tie & (tie_rank < need))

    lcum = jnp.dot(sel.astype(BF16), ut_incl, preferred_element_type=F32)
    lcum_b = lcum.astype(BF16)
    bcum_incl = last(jnp.dot(lt_incl, lcum_b, preferred_element_type=F32))
    bcum_excl = bcum_incl - last(lcum)
    pos_ref[0] = jnp.where(sel, (bcum_excl + lcum).astype(jnp.int32) - 1, -1)
    bex_ref[0] = bcum_excl.astype(jnp.int32)

    a1 = aff.astype(BF16)
    a2 = (aff - a1.astype(F32)).astype(BF16)
    a3 = (aff - a1.astype(F32) - a2.astype(F32)).astype(BF16)
    blk = lax.broadcasted_iota(jnp.int32, (nb, sc), 0)
    row = lax.broadcasted_iota(jnp.int32, (LANES, sc), 0)
    for c in range(cap // sc):
        slot = (c * sc + lax.broadcasted_iota(jnp.int32, (1, sc), 1)).astype(F32)
        hit = (bcum_excl <= slot) & (slot < bcum_incl)
        hit_b = hit.astype(BF16)
        base = jnp.sum(jnp.where(hit, bcum_excl, 0.0), axis=0, keepdims=True)
        blk_of = jnp.sum(jnp.where(hit, blk, 0), axis=0, keepdims=True)
        lc_t = lax.dot_general(lcum_b, hit_b, _TN, preferred_element_type=F32)
        k_in = slot - base
        j_of = jnp.sum((lc_t <= k_in).astype(jnp.int32), axis=0, keepdims=True)
        idx_ref[0, :, c * sc:(c + 1) * sc] = blk_of * LANES + j_of
        aff_t = (lax.dot_general(a1, hit_b, _TN, preferred_element_type=F32)
                 + lax.dot_general(a2, hit_b, _TN, preferred_element_type=F32)
                 + lax.dot_general(a3, hit_b, _TN, preferred_element_type=F32))
        gate_ref[0, :, c * sc:(c + 1) * sc] = jnp.sum(jnp.where(row == j_of, aff_t, 0.0), axis=0, keepdims=True)


def _select(lg3, *, cap, sc):
    E, nb, _ = lg3.shape
    sc = min(sc, cap)
    per_e = lambda r, c, dt: (pl.BlockSpec((1, r, c), lambda e: (e, 0, 0)), jax.ShapeDtypeStruct((E, r, c), dt))
    specs = [per_e(1, cap, jnp.int32), per_e(1, cap, F32), per_e(nb, LANES, jnp.int32), per_e(nb, 1, jnp.int32)]
    return pl.pallas_call(
        functools.partial(_select_kernel, cap=cap, sc=sc),
        grid=(E,),
        in_specs=[_const_spec((E, nb, LANES))],
        out_specs=[s for s, _ in specs],
        out_shape=[o for _, o in specs],
        compiler_params=_cparams(48, 1),
        name="select",
    )(lg3)


GATHER_UNROLL = 8


def _ffn_kernel(idx_ref, idxn_ref, gate_ref, h2s_ref, wg_ref, wu_ref, wd_ref, ye_ref, xbuf_ref, sem_ref):
    rows = ye_ref.shape[0]
    n = pl.program_id(0)
    n_steps = pl.num_programs(0)
    slot = n % 2

    def row_copy(ids_ref, r, to_slot):
        src = pl.multiple_of(ids_ref[0, 0, r] * ROW_SLABS, ROW_SLABS)
        return pltpu.make_async_copy(h2s_ref.at[pl.ds(src, ROW_SLABS), :],
                                     xbuf_ref.at[to_slot, pl.ds(r * ROW_SLABS, ROW_SLABS), :],
                                     sem_ref.at[to_slot])

    def start_gather(ids_ref, to_slot):
        def body(r0, carry):
            for u in range(GATHER_UNROLL):
                row_copy(ids_ref, r0 * GATHER_UNROLL + u, to_slot).start(priority=u % 2)
            return carry
        lax.fori_loop(0, rows // GATHER_UNROLL, body, 0)

    def wait_slot(s):
        pltpu.make_async_copy(h2s_ref.at[pl.ds(0, rows * ROW_SLABS), :], xbuf_ref.at[s], sem_ref.at[s]).wait()

    @pl.when(n == 0)
    def _():
        start_gather(idx_ref, 0)

    @pl.when(n + 1 < n_steps)
    def _():
        for r in range(rows):
            row_copy(idxn_ref, r, 1 - slot).start(priority=r % 2)

    wait_slot(slot)
    x = jnp.concatenate(
        [xbuf_ref[slot, pl.ds(s, rows, stride=ROW_SLABS), :].astype(BF16) for s in range(ROW_SLABS)], axis=1)
    a = jnp.dot(x, wg_ref[0], preferred_element_type=F32)
    u = jnp.dot(x, wu_ref[0], preferred_element_type=F32)
    hmid = (jax.nn.silu(a) * u).astype(BF16)
    y = jnp.dot(hmid, wd_ref[0], preferred_element_type=F32)
    ri = lax.broadcasted_iota(jnp.int32, (rows, rows), 0)
    ci = lax.broadcasted_iota(jnp.int32, (rows, rows), 1)
    gate_col = jnp.sum(jnp.where(ri == ci, gate_ref[0], 0.0), axis=1, keepdims=True)
    ye_ref[...] = (gate_col * y).astype(BF16)


def _ffn(idx3, gate3, h2s, wg, wu, wd, *, cap):
    n_steps, _, rows = idx3.shape
    per_e = cap // rows
    last = n_steps - 1
    return pl.pallas_call(
        _ffn_kernel,
        grid=(n_steps,),
        in_specs=[pl.BlockSpec((1, 1, rows), lambda n: (n, 0, 0), memory_space=pltpu.SMEM),
                  pl.BlockSpec((1, 1, rows), lambda n: (jnp.minimum(n + 1, last), 0, 0), memory_space=pltpu.SMEM),
                  pl.BlockSpec((1, 1, rows), lambda n: (n, 0, 0)),
                  pl.BlockSpec(memory_space=pl.ANY),
                  pl.BlockSpec((1, D_MODEL, D_EXPERT), lambda n: (n // per_e, 0, 0)),
                  pl.BlockSpec((1, D_MODEL, D_EXPERT), lambda n: (n // per_e, 0, 0)),
                  pl.BlockSpec((1, D_EXPERT, D_MODEL), lambda n: (n // per_e, 0, 0))],
        out_specs=pl.BlockSpec((rows, D_MODEL), lambda n: (n, 0)),
        out_shape=jax.ShapeDtypeStruct((n_steps * rows, D_MODEL), BF16),
        scratch_shapes=[pltpu.VMEM((2, rows * ROW_SLABS, LANES), F32), pltpu.SemaphoreType.DMA((2,))],
        compiler_params=_cparams(48, 1),
        name="ffn",
    )(idx3, idx3, gate3, h2s, wg, wu, wd)


BF16_ROWS = 16
MXU_DEPTH = 256


def _combine_kernel(tab_ref, x1_ref, pos_ref, p_ref, ye_ref, wpg_ref, wpp_ref, gp_ref, gfin_ref,
                    y_ref, win_ref, sem_ref, more_ref, msem_ref, acc_ref, *, cap, win):
    tt = x1_ref.shape[0]
    i = pl.program_id(0)
    n_tiles = pl.num_programs(0)
    slot = i % 2
    total = N_EXPERTS * cap
    per_dot = MXU_DEPTH // win
    wi = lax.broadcasted_iota(jnp.int32, (win, tt), 0)

    def first_row(e, tile):
        first = e * cap + tab_ref[e, tile]
        aligned = lax.shift_left(lax.shift_right_logical(first, 4), 4)
        return jnp.minimum(aligned, total - win)

    def fetch_windows(tile, to_slot):
        for e in range(N_EXPERTS):
            a0 = pl.multiple_of(first_row(e, tile), BF16_ROWS)
            pltpu.make_async_copy(ye_ref.at[pl.ds(a0, win), :], win_ref.at[to_slot, pl.ds(e * win, win), :],
                                  sem_ref.at[to_slot]).start()

    @pl.when(i == 0)
    def _():
        fetch_windows(0, 0)

    @pl.when(i + 1 < n_tiles)
    def _():
        fetch_windows(i + 1, 1 - slot)

    pltpu.make_async_copy(ye_ref.at[pl.ds(0, N_EXPERTS * win), :], win_ref.at[slot], sem_ref.at[slot]).wait()

    def onehot(e, a0):
        pos = pos_ref[pl.ds(e, 1), :]
        return (((pos + (e * cap - a0)) == wi) & (pos >= 0)).astype(BF16)

    def rows_from(e, a0):
        return e * cap + tab_ref[e, i + 1] - a0

    acc = None
    for j in range(N_EXPERTS // per_dot):
        es = range(j * per_dot, (j + 1) * per_dot)
        hot = jnp.concatenate([onehot(e, first_row(e, i)) for e in es], axis=0)
        part = lax.dot_general(hot, win_ref[slot, j * MXU_DEPTH:(j + 1) * MXU_DEPTH, :], _TN,
                               preferred_element_type=F32)
        acc = part if acc is None else acc + part
    acc_ref[...] = acc

    longest = rows_from(0, first_row(0, i))
    for e in range(1, N_EXPERTS):
        longest = jnp.maximum(longest, rows_from(e, first_row(e, i)))

    @pl.when(longest > win)
    def _():
        def per_expert(e, carry):
            a0 = first_row(e, i)
            n_windows = lax.shift_right_logical(rows_from(e, a0) + (win - 1), win.bit_length() - 1)

            def more(c, carry):
                a = pl.multiple_of(jnp.minimum(a0 + c * win, total - win), BF16_ROWS)
                copy = pltpu.make_async_copy(ye_ref.at[pl.ds(a, win), :], more_ref, msem_ref.at[0])
                copy.start()
                copy.wait()
                acc_ref[...] += lax.dot_general(onehot(e, a), more_ref[...], _TN, preferred_element_type=F32)
                return carry
            return lax.fori_loop(1, n_windows, more, carry)
        lax.fori_loop(0, N_EXPERTS, per_expert, 0)

    x2 = x1_ref[...] + acc_ref[...]
    gate = jax.nn.sigmoid(jnp.dot(_rms(x2, gp_ref[...]).astype(BF16), wpg_ref[...], preferred_element_type=F32))
    x3 = x2 + gate * jnp.dot(p_ref[...].astype(BF16), wpp_ref[...], preferred_element_type=F32)
    y_ref[...] = _rms(x3, gfin_ref[...])


def _combine(tab, x1, pos, p, ye, wpg, wpp, gp, gfin, *, cap, tt, win):
    N = x1.shape[0]
    grid_spec = pltpu.PrefetchScalarGridSpec(
        num_scalar_prefetch=1,
        grid=(N // tt,),
        in_specs=[pl.BlockSpec((tt, D_MODEL), lambda i, t: (i, 0)),
                  pl.BlockSpec((N_EXPERTS, tt), lambda i, t: (0, i)),
                  pl.BlockSpec((tt, D_PLE), lambda i, t: (i, 0)),
                  pl.BlockSpec(memory_space=pl.ANY),
                  pl.BlockSpec((D_MODEL, D_MODEL), lambda i, t: (0, 0)),
                  pl.BlockSpec((D_PLE, D_MODEL), lambda i, t: (0, 0)),
                  pl.BlockSpec((1, D_MODEL), lambda i, t: (0, 0)),
                  pl.BlockSpec((1, D_MODEL), lambda i, t: (0, 0))],
        out_specs=pl.BlockSpec((tt, D_MODEL), lambda i, t: (i, 0)),
        scratch_shapes=[pltpu.VMEM((2, N_EXPERTS * win, D_MODEL), BF16), pltpu.SemaphoreType.DMA((2,)),
                        pltpu.VMEM((win, D_MODEL), BF16), pltpu.SemaphoreType.DMA((1,)),
                        pltpu.VMEM((tt, D_MODEL), F32)],
    )
    return pl.pallas_call(
        functools.partial(_combine_kernel, cap=cap, win=win),
        grid_spec=grid_spec,
        out_shape=jax.ShapeDtypeStruct((N, D_MODEL), F32),
        compiler_params=_cparams(48, 1),
        name="combine",
    )(tab, x1, pos, p, ye, wpg, wpp, gp, gfin)


def _t5_bucket(rel):
    nb = NUM_BUCKETS // 2
    ret = jnp.where(rel > 0, nb, 0)
    n = jnp.abs(rel)
    max_exact = nb // 2
    large = max_exact + (jnp.log(jnp.maximum(n, 1).astype(F32) / max_exact)
                         / math.log(MAX_DISTANCE / max_exact) * (nb - max_exact)).astype(jnp.int32)
    large = jnp.minimum(large, nb - 1)
    return ret + jnp.where(n < max_exact, n, large)


def _bucket_table(tq):
    kw = tq + 2 * WINDOW
    rel = jnp.arange(kw)[:, None] - WINDOW - jnp.arange(tq)[None, :]
    code = jnp.where(jnp.abs(rel) <= WINDOW, _t5_bucket(rel) + 1, 0).astype(jnp.int32)
    return code & (2 * NUM_BUCKETS - 1)


def _rope_tables(T):
    rows = T // GRID_W
    row = jnp.repeat(jnp.arange(rows), GRID_W).astype(F32)
    col = jnp.tile(jnp.arange(GRID_W), rows).astype(F32)
    freqs = ROPE_THETA ** (-jnp.arange(AX_PAIRS, dtype=F32) / AX_PAIRS)
    ang = jnp.concatenate([row[:, None] * freqs, col[:, None] * freqs], axis=-1)
    cos, sin = jnp.cos(ang), jnp.sin(ang)
    spare = jnp.zeros((T, LANES - HEAD_DIM), F32)
    return cos.T, sin.T, jnp.concatenate([cos, cos, spare], axis=1), jnp.concatenate([-sin, sin, spare], axis=1)


def _prep_params(g_attn, w_in, gq_b, gk_b, w_out, w_router, w_gate, w_up, w_down, w_ple_gate, w_ple_proj):
    w = w_in[0]
    o = np.cumsum([0, D_MIX, D_KV, D_KV, D_MIX, D_KV, D_KV])
    wqa, wka, wva, wqb, wkb, wvb = (w[:, o[i]:o[i + 1]] for i in range(6))
    halves = jnp.concatenate([jnp.arange(0, HEAD_DIM, 2), jnp.arange(1, HEAD_DIM, 2)])
    perm_q = (jnp.arange(N_HEADS)[:, None] * HEAD_DIM + halves[None, :]).reshape(-1)
    perm_k = (jnp.arange(N_KV)[:, None] * HEAD_DIM + halves[None, :]).reshape(-1)

    def spread(a):
        z = jnp.zeros(a.shape[:-1] + (LANES - HEAD_DIM,), a.dtype)
        return jnp.concatenate([a[..., :HEAD_DIM], z, a[..., HEAD_DIM:], z], axis=-1)

    w1 = jnp.concatenate([spread(wka), spread(wkb[:, perm_k])], axis=1).astype(BF16)
    w2t = jnp.concatenate([wqa.T, wqb[:, perm_q].T, wva.T, wvb.T], axis=0).astype(BF16)
    hd = jnp.arange(D_KX) // HEAD_DIM
    bd = (hd[:, None] == hd[None, :]).astype(BF16)
    gq = (gq_b[0] * QSCALE)[halves]
    gk = gk_b[0][halves]
    wr = jnp.pad(w_router[0], ((0, 0), (0, LANES - N_EXPERTS)))
    return dict(
        g_attn=g_attn[0][None], w1=w1, w2t=w2t, bd=bd,
        gq=jnp.broadcast_to(gq[:, None], (HEAD_DIM, LANES)),
        gk=spread(jnp.tile(gk, N_KV))[None],
        wo=w_out[0].astype(BF16), wr=wr.astype(BF16),
        wg=w_gate[0].astype(BF16), wu=w_up[0].astype(BF16), wd=w_down[0].astype(BF16),
        wpg=w_ple_gate[0].astype(BF16), wpp=w_ple_proj[0].astype(BF16))


TOKEN_TILE = 512
WINDOW_Q_TILE = 256
DENSE_Q_TILE = 512
DENSE_KEY_CHUNK = 512
SELECT_SLOT_CHUNK = 1024
FFN_ROWS = 512
COMBINE_TILE = 256
COMBINE_WINDOW = 128


def _trunk(x, p, prm, sink, rel_bias, g_out_a, g_out_b, g_ffn, g_ple, g_final):
    B, T, _ = x.shape
    N = B * T
    cap = CAPACITY_FACTOR * N // N_EXPERTS
    qat, qbt, vat, vbt, ka, kb = _qkv(x, prm, *_rope_tables(T), tn=TOKEN_TILE)
    oa = _attn_a(qat, ka, vat, sink, rel_bias, _bucket_table(min(WINDOW_Q_TILE, T)))
    ob = _attn_b(qbt, kb, vbt, tq=DENSE_Q_TILE, kc=DENSE_KEY_CHUNK)
    x1, h2s, lg = _post(oa.reshape(N, D_MIX), ob.reshape(N, D_MIX), x.reshape(N, D_MODEL),
                        g_out_a, g_out_b, prm["wo"], g_ffn, prm["wr"], tn=TOKEN_TILE)
    lg3 = lg[:, :N_EXPERTS].T.reshape(N_EXPERTS, N // LANES, LANES)
    idx, gate, pos, bex = _select(lg3, cap=cap, sc=SELECT_SLOT_CHUNK)
    rows = min(FFN_ROWS, cap)
    ye = _ffn(idx.reshape(-1, 1, rows), gate.reshape(-1, 1, rows), h2s, prm["wg"], prm["wu"], prm["wd"], cap=cap)
    tt = min(COMBINE_TILE, N)
    win = min(COMBINE_WINDOW, cap)
    tile_start = bex.reshape(N_EXPERTS, N // LANES)[:, ::tt // LANES]
    tab = jnp.concatenate([tile_start, jnp.full((N_EXPERTS, 1), cap, jnp.int32)], axis=1)
    y = _combine(tab, x1, pos.reshape(N_EXPERTS, N), p.reshape(N, D_PLE), ye, prm["wpg"], prm["wpp"],
                 g_ple, g_final, cap=cap, tt=tt, win=win)
    return y.reshape(B, T, D_MODEL)


def kernel(x_prompt, x_sample, p_prompt, p_sample, g_attn, w_in, sink_a, rel_bias, gq_b, gk_b, g_out_a, g_out_b, w_out, g_ffn, w_router, w_gate, w_up, w_down, g_ple, w_ple_gate, w_ple_proj, g_final):
    prm = _prep_params(g_attn, w_in, gq_b, gk_b, w_out, w_router, w_gate, w_up, w_down, w_ple_gate, w_ple_proj)
    args = (prm, sink_a[0], rel_bias, g_out_a[0][None], g_out_b[0][None], g_ffn[0][None], g_ple[0][None],
            g_final[None])
    y_prompt = _trunk(x_prompt, p_prompt[0], *args)
    y_sample = _trunk(x_sample, p_sample[0], *args)
    return (y_prompt, y_sample)
```

```python
import functools
import math

import jax
import jax.numpy as jnp
import numpy as np
from jax import lax
from jax.experimental import pallas as pl
from jax.experimental.pallas import tpu as pltpu

D_MODEL = 1024
HEAD_DIM = 64
N_HEADS = 8
N_KV = 2
GROUP = N_HEADS // N_KV
D_MIX = N_HEADS * HEAD_DIM
D_KV = N_KV * HEAD_DIM
D_GRP = GROUP * HEAD_DIM
WINDOW = 128
NUM_BUCKETS = 32
MAX_DISTANCE = 128
GRID_W = 64
ROPE_THETA = 10000.0
AX_PAIRS = HEAD_DIM // 4
N_EXPERTS = 16
CAPACITY_FACTOR = 2
D_EXPERT = 512
D_PLE = 256
EPS = 1e-6
NEG = -1e30
LOG2E = math.log2(math.e)
QSCALE = HEAD_DIM ** -0.5 * LOG2E

LANES = 128
SUBLANES = 8
ROW_SLABS = D_MODEL // LANES

F32 = jnp.float32
BF16 = jnp.bfloat16

_NT = (((1,), (1,)), ((), ()))
_TN = (((0,), (0,)), ((), ()))


def _cparams(mib, n_axes, flags=None):
    return pltpu.CompilerParams(vmem_limit_bytes=mib * 1024 * 1024,
                                dimension_semantics=("arbitrary",) * n_axes, flags=flags)


def _const_spec(shape):
    zeros = (0,) * len(shape)
    return pl.BlockSpec(shape, lambda *_: zeros)


def _rms(x, g):
    r = lax.rsqrt(jnp.mean(x * x, axis=-1, keepdims=True) + EPS)
    return (x * r) * g


def _split_bf16(x):
    hi = x.astype(BF16)
    lo = (x - hi.astype(F32)).astype(BF16)
    return hi, lo


_Q_ROWS = 2 * D_MIX
D_KX = N_KV * LANES
HALF = HEAD_DIM // 2


def _qkv_kernel(x_ref, g_ref, w1_ref, w2t_ref, bd_ref, gq_ref, gk_ref, ct_ref, st_ref, ck_ref, sk_ref,
                qat_ref, qbt_ref, vat_ref, vbt_ref, ka_ref, kb_ref):
    tn = x_ref.shape[1]
    h = _rms(x_ref[0], g_ref[...]).astype(BF16)
    p1 = jnp.dot(h, w1_ref[...], preferred_element_type=F32)
    p2 = lax.dot_general(w2t_ref[...], h, _NT, preferred_element_type=F32)

    qat_ref[0] = (p2[0:D_MIX] * QSCALE).astype(BF16)
    vat_ref[0] = p2[_Q_ROWS:_Q_ROWS + D_KV].astype(BF16)
    vbt_ref[0] = p2[_Q_ROWS + D_KV:_Q_ROWS + 2 * D_KV].astype(BF16)
    ka_ref[0] = p1[:, 0:D_KX].astype(BF16)

    q = p2[D_MIX:2 * D_MIX].reshape(N_HEADS, HEAD_DIM, tn)
    rq = lax.rsqrt(jnp.sum(q * q, axis=1, keepdims=True) * (1.0 / HEAD_DIM) + EPS)
    gq = jnp.concatenate([gq_ref[...]] * (tn // LANES), axis=1)[None]
    qn = (q * rq) * gq
    x0, x1 = qn[:, :HALF], qn[:, HALF:]
    c, s = ct_ref[...][None], st_ref[...][None]
    qbt_ref[0] = jnp.concatenate([x0 * c - x1 * s, x0 * s + x1 * c], axis=1).reshape(D_MIX, tn).astype(BF16)

    k = p1[:, D_KX:2 * D_KX]
    hi, lo = _split_bf16(k * k)
    ssq = (jnp.dot(hi, bd_ref[...], preferred_element_type=F32)
           + jnp.dot(lo, bd_ref[...], preferred_element_type=F32))
    kn = (k * lax.rsqrt(ssq * (1.0 / HEAD_DIM) + EPS)) * gk_ref[...]
    lane = lax.broadcasted_iota(jnp.int32, (tn, LANES), 1)
    outs = []
    for hk in range(N_KV):
        kh = kn[:, hk * LANES:(hk + 1) * LANES]
        partner = jnp.where(lane < HALF, pltpu.roll(kh, LANES - HALF, 1), pltpu.roll(kh, HALF, 1))
        outs.append(kh * ck_ref[...] + partner * sk_ref[...])
    kb_ref[0] = jnp.concatenate(outs, axis=1).astype(BF16)


def _qkv(x, prm, ct, st, ck, sk, *, tn):
    B, T, _ = x.shape
    tn = min(tn, T)
    grid = (B, T // tn)
    tok = lambda w: pl.BlockSpec((1, tn, w), lambda b, j: (b, j, 0))
    feat = lambda r: pl.BlockSpec((1, r, tn), lambda b, j: (b, 0, j))
    return pl.pallas_call(
        _qkv_kernel,
        grid=grid,
        in_specs=[tok(D_MODEL), _const_spec((1, D_MODEL)), _const_spec((D_MODEL, 2 * D_KX)),
                  _const_spec((_Q_ROWS + 2 * D_KV, D_MODEL)), _const_spec((D_KX, D_KX)),
                  _const_spec((HEAD_DIM, LANES)), _const_spec((1, D_KX)),
                  pl.BlockSpec((HALF, tn), lambda b, j: (0, j)),
                  pl.BlockSpec((HALF, tn), lambda b, j: (0, j)),
                  pl.BlockSpec((tn, LANES), lambda b, j: (j, 0)),
                  pl.BlockSpec((tn, LANES), lambda b, j: (j, 0))],
        out_specs=[feat(D_MIX), feat(D_MIX), feat(D_KV), feat(D_KV), tok(D_KX), tok(D_KX)],
        out_shape=[jax.ShapeDtypeStruct((B, D_MIX, T), BF16), jax.ShapeDtypeStruct((B, D_MIX, T), BF16),
                   jax.ShapeDtypeStruct((B, D_KV, T), BF16), jax.ShapeDtypeStruct((B, D_KV, T), BF16),
                   jax.ShapeDtypeStruct((B, T, D_KX), BF16), jax.ShapeDtypeStruct((B, T, D_KX), BF16)],
        compiler_params=_cparams(48, 2),
        name="qkv",
    )(x, prm["g_attn"], prm["w1"], prm["w2t"], prm["bd"], prm["gq"], prm["gk"], ct, st, ck, sk)


ONES_ROWS = 16
V1_ROWS = HEAD_DIM + ONES_ROWS


def _head_weights(qt_ref, g, extra):
    return jnp.concatenate([qt_ref[0, g * HEAD_DIM:(g + 1) * HEAD_DIM, :], extra], axis=0)


def _normalised(acc):
    return acc[0:HEAD_DIM] * (1.0 / acc[HEAD_DIM:HEAD_DIM + 1])


def _attn_b_kernel(qt_ref, k_ref, vt_ref, o_ref, v1_ref, s0_ref, s1_ref, p0_ref, p1_ref, a0_ref, a1_ref,
                   acc_ref, m_ref, *, kc):
    s_refs, p_refs, a_refs = (s0_ref, s1_ref), (p0_ref, p1_ref), (a0_ref, a1_ref)
    T = k_ref.shape[1]
    tq = qt_ref.shape[2]
    n_chunks = T // kc

    @pl.when(pl.program_id(2) == 0)
    def _():
        v1_ref[0:HEAD_DIM, :] = vt_ref[0]
        v1_ref[HEAD_DIM:V1_ROWS, :] = jnp.ones((ONES_ROWS, T), BF16)

    acc_ref[...] = jnp.zeros(acc_ref.shape, F32)
    m_ref[...] = jnp.full(m_ref.shape, NEG, F32)
    spare = jnp.zeros((HEAD_DIM, tq), BF16)

    heads = range(GROUP)

    def scores(c, slot, hs=heads):
        kblk = k_ref[0, pl.ds(pl.multiple_of(c * kc, kc), kc), :]
        for g in hs:
            s_refs[slot][g] = jnp.dot(kblk, _head_weights(qt_ref, g, spare),
                                      preferred_element_type=F32).astype(BF16)

    def softmax(slot, hs=heads):
        for g in hs:
            s = s_refs[slot][g]
            m_old = m_ref[g]
            m_new = jnp.maximum(m_old, jnp.max(s, axis=0, keepdims=True).astype(F32))
            p_refs[slot][g] = jnp.exp2(s - m_new.astype(BF16))
            a_refs[slot][g] = jnp.exp2(m_old - m_new)
            m_ref[g] = m_new

    def values(c, slot, hs=heads):
        v1 = v1_ref[:, pl.ds(pl.multiple_of(c * kc, kc), kc)]
        for g in hs:
            acc_ref[g] = (a_refs[slot][g] * acc_ref[g]
                          + jnp.dot(v1, p_refs[slot][g], preferred_element_type=F32))

    scores(0, 0)
    softmax(0)
    scores(1, 1)

    def body(j, carry):
        c = 2 * j
        for g in heads:
            scores(c, 0, (g,))
            softmax(1, (g,))
            values(c - 2, 0, (g,))
        for g in heads:
            scores(c + 1, 1, (g,))
            softmax(0, (g,))
            values(c - 1, 1, (g,))
        return carry

    lax.fori_loop(1, n_chunks // 2, body, 0)
    values(n_chunks - 2, 0)
    softmax(1)
    values(n_chunks - 1, 1)
    o_ref[0] = jnp.concatenate([_normalised(acc_ref[g]) for g in range(GROUP)], axis=0).T


def _attn_b(qt, k, vt, *, tq, kc):
    B, _, T = qt.shape
    tq = min(tq, T)
    kc = min(kc, T // 2)
    assert T % (2 * kc) == 0
    grid = (B, N_KV, T // tq)
    return pl.pallas_call(
        functools.partial(_attn_b_kernel, kc=kc),
        grid=grid,
        in_specs=[pl.BlockSpec((1, D_GRP, tq), lambda b, h, i: (b, h, i)),
                  pl.BlockSpec((1, T, LANES), lambda b, h, i: (b, 0, h)),
                  pl.BlockSpec((1, HEAD_DIM, T), lambda b, h, i: (b, h, 0))],
        out_specs=pl.BlockSpec((1, tq, D_GRP), lambda b, h, i: (b, i, h)),
        out_shape=jax.ShapeDtypeStruct((B, T, D_MIX), F32),
        scratch_shapes=[pltpu.VMEM((V1_ROWS, T), BF16),
                        pltpu.VMEM((GROUP, kc, tq), BF16), pltpu.VMEM((GROUP, kc, tq), BF16),
                        pltpu.VMEM((GROUP, kc, tq), BF16), pltpu.VMEM((GROUP, kc, tq), BF16),
                        pltpu.VMEM((GROUP, 1, tq), F32), pltpu.VMEM((GROUP, 1, tq), F32),
                        pltpu.VMEM((GROUP, V1_ROWS, tq), F32),
                        pltpu.VMEM((GROUP, 1, tq), F32)],
        compiler_params=_cparams(48, 3),
        name="attn_b",
    )(qt, k, vt)


HEADS_PER_DOT = 2


def _attn_a_kernel(sink_ref, rb_ref, bucket_ref, qt_ref, k_ref, vt_ref, o_ref, kp_ref, v1_ref, bias_ref):
    T = k_ref.shape[1]
    tq = qt_ref.shape[2]
    kw = tq + 2 * WINDOW
    kvh = pl.program_id(1)
    qi = pl.program_id(2)

    @pl.when((pl.program_id(0) == 0) & (kvh == 0) & (qi == 0))
    def _():
        lane = lax.broadcasted_iota(jnp.int32, (WINDOW, LANES), 1)
        off_seq = (lane == HEAD_DIM).astype(BF16)
        kp_ref[0:WINDOW, :] = off_seq
        kp_ref[WINDOW + T:, :] = off_seq
        v1_ref[0:HEAD_DIM, 0:WINDOW] = jnp.zeros((HEAD_DIM, WINDOW), BF16)
        v1_ref[0:HEAD_DIM, WINDOW + T:] = jnp.zeros((HEAD_DIM, WINDOW), BF16)
        v1_ref[HEAD_DIM:V1_ROWS, :] = jnp.ones((ONES_ROWS, T + 2 * WINDOW), BF16)
        bucket = bucket_ref[...]
        for h in range(N_HEADS):
            bias = jnp.full((kw, tq), NEG, F32)
            for k in range(NUM_BUCKETS):
                bias = jnp.where(bucket == k + 1, rb_ref[k, h], bias)
            bias_ref[h] = bias * LOG2E

    @pl.when(qi == 0)
    def _():
        kp_ref[WINDOW:WINDOW + T, :] = k_ref[0]
        v1_ref[0:HEAD_DIM, WINDOW:WINDOW + T] = vt_ref[0]

    start = pl.multiple_of(qi * tq, tq)
    kblk = kp_ref[pl.ds(start, kw), :]
    v1 = v1_ref[:, pl.ds(start, kw)]
    spare_row = lax.broadcasted_iota(jnp.int32, (HEAD_DIM, tq), 0)
    spare = jnp.where(spare_row == 0, NEG, 0.0).astype(BF16)
    def scores(heads):
        w = jnp.concatenate([_head_weights(qt_ref, g, spare) for g in heads], axis=1)
        bias = jnp.concatenate([bias_ref[kvh * GROUP + g] for g in heads], axis=1)
        return (jnp.dot(kblk, w, preferred_element_type=F32) + bias).astype(BF16)

    head_sets = [tuple(range(h, h + HEADS_PER_DOT)) for h in range(0, GROUP, HEADS_PER_DOT)]
    outs = []
    s_next = scores(head_sets[0])
    for i, heads in enumerate(head_sets):
        s = s_next
        if i + 1 < len(head_sets):
            s_next = scores(head_sets[i + 1])
        sink = jnp.concatenate([jnp.full((1, tq), sink_ref[kvh * GROUP + g], F32) for g in heads], axis=1) * LOG2E
        m = jnp.maximum(jnp.max(s, axis=0, keepdims=True).astype(F32), sink).astype(BF16)
        pv = jnp.dot(v1, jnp.exp2(s - m), preferred_element_type=F32)
        den = pv[HEAD_DIM:HEAD_DIM + 1] + jnp.exp2(sink - m.astype(F32))
        o = pv[0:HEAD_DIM] * (1.0 / den)
        outs.extend(o[:, j * tq:(j + 1) * tq] for j in range(len(heads)))
    o_ref[0] = jnp.concatenate(outs, axis=0).T


def _attn_a(qt, k, vt, sink, rel_bias, bucket):
    B, _, T = qt.shape
    kw, tq = bucket.shape
    grid = (B, N_KV, T // tq)
    smem = pl.BlockSpec(memory_space=pltpu.SMEM)
    return pl.pallas_call(
        _attn_a_kernel,
        grid=grid,
        in_specs=[smem, smem, _const_spec((kw, tq)),
                  pl.BlockSpec((1, D_GRP, tq), lambda b, h, i: (b, h, i)),
                  pl.BlockSpec((1, T, LANES), lambda b, h, i: (b, 0, h)),
                  pl.BlockSpec((1, HEAD_DIM, T), lambda b, h, i: (b, h, 0))],
        out_specs=pl.BlockSpec((1, tq, D_GRP), lambda b, h, i: (b, i, h)),
        out_shape=jax.ShapeDtypeStruct((B, T, D_MIX), F32),
        scratch_shapes=[pltpu.VMEM((T + 2 * WINDOW, LANES), BF16),
                        pltpu.VMEM((V1_ROWS, T + 2 * WINDOW), BF16),
                        pltpu.VMEM((N_HEADS, kw, tq), F32)],
        compiler_params=_cparams(48, 3),
        name="attn_a",
    )(sink, rel_bias, bucket, qt, k, vt)


def _post_kernel(oa_ref, ob_ref, x_ref, ga_ref, gb_ref, wo_ref, gf_ref, wr_ref,
                 x1_ref, h2s_ref, lg_ref):
    tn = x_ref.shape[0]
    mixed = jnp.concatenate([_rms(oa_ref[...], ga_ref[...]), _rms(ob_ref[...], gb_ref[...])], axis=1)
    x1 = x_ref[...] + jnp.dot(mixed.astype(BF16), wo_ref[...], preferred_element_type=F32)
    x1_ref[...] = x1
    h2 = _rms(x1, gf_ref[...])
    for s in range(ROW_SLABS):
        h2s_ref[pl.ds(s, tn, stride=ROW_SLABS), :] = h2[:, s * LANES:(s + 1) * LANES]
    lg_ref[...] = jnp.dot(h2.astype(BF16), wr_ref[...], preferred_element_type=F32)


def _post(oa, ob, x, ga, gb, wo, gf, wr, *, tn):
    N = x.shape[0]
    tn = min(tn, N)
    tok = lambda w: pl.BlockSpec((tn, w), lambda j: (j, 0))
    return pl.pallas_call(
        _post_kernel,
        grid=(N // tn,),
        in_specs=[tok(D_MIX), tok(D_MIX), tok(D_MODEL), _const_spec((1, D_MIX)), _const_spec((1, D_MIX)),
                  _const_spec((D_MODEL, D_MODEL)), _const_spec((1, D_MODEL)),
                  _const_spec((D_MODEL, LANES))],
        out_specs=[tok(D_MODEL), pl.BlockSpec((ROW_SLABS * tn, LANES), lambda j: (j, 0)), tok(LANES)],
        out_shape=[jax.ShapeDtypeStruct((N, D_MODEL), F32),
                   jax.ShapeDtypeStruct((ROW_SLABS * N, LANES), F32),
                   jax.ShapeDtypeStruct((N, LANES), F32)],
        compiler_params=_cparams(48, 1),
        name="post",
    )(oa, ob, x, ga, gb, wo, gf, wr)


def _select_kernel(lg_ref, idx_ref, gate_ref, pos_ref, bex_ref, *, cap, sc):
    e = pl.program_id(0)
    nb = lg_ref.shape[1]
    lg = lg_ref[...]
    mx = jnp.max(lg, axis=0)
    den = jnp.sum(jnp.exp(lg - mx), axis=0)
    aff = jnp.exp(lg_ref[e] - mx) / den

    def bisect(_, lohi):
        lo, hi = lohi
        mid = lo + jnp.right_shift(hi - lo, 1)
        thr = lax.bitcast_convert_type(mid, F32)
        cnt = jnp.sum((aff >= thr).astype(jnp.int32))
        ok = cnt >= cap
        return jnp.where(ok, mid, lo), jnp.where(ok, hi, mid)

    lo0 = jnp.zeros((1, 1), jnp.int32)
    hi0 = jnp.full((1, 1), 0x7F800000, jnp.int32)
    lo, hi = lax.fori_loop(0, 31, bisect, (lo0, hi0))
    above = aff >= lax.bitcast_convert_type(hi, F32)
    tie = (aff >= lax.bitcast_convert_type(lo, F32)) & jnp.logical_not(above)
    need = cap - jnp.sum(above.astype(jnp.int32))

    li = lax.broadcasted_iota(jnp.int32, (LANES, LANES), 0)
    lj = lax.broadcasted_iota(jnp.int32, (LANES, LANES), 1)
    ut_incl = (li <= lj).astype(BF16)
    ut_excl = (li < lj).astype(BF16)
    bi = lax.broadcasted_iota(jnp.int32, (nb, nb), 0)
    bj = lax.broadcasted_iota(jnp.int32, (nb, nb), 1)
    lt_incl = (bj <= bi).astype(BF16)
    lt_excl = (bj < bi).astype(BF16)

    def last(x):
        return x[:, LANES - 1:LANES]

    tie_b = tie.astype(BF16)
    tie_excl = jnp.dot(tie_b, ut_excl, preferred_element_type=F32)
    tie_incl = jnp.dot(tie_b, ut_incl, preferred_element_type=F32)
    tie_before = last(jnp.dot(lt_excl, tie_incl.astype(BF16), preferred_element_type=F32))
    tie_rank = (tie_before + tie_excl).astype(jnp.int32)
    sel = above | (tie & (tie_rank < need))

    lcum = jnp.dot(sel.astype(BF16), ut_incl, preferred_element_type=F32)
    lcum_b = lcum.astype(BF16)
    bcum_incl = last(jnp.dot(lt_incl, lcum_b, preferred_element_type=F32))
    bcum_excl = bcum_incl - last(lcum)
    pos_ref[0] = jnp.where(sel, (bcum_excl + lcum).astype(jnp.int32) - 1, -1)
    bex_ref[0] = bcum_excl.astype(jnp.int32)

    a1 = aff.astype(BF16).astype(F32)
    a2 = (aff - a1).astype(BF16).astype(F32)
    a3 = (aff - a1 - a2).astype(BF16).astype(F32)
    excl_i = bcum_excl.astype(jnp.int32)
    blk_i = lax.broadcasted_iota(jnp.int32, (nb, 1), 0)
    digits = [jnp.right_shift(excl_i, 7), excl_i & (LANES - 1), jnp.right_shift(blk_i, 7), blk_i & (LANES - 1)]
    side = jnp.concatenate([d.astype(F32) for d in digits] + [jnp.zeros((nb, LANES - len(digits)), F32)], axis=1)
    tables_t = jnp.concatenate([lcum, side, a1, a2, a3], axis=1).T.astype(BF16)
    row = lax.broadcasted_iota(jnp.int32, (LANES, sc), 0)
    for c in range(cap // sc):
        slot = (c * sc + lax.broadcasted_iota(jnp.int32, (1, sc), 1)).astype(F32)
        hit = ((bcum_excl <= slot) & (slot < bcum_incl)).astype(BF16)
        got = jnp.dot(tables_t, hit, preferred_element_type=F32)
        lc_t = got[0:LANES]
        base = got[LANES:LANES + 1] * LANES + got[LANES + 1:LANES + 2]
        blk_of = (got[LANES + 2:LANES + 3] * LANES + got[LANES + 3:LANES + 4]).astype(jnp.int32)
        j_of = jnp.sum((lc_t <= slot - base).astype(jnp.int32), axis=0, keepdims=True)
        idx_ref[0, :, c * sc:(c + 1) * sc] = blk_of * LANES + j_of
        aff_t = got[2 * LANES:3 * LANES] + got[3 * LANES:4 * LANES] + got[4 * LANES:5 * LANES]
        gate_ref[0, :, c * sc:(c + 1) * sc] = jnp.sum(jnp.where(row == j_of, aff_t, 0.0), axis=0, keepdims=True)


def _select(lg3, *, cap, sc):
    E, nb, _ = lg3.shape
    sc = min(sc, cap)
    per_e = lambda r, c, dt: (pl.BlockSpec((1, r, c), lambda e: (e, 0, 0)), jax.ShapeDtypeStruct((E, r, c), dt))
    specs = [per_e(1, cap, jnp.int32), per_e(1, cap, F32), per_e(nb, LANES, jnp.int32), per_e(nb, 1, jnp.int32)]
    return pl.pallas_call(
        functools.partial(_select_kernel, cap=cap, sc=sc),
        grid=(E,),
        in_specs=[_const_spec((E, nb, LANES))],
        out_specs=[s for s, _ in specs],
        out_shape=[o for _, o in specs],
        compiler_params=_cparams(48, 1),
        name="select",
    )(lg3)


GATHER_UNROLL = 8


def _ffn_kernel(idx_ref, idxn_ref, gate_ref, h2s_ref, wg_ref, wu_ref, wd_ref, ye_ref, xbuf_ref, sem_ref):
    rows = ye_ref.shape[0]
    n = pl.program_id(0)
    n_steps = pl.num_programs(0)
    slot = n % 2

    def row_copy(ids_ref, r, to_slot):
        src = pl.multiple_of(ids_ref[0, 0, r] * ROW_SLABS, ROW_SLABS)
        return pltpu.make_async_copy(h2s_ref.at[pl.ds(src, ROW_SLABS), :],
                                     xbuf_ref.at[to_slot, pl.ds(r * ROW_SLABS, ROW_SLABS), :],
                                     sem_ref.at[to_slot])

    def start_gather(ids_ref, to_slot):
        def body(r0, carry):
            for u in range(GATHER_UNROLL):
                row_copy(ids_ref, r0 * GATHER_UNROLL + u, to_slot).start(priority=u % 2)
            return carry
        lax.fori_loop(0, rows // GATHER_UNROLL, body, 0)

    def wait_slot(s):
        pltpu.make_async_copy(h2s_ref.at[pl.ds(0, rows * ROW_SLABS), :], xbuf_ref.at[s], sem_ref.at[s]).wait()

    @pl.when(n == 0)
    def _():
        start_gather(idx_ref, 0)

    @pl.when(n + 1 < n_steps)
    def _():
        for r in range(rows):
            row_copy(idxn_ref, r, 1 - slot).start(priority=r % 2)

    wait_slot(slot)
    x = jnp.concatenate(
        [xbuf_ref[slot, pl.ds(s, rows, stride=ROW_SLABS), :].astype(BF16) for s in range(ROW_SLABS)], axis=1)
    a = jnp.dot(x, wg_ref[0], preferred_element_type=F32)
    u = jnp.dot(x, wu_ref[0], preferred_element_type=F32)
    hmid = (jax.nn.silu(a) * u).astype(BF16)
    y = jnp.dot(hmid, wd_ref[0], preferred_element_type=F32)
    ri = lax.broadcasted_iota(jnp.int32, (rows, rows), 0)
    ci = lax.broadcasted_iota(jnp.int32, (rows, rows), 1)
    gate_col = jnp.sum(jnp.where(ri == ci, gate_ref[0], 0.0), axis=1, keepdims=True)
    ye_ref[...] = (gate_col * y).astype(BF16)


def _ffn(idx3, gate3, h2s, wg, wu, wd, *, cap):
    n_steps, _, rows = idx3.shape
    per_e = cap // rows
    last = n_steps - 1
    return pl.pallas_call(
        _ffn_kernel,
        grid=(n_steps,),
        in_specs=[pl.BlockSpec((1, 1, rows), lambda n: (n, 0, 0), memory_space=pltpu.SMEM),
                  pl.BlockSpec((1, 1, rows), lambda n: (jnp.minimum(n + 1, last), 0, 0), memory_space=pltpu.SMEM),
                  pl.BlockSpec((1, 1, rows), lambda n: (n, 0, 0)),
                  pl.BlockSpec(memory_space=pl.ANY),
                  pl.BlockSpec((1, D_MODEL, D_EXPERT), lambda n: (n // per_e, 0, 0)),
                  pl.BlockSpec((1, D_MODEL, D_EXPERT), lambda n: (n // per_e, 0, 0)),
                  pl.BlockSpec((1, D_EXPERT, D_MODEL), lambda n: (n // per_e, 0, 0))],
        out_specs=pl.BlockSpec((rows, D_MODEL), lambda n: (n, 0)),
        out_shape=jax.ShapeDtypeStruct((n_steps * rows, D_MODEL), BF16),
        scratch_shapes=[pltpu.VMEM((2, rows * ROW_SLABS, LANES), F32), pltpu.SemaphoreType.DMA((2,))],
        compiler_params=_cparams(48, 1),
        name="ffn",
    )(idx3, idx3, gate3, h2s, wg, wu, wd)


BF16_ROWS = 16
MXU_DEPTH = 256


def _combine_kernel(tab_ref, x1_ref, pos_ref, p_ref, ye_ref, wpg_ref, wpp_ref, gp_ref, gfin_ref,
                    y_ref, win_ref, sem_ref, more_ref, msem_ref, acc_ref, *, cap, win):
    tt = x1_ref.shape[0]
    i = pl.program_id(0)
    n_tiles = pl.num_programs(0)
    slot = i % 2
    total = N_EXPERTS * cap
    per_dot = MXU_DEPTH // win
    wi = lax.broadcasted_iota(jnp.int32, (win, tt), 0)

    def first_row(e, tile):
        first = e * cap + tab_ref[e, tile]
        aligned = lax.shift_left(lax.shift_right_logical(first, 4), 4)
        return jnp.minimum(aligned, total - win)

    def fetch_windows(tile, to_slot):
        for e in range(N_EXPERTS):
            a0 = pl.multiple_of(first_row(e, tile), BF16_ROWS)
            pltpu.make_async_copy(ye_ref.at[pl.ds(a0, win), :], win_ref.at[to_slot, pl.ds(e * win, win), :],
                                  sem_ref.at[to_slot]).start()

    @pl.when(i == 0)
    def _():
        fetch_windows(0, 0)

    @pl.when(i + 1 < n_tiles)
    def _():
        fetch_windows(i + 1, 1 - slot)

    pltpu.make_async_copy(ye_ref.at[pl.ds(0, N_EXPERTS * win), :], win_ref.at[slot], sem_ref.at[slot]).wait()

    def onehot(e, a0):
        pos = pos_ref[pl.ds(e, 1), :]
        return (((pos + (e * cap - a0)) == wi) & (pos >= 0)).astype(BF16)

    def rows_from(e, a0):
        return e * cap + tab_ref[e, i + 1] - a0

    acc = None
    for j in range(N_EXPERTS // per_dot):
        es = range(j * per_dot, (j + 1) * per_dot)
        hot = jnp.concatenate([onehot(e, first_row(e, i)) for e in es], axis=0)
        part = lax.dot_general(hot, win_ref[slot, j * MXU_DEPTH:(j + 1) * MXU_DEPTH, :], _TN,
                               preferred_element_type=F32)
        acc = part if acc is None else acc + part
    acc_ref[...] = acc

    longest = rows_from(0, first_row(0, i))
    for e in range(1, N_EXPERTS):
        longest = jnp.maximum(longest, rows_from(e, first_row(e, i)))

    @pl.when(longest > win)
    def _():
        def per_expert(e, carry):
            a0 = first_row(e, i)
            n_windows = lax.shift_right_logical(rows_from(e, a0) + (win - 1), win.bit_length() - 1)

            def more(c, carry):
                a = pl.multiple_of(jnp.minimum(a0 + c * win, total - win), BF16_ROWS)
                copy = pltpu.make_async_copy(ye_ref.at[pl.ds(a, win), :], more_ref, msem_ref.at[0])
                copy.start()
                copy.wait()
                acc_ref[...] += lax.dot_general(onehot(e, a), more_ref[...], _TN, preferred_element_type=F32)
                return carry
            return lax.fori_loop(1, n_windows, more, carry)
        lax.fori_loop(0, N_EXPERTS, per_expert, 0)

    x2 = x1_ref[...] + acc_ref[...]
    gate = jax.nn.sigmoid(jnp.dot(_rms(x2, gp_ref[...]).astype(BF16), wpg_ref[...], preferred_element_type=F32))
    x3 = x2 + gate * jnp.dot(p_ref[...].astype(BF16), wpp_ref[...], preferred_element_type=F32)
    y_ref[...] = _rms(x3, gfin_ref[...])


def _combine(tab, x1, pos, p, ye, wpg, wpp, gp, gfin, *, cap, tt, win):
    N = x1.shape[0]
    grid_spec = pltpu.PrefetchScalarGridSpec(
        num_scalar_prefetch=1,
        grid=(N // tt,),
        in_specs=[pl.BlockSpec((tt, D_MODEL), lambda i, t: (i, 0)),
                  pl.BlockSpec((N_EXPERTS, tt), lambda i, t: (0, i)),
                  pl.BlockSpec((tt, D_PLE), lambda i, t: (i, 0)),
                  pl.BlockSpec(memory_space=pl.ANY),
                  pl.BlockSpec((D_MODEL, D_MODEL), lambda i, t: (0, 0)),
                  pl.BlockSpec((D_PLE, D_MODEL), lambda i, t: (0, 0)),
                  pl.BlockSpec((1, D_MODEL), lambda i, t: (0, 0)),
                  pl.BlockSpec((1, D_MODEL), lambda i, t: (0, 0))],
        out_specs=pl.BlockSpec((tt, D_MODEL), lambda i, t: (i, 0)),
        scratch_shapes=[pltpu.VMEM((2, N_EXPERTS * win, D_MODEL), BF16), pltpu.SemaphoreType.DMA((2,)),
                        pltpu.VMEM((win, D_MODEL), BF16), pltpu.SemaphoreType.DMA((1,)),
                        pltpu.VMEM((tt, D_MODEL), F32)],
    )
    return pl.pallas_call(
        functools.partial(_combine_kernel, cap=cap, win=win),
        grid_spec=grid_spec,
        out_shape=jax.ShapeDtypeStruct((N, D_MODEL), F32),
        compiler_params=_cparams(48, 1),
        name="combine",
    )(tab, x1, pos, p, ye, wpg, wpp, gp, gfin)


def _t5_bucket(rel):
    nb = NUM_BUCKETS // 2
    ret = jnp.where(rel > 0, nb, 0)
    n = jnp.abs(rel)
    max_exact = nb // 2
    large = max_exact + (jnp.log(jnp.maximum(n, 1).astype(F32) / max_exact)
                         / math.log(MAX_DISTANCE / max_exact) * (nb - max_exact)).astype(jnp.int32)
    large = jnp.minimum(large, nb - 1)
    return ret + jnp.where(n < max_exact, n, large)


def _bucket_table(tq):
    kw = tq + 2 * WINDOW
    rel = jnp.arange(kw)[:, None] - WINDOW - jnp.arange(tq)[None, :]
    code = jnp.where(jnp.abs(rel) <= WINDOW, _t5_bucket(rel) + 1, 0).astype(jnp.int32)
    return code & (2 * NUM_BUCKETS - 1)


def _rope_tables(T):
    rows = T // GRID_W
    row = jnp.repeat(jnp.arange(rows), GRID_W).astype(F32)
    col = jnp.tile(jnp.arange(GRID_W), rows).astype(F32)
    freqs = ROPE_THETA ** (-jnp.arange(AX_PAIRS, dtype=F32) / AX_PAIRS)
    ang = jnp.concatenate([row[:, None] * freqs, col[:, None] * freqs], axis=-1)
    cos, sin = jnp.cos(ang), jnp.sin(ang)
    spare = jnp.zeros((T, LANES - HEAD_DIM), F32)
    return cos.T, sin.T, jnp.concatenate([cos, cos, spare], axis=1), jnp.concatenate([-sin, sin, spare], axis=1)


def _prep_params(g_attn, w_in, gq_b, gk_b, w_out, w_router, w_gate, w_up, w_down, w_ple_gate, w_ple_proj):
    w = w_in[0]
    o = np.cumsum([0, D_MIX, D_KV, D_KV, D_MIX, D_KV, D_KV])
    wqa, wka, wva, wqb, wkb, wvb = (w[:, o[i]:o[i + 1]] for i in range(6))
    halves = jnp.concatenate([jnp.arange(0, HEAD_DIM, 2), jnp.arange(1, HEAD_DIM, 2)])
    perm_q = (jnp.arange(N_HEADS)[:, None] * HEAD_DIM + halves[None, :]).reshape(-1)
    perm_k = (jnp.arange(N_KV)[:, None] * HEAD_DIM + halves[None, :]).reshape(-1)

    def spread(a):
        z = jnp.zeros(a.shape[:-1] + (LANES - HEAD_DIM,), a.dtype)
        return jnp.concatenate([a[..., :HEAD_DIM], z, a[..., HEAD_DIM:], z], axis=-1)

    w1 = jnp.concatenate([spread(wka), spread(wkb[:, perm_k])], axis=1).astype(BF16)
    w2t = jnp.concatenate([wqa.T, wqb[:, perm_q].T, wva.T, wvb.T], axis=0).astype(BF16)
    hd = jnp.arange(D_KX) // HEAD_DIM
    bd = (hd[:, None] == hd[None, :]).astype(BF16)
    gq = (gq_b[0] * QSCALE)[halves]
    gk = gk_b[0][halves]
    wr = jnp.pad(w_router[0], ((0, 0), (0, LANES - N_EXPERTS)))
    return dict(
        g_attn=g_attn[0][None], w1=w1, w2t=w2t, bd=bd,
        gq=jnp.broadcast_to(gq[:, None], (HEAD_DIM, LANES)),
        gk=spread(jnp.tile(gk, N_KV))[None],
        wo=w_out[0].astype(BF16), wr=wr.astype(BF16),
        wg=w_gate[0].astype(BF16), wu=w_up[0].astype(BF16), wd=w_down[0].astype(BF16),
        wpg=w_ple_gate[0].astype(BF16), wpp=w_ple_proj[0].astype(BF16))


TOKEN_TILE = 512
WINDOW_Q_TILE = 256
DENSE_Q_TILE = 512
DENSE_KEY_CHUNK = 512
SELECT_SLOT_CHUNK = 1024
FFN_ROWS = 512
COMBINE_TILE = 256
COMBINE_WINDOW = 128


def _trunk(x, p, prm, sink, rel_bias, g_out_a, g_out_b, g_ffn, g_ple, g_final):
    B, T, _ = x.shape
    N = B * T
    cap = CAPACITY_FACTOR * N // N_EXPERTS
    qat, qbt, vat, vbt, ka, kb = _qkv(x, prm, *_rope_tables(T), tn=TOKEN_TILE)
    oa = _attn_a(qat, ka, vat, sink, rel_bias, _bucket_table(min(WINDOW_Q_TILE, T)))
    ob = _attn_b(qbt, kb, vbt, tq=DENSE_Q_TILE, kc=DENSE_KEY_CHUNK)
    x1, h2s, lg = _post(oa.reshape(N, D_MIX), ob.reshape(N, D_MIX), x.reshape(N, D_MODEL),
                        g_out_a, g_out_b, prm["wo"], g_ffn, prm["wr"], tn=TOKEN_TILE)
    lg3 = lg[:, :N_EXPERTS].T.reshape(N_EXPERTS, N // LANES, LANES)
    idx, gate, pos, bex = _select(lg3, cap=cap, sc=SELECT_SLOT_CHUNK)
    rows = min(FFN_ROWS, cap)
    ye = _ffn(idx.reshape(-1, 1, rows), gate.reshape(-1, 1, rows), h2s, prm["wg"], prm["wu"], prm["wd"], cap=cap)
    tt = min(COMBINE_TILE, N)
    win = min(COMBINE_WINDOW, cap)
    tile_start = bex.reshape(N_EXPERTS, N // LANES)[:, ::tt // LANES]
    tab = jnp.concatenate([tile_start, jnp.full((N_EXPERTS, 1), cap, jnp.int32)], axis=1)
    y = _combine(tab, x1, pos.reshape(N_EXPERTS, N), p.reshape(N, D_PLE), ye, prm["wpg"], prm["wpp"],
                 g_ple, g_final, cap=cap, tt=tt, win=win)
    return y.reshape(B, T, D_MODEL)


def kernel(x_prompt, x_sample, p_prompt, p_sample, g_attn, w_in, sink_a, rel_bias, gq_b, gk_b, g_out_a, g_out_b, w_out, g_ffn, w_router, w_gate, w_up, w_down, g_ple, w_ple_gate, w_ple_proj, g_final):
    prm = _prep_params(g_attn, w_in, gq_b, gk_b, w_out, w_router, w_gate, w_up, w_down, w_ple_gate, w_ple_proj)
    args = (prm, sink_a[0], rel_bias, g_out_a[0][None], g_out_b[0][None], g_ffn[0][None], g_ple[0][None],
            g_final[None])
    y_prompt = _trunk(x_prompt, p_prompt[0], *args)
    y_sample = _trunk(x_sample, p_sample[0], *args)
    return (y_prompt, y_sample)
```

```python
import functools
import math

import jax
import jax.numpy as jnp
import numpy as np
from jax import lax
from jax.experimental import pallas as pl
from jax.experimental.pallas import tpu as pltpu

D_MODEL = 1024
HEAD_DIM = 64
N_HEADS = 8
N_KV = 2
GROUP = N_HEADS // N_KV
D_MIX = N_HEADS * HEAD_DIM
D_KV = N_KV * HEAD_DIM
D_GRP = GROUP * HEAD_DIM
WINDOW = 128
NUM_BUCKETS = 32
MAX_DISTANCE = 128
GRID_W = 64
ROPE_THETA = 10000.0
AX_PAIRS = HEAD_DIM // 4
N_EXPERTS = 16
CAPACITY_FACTOR = 2
D_EXPERT = 512
D_PLE = 256
EPS = 1e-6
NEG = -1e30
LOG2E = math.log2(math.e)
QSCALE = HEAD_DIM ** -0.5 * LOG2E

LANES = 128
SUBLANES = 8
ROW_SLABS = D_MODEL // LANES

F32 = jnp.float32
BF16 = jnp.bfloat16

_NT = (((1,), (1,)), ((), ()))
_TN = (((0,), (0,)), ((), ()))


def _cparams(mib, n_axes, flags=None):
    return pltpu.CompilerParams(vmem_limit_bytes=mib * 1024 * 1024,
                                dimension_semantics=("arbitrary",) * n_axes, flags=flags)


def _const_spec(shape):
    zeros = (0,) * len(shape)
    return pl.BlockSpec(shape, lambda *_: zeros)


def _rms(x, g):
    r = lax.rsqrt(jnp.mean(x * x, axis=-1, keepdims=True) + EPS)
    return (x * r) * g


def _split_bf16(x):
    hi = x.astype(BF16)
    lo = (x - hi.astype(F32)).astype(BF16)
    return hi, lo


_Q_ROWS = 2 * D_MIX
D_KX = N_KV * LANES
HALF = HEAD_DIM // 2


def _qkv_kernel(x_ref, g_ref, w1_ref, w2t_ref, bd_ref, gq_ref, gk_ref, ct_ref, st_ref, ck_ref, sk_ref,
                qat_ref, qbt_ref, vat_ref, vbt_ref, ka_ref, kb_ref):
    tn = x_ref.shape[1]
    h = _rms(x_ref[0], g_ref[...]).astype(BF16)
    p1 = jnp.dot(h, w1_ref[...], preferred_element_type=F32)
    p2 = lax.dot_general(w2t_ref[...], h, _NT, preferred_element_type=F32)

    qat_ref[0] = (p2[0:D_MIX] * QSCALE).astype(BF16)
    vat_ref[0] = p2[_Q_ROWS:_Q_ROWS + D_KV].astype(BF16)
    vbt_ref[0] = p2[_Q_ROWS + D_KV:_Q_ROWS + 2 * D_KV].astype(BF16)
    ka_ref[0] = p1[:, 0:D_KX].astype(BF16)

    q = p2[D_MIX:2 * D_MIX].reshape(N_HEADS, HEAD_DIM, tn)
    rq = lax.rsqrt(jnp.sum(q * q, axis=1, keepdims=True) * (1.0 / HEAD_DIM) + EPS)
    gq = jnp.concatenate([gq_ref[...]] * (tn // LANES), axis=1)[None]
    qn = (q * rq) * gq
    x0, x1 = qn[:, :HALF], qn[:, HALF:]
    c, s = ct_ref[...][None], st_ref[...][None]
    qbt_ref[0] = jnp.concatenate([x0 * c - x1 * s, x0 * s + x1 * c], axis=1).reshape(D_MIX, tn).astype(BF16)

    k = p1[:, D_KX:2 * D_KX]
    hi, lo = _split_bf16(k * k)
    ssq = (jnp.dot(hi, bd_ref[...], preferred_element_type=F32)
           + jnp.dot(lo, bd_ref[...], preferred_element_type=F32))
    kn = (k * lax.rsqrt(ssq * (1.0 / HEAD_DIM) + EPS)) * gk_ref[...]
    lane = lax.broadcasted_iota(jnp.int32, (tn, LANES), 1)
    outs = []
    for hk in range(N_KV):
        kh = kn[:, hk * LANES:(hk + 1) * LANES]
        partner = jnp.where(lane < HALF, pltpu.roll(kh, LANES - HALF, 1), pltpu.roll(kh, HALF, 1))
        outs.append(kh * ck_ref[...] + partner * sk_ref[...])
    kb_ref[0] = jnp.concatenate(outs, axis=1).astype(BF16)


def _qkv(x, prm, ct, st, ck, sk, *, tn):
    B, T, _ = x.shape
    tn = min(tn, T)
    grid = (B, T // tn)
    tok = lambda w: pl.BlockSpec((1, tn, w), lambda b, j: (b, j, 0))
    feat = lambda r: pl.BlockSpec((1, r, tn), lambda b, j: (b, 0, j))
    return pl.pallas_call(
        _qkv_kernel,
        grid=grid,
        in_specs=[tok(D_MODEL), _const_spec((1, D_MODEL)), _const_spec((D_MODEL, 2 * D_KX)),
                  _const_spec((_Q_ROWS + 2 * D_KV, D_MODEL)), _const_spec((D_KX, D_KX)),
                  _const_spec((HEAD_DIM, LANES)), _const_spec((1, D_KX)),
                  pl.BlockSpec((HALF, tn), lambda b, j: (0, j)),
                  pl.BlockSpec((HALF, tn), lambda b, j: (0, j)),
                  pl.BlockSpec((tn, LANES), lambda b, j: (j, 0)),
                  pl.BlockSpec((tn, LANES), lambda b, j: (j, 0))],
        out_specs=[feat(D_MIX), feat(D_MIX), feat(D_KV), feat(D_KV), tok(D_KX), tok(D_KX)],
        out_shape=[jax.ShapeDtypeStruct((B, D_MIX, T), BF16), jax.ShapeDtypeStruct((B, D_MIX, T), BF16),
                   jax.ShapeDtypeStruct((B, D_KV, T), BF16), jax.ShapeDtypeStruct((B, D_KV, T), BF16),
                   jax.ShapeDtypeStruct((B, T, D_KX), BF16), jax.ShapeDtypeStruct((B, T, D_KX), BF16)],
        compiler_params=_cparams(48, 2),
        name="qkv",
    )(x, prm["g_attn"], prm["w1"], prm["w2t"], prm["bd"], prm["gq"], prm["gk"], ct, st, ck, sk)


ONES_ROWS = 16
V1_ROWS = HEAD_DIM + ONES_ROWS


def _head_weights(qt_ref, g, extra):
    return jnp.concatenate([qt_ref[0, g * HEAD_DIM:(g + 1) * HEAD_DIM, :], extra], axis=0)


def _normalised(acc):
    return acc[0:HEAD_DIM] * (1.0 / acc[HEAD_DIM:HEAD_DIM + 1])


def _attn_b_kernel(qt_ref, k_ref, vt_ref, o_ref, v1_ref, s0_ref, s1_ref, p0_ref, p1_ref, a0_ref, a1_ref,
                   acc_ref, m_ref, *, kc):
    s_refs, p_refs, a_refs = (s0_ref, s1_ref), (p0_ref, p1_ref), (a0_ref, a1_ref)
    T = k_ref.shape[1]
    tq = qt_ref.shape[2]
    n_chunks = T // kc

    @pl.when(pl.program_id(2) == 0)
    def _():
        v1_ref[0:HEAD_DIM, :] = vt_ref[0]
        v1_ref[HEAD_DIM:V1_ROWS, :] = jnp.ones((ONES_ROWS, T), BF16)

    acc_ref[...] = jnp.zeros(acc_ref.shape, F32)
    m_ref[...] = jnp.full(m_ref.shape, NEG, F32)
    spare = jnp.zeros((HEAD_DIM, tq), BF16)

    heads = range(GROUP)

    def scores(c, slot, hs=heads):
        kblk = k_ref[0, pl.ds(pl.multiple_of(c * kc, kc), kc), :]
        for g in hs:
            s_refs[slot][g] = jnp.dot(kblk, _head_weights(qt_ref, g, spare),
                                      preferred_element_type=F32).astype(BF16)

    def softmax(slot, hs=heads):
        for g in hs:
            s = s_refs[slot][g]
            m_old = m_ref[g]
            m_new = jnp.maximum(m_old, jnp.max(s, axis=0, keepdims=True).astype(F32))
            p_refs[slot][g] = jnp.exp2(s - m_new.astype(BF16))
            a_refs[slot][g] = jnp.exp2(m_old - m_new)
            m_ref[g] = m_new

    def values(c, slot, hs=heads):
        v1 = v1_ref[:, pl.ds(pl.multiple_of(c * kc, kc), kc)]
        for g in hs:
            acc_ref[g] = (a_refs[slot][g] * acc_ref[g]
                          + jnp.dot(v1, p_refs[slot][g], preferred_element_type=F32))

    scores(0, 0)
    softmax(0)
    scores(1, 1)

    def body(j, carry):
        c = 2 * j
        for g in heads:
            scores(c, 0, (g,))
            softmax(1, (g,))
            values(c - 2, 0, (g,))
        for g in heads:
            scores(c + 1, 1, (g,))
            softmax(0, (g,))
            values(c - 1, 1, (g,))
        return carry

    lax.fori_loop(1, n_chunks // 2, body, 0)
    values(n_chunks - 2, 0)
    softmax(1)
    values(n_chunks - 1, 1)
    o_ref[0] = jnp.concatenate([_normalised(acc_ref[g]) for g in range(GROUP)], axis=0).T


def _attn_b(qt, k, vt, *, tq, kc):
    B, _, T = qt.shape
    tq = min(tq, T)
    kc = min(kc, T // 2)
    assert T % (2 * kc) == 0
    grid = (B, N_KV, T // tq)
    return pl.pallas_call(
        functools.partial(_attn_b_kernel, kc=kc),
        grid=grid,
        in_specs=[pl.BlockSpec((1, D_GRP, tq), lambda b, h, i: (b, h, i)),
                  pl.BlockSpec((1, T, LANES), lambda b, h, i: (b, 0, h)),
                  pl.BlockSpec((1, HEAD_DIM, T), lambda b, h, i: (b, h, 0))],
        out_specs=pl.BlockSpec((1, tq, D_GRP), lambda b, h, i: (b, i, h)),
        out_shape=jax.ShapeDtypeStruct((B, T, D_MIX), F32),
        scratch_shapes=[pltpu.VMEM((V1_ROWS, T), BF16),
                        pltpu.VMEM((GROUP, kc, tq), BF16), pltpu.VMEM((GROUP, kc, tq), BF16),
                        pltpu.VMEM((GROUP, kc, tq), BF16), pltpu.VMEM((GROUP, kc, tq), BF16),
                        pltpu.VMEM((GROUP, 1, tq), F32), pltpu.VMEM((GROUP, 1, tq), F32),
                        pltpu.VMEM((GROUP, V1_ROWS, tq), F32),
                        pltpu.VMEM((GROUP, 1, tq), F32)],
        compiler_params=_cparams(48, 3),
        name="attn_b",
    )(qt, k, vt)


HEADS_PER_DOT = 2


def _attn_a_kernel(sink_ref, rb_ref, bucket_ref, qt_ref, k_ref, vt_ref, o_ref, kp_ref, v1_ref, bias_ref):
    T = k_ref.shape[1]
    tq = qt_ref.shape[2]
    kw = tq + 2 * WINDOW
    kvh = pl.program_id(1)
    qi = pl.program_id(2)

    @pl.when((pl.program_id(0) == 0) & (kvh == 0) & (qi == 0))
    def _():
        lane = lax.broadcasted_iota(jnp.int32, (WINDOW, LANES), 1)
        off_seq = (lane == HEAD_DIM).astype(BF16)
        kp_ref[0:WINDOW, :] = off_seq
        kp_ref[WINDOW + T:, :] = off_seq
        v1_ref[0:HEAD_DIM, 0:WINDOW] = jnp.zeros((HEAD_DIM, WINDOW), BF16)
        v1_ref[0:HEAD_DIM, WINDOW + T:] = jnp.zeros((HEAD_DIM, WINDOW), BF16)
        v1_ref[HEAD_DIM:V1_ROWS, :] = jnp.ones((ONES_ROWS, T + 2 * WINDOW), BF16)
        bucket = bucket_ref[...]
        for h in range(N_HEADS):
            bias = jnp.full((kw, tq), NEG, F32)
            for k in range(NUM_BUCKETS):
                bias = jnp.where(bucket == k + 1, rb_ref[k, h], bias)
            bias_ref[h] = bias * LOG2E

    @pl.when(qi == 0)
    def _():
        kp_ref[WINDOW:WINDOW + T, :] = k_ref[0]
        v1_ref[0:HEAD_DIM, WINDOW:WINDOW + T] = vt_ref[0]

    start = pl.multiple_of(qi * tq, tq)
    kblk = kp_ref[pl.ds(start, kw), :]
    v1 = v1_ref[:, pl.ds(start, kw)]
    spare_row = lax.broadcasted_iota(jnp.int32, (HEAD_DIM, tq), 0)
    spare = jnp.where(spare_row == 0, NEG, 0.0).astype(BF16)
    def scores(heads):
        w = jnp.concatenate([_head_weights(qt_ref, g, spare) for g in heads], axis=1)
        bias = jnp.concatenate([bias_ref[kvh * GROUP + g] for g in heads], axis=1)
        return (jnp.dot(kblk, w, preferred_element_type=F32) + bias).astype(BF16)

    head_sets = [tuple(range(h, h + HEADS_PER_DOT)) for h in range(0, GROUP, HEADS_PER_DOT)]
    outs = []
    s_next = scores(head_sets[0])
    for i, heads in enumerate(head_sets):
        s = s_next
        if i + 1 < len(head_sets):
            s_next = scores(head_sets[i + 1])
        sink = jnp.concatenate([jnp.full((1, tq), sink_ref[kvh * GROUP + g], F32) for g in heads], axis=1) * LOG2E
        m = jnp.maximum(jnp.max(s, axis=0, keepdims=True).astype(F32), sink).astype(BF16)
        pv = jnp.dot(v1, jnp.exp2(s - m), preferred_element_type=F32)
        den = pv[HEAD_DIM:HEAD_DIM + 1] + jnp.exp2(sink - m.astype(F32))
        o = pv[0:HEAD_DIM] * (1.0 / den)
        outs.extend(o[:, j * tq:(j + 1) * tq] for j in range(len(heads)))
    o_ref[0] = jnp.concatenate(outs, axis=0).T


def _attn_a(qt, k, vt, sink, rel_bias, bucket):
    B, _, T = qt.shape
    kw, tq = bucket.shape
    grid = (B, N_KV, T // tq)
    smem = pl.BlockSpec(memory_space=pltpu.SMEM)
    return pl.pallas_call(
        _attn_a_kernel,
        grid=grid,
        in_specs=[smem, smem, _const_spec((kw, tq)),
                  pl.BlockSpec((1, D_GRP, tq), lambda b, h, i: (b, h, i)),
                  pl.BlockSpec((1, T, LANES), lambda b, h, i: (b, 0, h)),
                  pl.BlockSpec((1, HEAD_DIM, T), lambda b, h, i: (b, h, 0))],
        out_specs=pl.BlockSpec((1, tq, D_GRP), lambda b, h, i: (b, i, h)),
        out_shape=jax.ShapeDtypeStruct((B, T, D_MIX), F32),
        scratch_shapes=[pltpu.VMEM((T + 2 * WINDOW, LANES), BF16),
                        pltpu.VMEM((V1_ROWS, T + 2 * WINDOW), BF16),
                        pltpu.VMEM((N_HEADS, kw, tq), F32)],
        compiler_params=_cparams(48, 3),
        name="attn_a",
    )(sink, rel_bias, bucket, qt, k, vt)


def _post_kernel(oa_ref, ob_ref, x_ref, ga_ref, gb_ref, wo_ref, gf_ref, wr_ref,
                 x1_ref, h2s_ref, lg_ref):
    tn = x_ref.shape[0]
    mixed = jnp.concatenate([_rms(oa_ref[...], ga_ref[...]), _rms(ob_ref[...], gb_ref[...])], axis=1)
    x1 = x_ref[...] + jnp.dot(mixed.astype(BF16), wo_ref[...], preferred_element_type=F32)
    x1_ref[...] = x1
    h2 = _rms(x1, gf_ref[...])
    for s in range(ROW_SLABS):
        h2s_ref[pl.ds(s, tn, stride=ROW_SLABS), :] = h2[:, s * LANES:(s + 1) * LANES]
    lg_ref[...] = lax.dot_general(wr_ref[...], h2.astype(BF16), _NT, preferred_element_type=F32)


def _post(oa, ob, x, ga, gb, wo, gf, wr, *, tn):
    N = x.shape[0]
    tn = min(tn, N)
    tok = lambda w: pl.BlockSpec((tn, w), lambda j: (j, 0))
    return pl.pallas_call(
        _post_kernel,
        grid=(N // tn,),
        in_specs=[tok(D_MIX), tok(D_MIX), tok(D_MODEL), _const_spec((1, D_MIX)), _const_spec((1, D_MIX)),
                  _const_spec((D_MODEL, D_MODEL)), _const_spec((1, D_MODEL)),
                  _const_spec((N_EXPERTS, D_MODEL))],
        out_specs=[tok(D_MODEL), pl.BlockSpec((ROW_SLABS * tn, LANES), lambda j: (j, 0)),
                   pl.BlockSpec((N_EXPERTS, tn), lambda j: (0, j))],
        out_shape=[jax.ShapeDtypeStruct((N, D_MODEL), F32),
                   jax.ShapeDtypeStruct((ROW_SLABS * N, LANES), F32),
                   jax.ShapeDtypeStruct((N_EXPERTS, N), F32)],
        compiler_params=_cparams(48, 1),
        name="post",
    )(oa, ob, x, ga, gb, wo, gf, wr)


def _select_kernel(lg_ref, idx_ref, gate_ref, pos_ref, bex_ref, *, cap, sc):
    e = pl.program_id(0)
    nb = lg_ref.shape[1]
    lg = lg_ref[...]
    mx = jnp.max(lg, axis=0)
    den = jnp.sum(jnp.exp(lg - mx), axis=0)
    aff = jnp.exp(lg_ref[e] - mx) / den

    def bisect(_, lohi):
        lo, hi = lohi
        mid = lo + jnp.right_shift(hi - lo, 1)
        thr = lax.bitcast_convert_type(mid, F32)
        cnt = jnp.sum((aff >= thr).astype(jnp.int32))
        ok = cnt >= cap
        return jnp.where(ok, mid, lo), jnp.where(ok, hi, mid)

    lo0 = jnp.zeros((1, 1), jnp.int32)
    hi0 = jnp.full((1, 1), 0x7F800000, jnp.int32)
    lo, hi = lax.fori_loop(0, 31, bisect, (lo0, hi0))
    above = aff >= lax.bitcast_convert_type(hi, F32)
    tie = (aff >= lax.bitcast_convert_type(lo, F32)) & jnp.logical_not(above)
    need = cap - jnp.sum(above.astype(jnp.int32))

    li = lax.broadcasted_iota(jnp.int32, (LANES, LANES), 0)
    lj = lax.broadcasted_iota(jnp.int32, (LANES, LANES), 1)
    ut_incl = (li <= lj).astype(BF16)
    ut_excl = (li < lj).astype(BF16)
    bi = lax.broadcasted_iota(jnp.int32, (nb, nb), 0)
    bj = lax.broadcasted_iota(jnp.int32, (nb, nb), 1)
    lt_incl = (bj <= bi).astype(BF16)
    lt_excl = (bj < bi).astype(BF16)

    def last(x):
        return x[:, LANES - 1:LANES]

    tie_b = tie.astype(BF16)
    tie_excl = jnp.dot(tie_b, ut_excl, preferred_element_type=F32)
    tie_incl = jnp.dot(tie_b, ut_incl, preferred_element_type=F32)
    tie_before = last(jnp.dot(lt_excl, tie_incl.astype(BF16), preferred_element_type=F32))
    tie_rank = (tie_before + tie_excl).astype(jnp.int32)
    sel = above | (tie & (tie_rank < need))

    lcum = jnp.dot(sel.astype(BF16), ut_incl, preferred_element_type=F32)
    lcum_b = lcum.astype(BF16)
    bcum_incl = last(jnp.dot(lt_incl, lcum_b, preferred_element_type=F32))
    bcum_excl = bcum_incl - last(lcum)
    pos_ref[0] = jnp.where(sel, (bcum_excl + lcum).astype(jnp.int32) - 1, -1)
    bex_ref[0] = bcum_excl.astype(jnp.int32)

    a1 = aff.astype(BF16).astype(F32)
    a2 = (aff - a1).astype(BF16).astype(F32)
    excl_i = bcum_excl.astype(jnp.int32)
    blk_i = lax.broadcasted_iota(jnp.int32, (nb, 1), 0)
    digits = [jnp.right_shift(excl_i, 7), excl_i & (LANES - 1), jnp.right_shift(blk_i, 7), blk_i & (LANES - 1)]
    side = jnp.concatenate([d.astype(F32) for d in digits] + [jnp.zeros((nb, LANES - len(digits)), F32)], axis=1)
    tables_t = jnp.concatenate([lcum, side, a1, a2], axis=1).T.astype(BF16)
    row = lax.broadcasted_iota(jnp.int32, (LANES, sc), 0)
    for c in range(cap // sc):
        slot = (c * sc + lax.broadcasted_iota(jnp.int32, (1, sc), 1)).astype(F32)
        hit = ((bcum_excl <= slot) & (slot < bcum_incl)).astype(BF16)
        got = jnp.dot(tables_t, hit, preferred_element_type=F32)
        lc_t = got[0:LANES]
        base = got[LANES:LANES + 1] * LANES + got[LANES + 1:LANES + 2]
        blk_of = (got[LANES + 2:LANES + 3] * LANES + got[LANES + 3:LANES + 4]).astype(jnp.int32)
        j_of = jnp.sum((lc_t <= slot - base).astype(jnp.int32), axis=0, keepdims=True)
        idx_ref[0, :, c * sc:(c + 1) * sc] = blk_of * LANES + j_of
        aff_t = got[2 * LANES:3 * LANES] + got[3 * LANES:4 * LANES]
        gate_ref[0, :, c * sc:(c + 1) * sc] = jnp.sum(jnp.where(row == j_of, aff_t, 0.0), axis=0, keepdims=True)


def _select(lg3, *, cap, sc):
    E, nb, _ = lg3.shape
    sc = min(sc, cap)
    per_e = lambda r, c, dt: (pl.BlockSpec((1, r, c), lambda e: (e, 0, 0)), jax.ShapeDtypeStruct((E, r, c), dt))
    specs = [per_e(1, cap, jnp.int32), per_e(1, cap, F32), per_e(nb, LANES, jnp.int32), per_e(nb, 1, jnp.int32)]
    return pl.pallas_call(
        functools.partial(_select_kernel, cap=cap, sc=sc),
        grid=(E,),
        in_specs=[_const_spec((E, nb, LANES))],
        out_specs=[s for s, _ in specs],
        out_shape=[o for _, o in specs],
        compiler_params=_cparams(48, 1),
        name="select",
    )(lg3)


GATHER_UNROLL = 8


def _ffn_kernel(idx_ref, idxn_ref, gate_ref, h2s_ref, wg_ref, wu_ref, wd_ref, ye_ref, xbuf_ref, sem_ref):
    rows = ye_ref.shape[0]
    n = pl.program_id(0)
    n_steps = pl.num_programs(0)
    slot = n % 2

    def row_copy(ids_ref, r, to_slot):
        src = pl.multiple_of(ids_ref[0, 0, r] * ROW_SLABS, ROW_SLABS)
        return pltpu.make_async_copy(h2s_ref.at[pl.ds(src, ROW_SLABS), :],
                                     xbuf_ref.at[to_slot, pl.ds(r * ROW_SLABS, ROW_SLABS), :],
                                     sem_ref.at[to_slot])

    def start_gather(ids_ref, to_slot):
        def body(r0, carry):
            for u in range(GATHER_UNROLL):
                row_copy(ids_ref, r0 * GATHER_UNROLL + u, to_slot).start(priority=u % 2)
            return carry
        lax.fori_loop(0, rows // GATHER_UNROLL, body, 0)

    def wait_slot(s):
        pltpu.make_async_copy(h2s_ref.at[pl.ds(0, rows * ROW_SLABS), :], xbuf_ref.at[s], sem_ref.at[s]).wait()

    @pl.when(n == 0)
    def _():
        start_gather(idx_ref, 0)

    @pl.when(n + 1 < n_steps)
    def _():
        for r in range(rows):
            row_copy(idxn_ref, r, 1 - slot).start(priority=r % 2)

    wait_slot(slot)
    x = jnp.concatenate(
        [xbuf_ref[slot, pl.ds(s, rows, stride=ROW_SLABS), :].astype(BF16) for s in range(ROW_SLABS)], axis=1)
    a = jnp.dot(x, wg_ref[0], preferred_element_type=F32)
    u = jnp.dot(x, wu_ref[0], preferred_element_type=F32)
    hmid = (jax.nn.silu(a) * u).astype(BF16)
    y = jnp.dot(hmid, wd_ref[0], preferred_element_type=F32)
    ri = lax.broadcasted_iota(jnp.int32, (rows, rows), 0)
    ci = lax.broadcasted_iota(jnp.int32, (rows, rows), 1)
    gate_col = jnp.sum(jnp.where(ri == ci, gate_ref[0], 0.0), axis=1, keepdims=True)
    ye_ref[...] = (gate_col * y).astype(BF16)


def _ffn(idx3, gate3, h2s, wg, wu, wd, *, cap):
    n_steps, _, rows = idx3.shape
    per_e = cap // rows
    last = n_steps - 1
    return pl.pallas_call(
        _ffn_kernel,
        grid=(n_steps,),
        in_specs=[pl.BlockSpec((1, 1, rows), lambda n: (n, 0, 0), memory_space=pltpu.SMEM),
                  pl.BlockSpec((1, 1, rows), lambda n: (jnp.minimum(n + 1, last), 0, 0), memory_space=pltpu.SMEM),
                  pl.BlockSpec((1, 1, rows), lambda n: (n, 0, 0)),
                  pl.BlockSpec(memory_space=pl.ANY),
                  pl.BlockSpec((1, D_MODEL, D_EXPERT), lambda n: (n // per_e, 0, 0)),
                  pl.BlockSpec((1, D_MODEL, D_EXPERT), lambda n: (n // per_e, 0, 0)),
                  pl.BlockSpec((1, D_EXPERT, D_MODEL), lambda n: (n // per_e, 0, 0))],
        out_specs=pl.BlockSpec((rows, D_MODEL), lambda n: (n, 0)),
        out_shape=jax.ShapeDtypeStruct((n_steps * rows, D_MODEL), BF16),
        scratch_shapes=[pltpu.VMEM((2, rows * ROW_SLABS, LANES), F32), pltpu.SemaphoreType.DMA((2,))],
        compiler_params=_cparams(48, 1),
        name="ffn",
    )(idx3, idx3, gate3, h2s, wg, wu, wd)


BF16_ROWS = 16
MXU_DEPTH = 256


def _combine_kernel(tab_ref, x1_ref, pos_ref, p_ref, ye_ref, wpg_ref, wpp_ref, gp_ref, gfin_ref,
                    y_ref, win_ref, sem_ref, more_ref, msem_ref, acc_ref, *, cap, win):
    tt = x1_ref.shape[0]
    i = pl.program_id(0)
    n_tiles = pl.num_programs(0)
    slot = i % 2
    total = N_EXPERTS * cap
    per_dot = MXU_DEPTH // win
    wi = lax.broadcasted_iota(jnp.int32, (win, tt), 0)

    def first_row(e, tile):
        first = e * cap + tab_ref[e, tile]
        aligned = lax.shift_left(lax.shift_right_logical(first, 4), 4)
        return jnp.minimum(aligned, total - win)

    def fetch_windows(tile, to_slot):
        for e in range(N_EXPERTS):
            a0 = pl.multiple_of(first_row(e, tile), BF16_ROWS)
            pltpu.make_async_copy(ye_ref.at[pl.ds(a0, win), :], win_ref.at[to_slot, pl.ds(e * win, win), :],
                                  sem_ref.at[to_slot]).start()

    @pl.when(i == 0)
    def _():
        fetch_windows(0, 0)

    @pl.when(i + 1 < n_tiles)
    def _():
        fetch_windows(i + 1, 1 - slot)

    pltpu.make_async_copy(ye_ref.at[pl.ds(0, N_EXPERTS * win), :], win_ref.at[slot], sem_ref.at[slot]).wait()

    def onehot(e, a0):
        pos = pos_ref[pl.ds(e, 1), :]
        return (((pos + (e * cap - a0)) == wi) & (pos >= 0)).astype(BF16)

    def rows_from(e, a0):
        return e * cap + tab_ref[e, i + 1] - a0

    acc = None
    for j in range(N_EXPERTS // per_dot):
        es = range(j * per_dot, (j + 1) * per_dot)
        hot = jnp.concatenate([onehot(e, first_row(e, i)) for e in es], axis=0)
        part = lax.dot_general(hot, win_ref[slot, j * MXU_DEPTH:(j + 1) * MXU_DEPTH, :], _TN,
                               preferred_element_type=F32)
        acc = part if acc is None else acc + part
    acc_ref[...] = acc

    longest = rows_from(0, first_row(0, i))
    for e in range(1, N_EXPERTS):
        longest = jnp.maximum(longest, rows_from(e, first_row(e, i)))

    @pl.when(longest > win)
    def _():
        def per_expert(e, carry):
            a0 = first_row(e, i)
            n_windows = lax.shift_right_logical(rows_from(e, a0) + (win - 1), win.bit_length() - 1)

            def more(c, carry):
                a = pl.multiple_of(jnp.minimum(a0 + c * win, total - win), BF16_ROWS)
                copy = pltpu.make_async_copy(ye_ref.at[pl.ds(a, win), :], more_ref, msem_ref.at[0])
                copy.start()
                copy.wait()
                acc_ref[...] += lax.dot_general(onehot(e, a), more_ref[...], _TN, preferred_element_type=F32)
                return carry
            return lax.fori_loop(1, n_windows, more, carry)
        lax.fori_loop(0, N_EXPERTS, per_expert, 0)

    x2 = x1_ref[...] + acc_ref[...]
    gate = jax.nn.sigmoid(jnp.dot(_rms(x2, gp_ref[...]).astype(BF16), wpg_ref[...], preferred_element_type=F32))
    x3 = x2 + gate * jnp.dot(p_ref[...].astype(BF16), wpp_ref[...], preferred_element_type=F32)
    y_ref[...] = _rms(x3, gfin_ref[...])


def _combine(tab, x1, pos, p, ye, wpg, wpp, gp, gfin, *, cap, tt, win):
    N = x1.shape[0]
    grid_spec = pltpu.PrefetchScalarGridSpec(
        num_scalar_prefetch=1,
        grid=(N // tt,),
        in_specs=[pl.BlockSpec((tt, D_MODEL), lambda i, t: (i, 0)),
                  pl.BlockSpec((N_EXPERTS, tt), lambda i, t: (0, i)),
                  pl.BlockSpec((tt, D_PLE), lambda i, t: (i, 0)),
                  pl.BlockSpec(memory_space=pl.ANY),
                  pl.BlockSpec((D_MODEL, D_MODEL), lambda i, t: (0, 0)),
                  pl.BlockSpec((D_PLE, D_MODEL), lambda i, t: (0, 0)),
                  pl.BlockSpec((1, D_MODEL), lambda i, t: (0, 0)),
                  pl.BlockSpec((1, D_MODEL), lambda i, t: (0, 0))],
        out_specs=pl.BlockSpec((tt, D_MODEL), lambda i, t: (i, 0)),
        scratch_shapes=[pltpu.VMEM((2, N_EXPERTS * win, D_MODEL), BF16), pltpu.SemaphoreType.DMA((2,)),
                        pltpu.VMEM((win, D_MODEL), BF16), pltpu.SemaphoreType.DMA((1,)),
                        pltpu.VMEM((tt, D_MODEL), F32)],
    )
    return pl.pallas_call(
        functools.partial(_combine_kernel, cap=cap, win=win),
        grid_spec=grid_spec,
        out_shape=jax.ShapeDtypeStruct((N, D_MODEL), F32),
        compiler_params=_cparams(48, 1),
        name="combine",
    )(tab, x1, pos, p, ye, wpg, wpp, gp, gfin)


def _t5_bucket(rel):
    nb = NUM_BUCKETS // 2
    ret = jnp.where(rel > 0, nb, 0)
    n = jnp.abs(rel)
    max_exact = nb // 2
    large = max_exact + (jnp.log(jnp.maximum(n, 1).astype(F32) / max_exact)
                         / math.log(MAX_DISTANCE / max_exact) * (nb - max_exact)).astype(jnp.int32)
    large = jnp.minimum(large, nb - 1)
    return ret + jnp.where(n < max_exact, n, large)


def _bucket_table(tq):
    kw = tq + 2 * WINDOW
    rel = jnp.arange(kw)[:, None] - WINDOW - jnp.arange(tq)[None, :]
    code = jnp.where(jnp.abs(rel) <= WINDOW, _t5_bucket(rel) + 1, 0).astype(jnp.int32)
    return code & (2 * NUM_BUCKETS - 1)


def _rope_tables(T):
    rows = T // GRID_W
    row = jnp.repeat(jnp.arange(rows), GRID_W).astype(F32)
    col = jnp.tile(jnp.arange(GRID_W), rows).astype(F32)
    freqs = ROPE_THETA ** (-jnp.arange(AX_PAIRS, dtype=F32) / AX_PAIRS)
    ang = jnp.concatenate([row[:, None] * freqs, col[:, None] * freqs], axis=-1)
    cos, sin = jnp.cos(ang), jnp.sin(ang)
    spare = jnp.zeros((T, LANES - HEAD_DIM), F32)
    return cos.T, sin.T, jnp.concatenate([cos, cos, spare], axis=1), jnp.concatenate([-sin, sin, spare], axis=1)


def _prep_params(g_attn, w_in, gq_b, gk_b, w_out, w_router, w_gate, w_up, w_down, w_ple_gate, w_ple_proj):
    w = w_in[0]
    o = np.cumsum([0, D_MIX, D_KV, D_KV, D_MIX, D_KV, D_KV])
    wqa, wka, wva, wqb, wkb, wvb = (w[:, o[i]:o[i + 1]] for i in range(6))
    halves = jnp.concatenate([jnp.arange(0, HEAD_DIM, 2), jnp.arange(1, HEAD_DIM, 2)])
    perm_q = (jnp.arange(N_HEADS)[:, None] * HEAD_DIM + halves[None, :]).reshape(-1)
    perm_k = (jnp.arange(N_KV)[:, None] * HEAD_DIM + halves[None, :]).reshape(-1)

    def spread(a):
        z = jnp.zeros(a.shape[:-1] + (LANES - HEAD_DIM,), a.dtype)
        return jnp.concatenate([a[..., :HEAD_DIM], z, a[..., HEAD_DIM:], z], axis=-1)

    w1 = jnp.concatenate([spread(wka), spread(wkb[:, perm_k])], axis=1).astype(BF16)
    w2t = jnp.concatenate([wqa.T, wqb[:, perm_q].T, wva.T, wvb.T], axis=0).astype(BF16)
    hd = jnp.arange(D_KX) // HEAD_DIM
    bd = (hd[:, None] == hd[None, :]).astype(BF16)
    gq = (gq_b[0] * QSCALE)[halves]
    gk = gk_b[0][halves]
    return dict(
        g_attn=g_attn[0][None], w1=w1, w2t=w2t, bd=bd,
        gq=jnp.broadcast_to(gq[:, None], (HEAD_DIM, LANES)),
        gk=spread(jnp.tile(gk, N_KV))[None],
        wo=w_out[0].astype(BF16), wr=w_router[0].T.astype(BF16),
        wg=w_gate[0].astype(BF16), wu=w_up[0].astype(BF16), wd=w_down[0].astype(BF16),
        wpg=w_ple_gate[0].astype(BF16), wpp=w_ple_proj[0].astype(BF16))


TOKEN_TILE = 512
WINDOW_Q_TILE = 256
DENSE_Q_TILE = 512
DENSE_KEY_CHUNK = 512
SELECT_SLOT_CHUNK = 1024
FFN_ROWS = 512
COMBINE_TILE = 256
COMBINE_WINDOW = 128


def _trunk(x, p, prm, sink, rel_bias, g_out_a, g_out_b, g_ffn, g_ple, g_final):
    B, T, _ = x.shape
    N = B * T
    cap = CAPACITY_FACTOR * N // N_EXPERTS
    qat, qbt, vat, vbt, ka, kb = _qkv(x, prm, *_rope_tables(T), tn=TOKEN_TILE)
    oa = _attn_a(qat, ka, vat, sink, rel_bias, _bucket_table(min(WINDOW_Q_TILE, T)))
    ob = _attn_b(qbt, kb, vbt, tq=DENSE_Q_TILE, kc=DENSE_KEY_CHUNK)
    x1, h2s, lg = _post(oa.reshape(N, D_MIX), ob.reshape(N, D_MIX), x.reshape(N, D_MODEL),
                        g_out_a, g_out_b, prm["wo"], g_ffn, prm["wr"], tn=TOKEN_TILE)
    lg3 = lg.reshape(N_EXPERTS, N // LANES, LANES)
    idx, gate, pos, bex = _select(lg3, cap=cap, sc=SELECT_SLOT_CHUNK)
    rows = min(FFN_ROWS, cap)
    ye = _ffn(idx.reshape(-1, 1, rows), gate.reshape(-1, 1, rows), h2s, prm["wg"], prm["wu"], prm["wd"], cap=cap)
    tt = min(COMBINE_TILE, N)
    win = min(COMBINE_WINDOW, cap)
    tile_start = bex.reshape(N_EXPERTS, N // LANES)[:, ::tt // LANES]
    tab = jnp.concatenate([tile_start, jnp.full((N_EXPERTS, 1), cap, jnp.int32)], axis=1)
    y = _combine(tab, x1, pos.reshape(N_EXPERTS, N), p.reshape(N, D_PLE), ye, prm["wpg"], prm["wpp"],
                 g_ple, g_final, cap=cap, tt=tt, win=win)
    return y.reshape(B, T, D_MODEL)


def kernel(x_prompt, x_sample, p_prompt, p_sample, g_attn, w_in, sink_a, rel_bias, gq_b, gk_b, g_out_a, g_out_b, w_out, g_ffn, w_router, w_gate, w_up, w_down, g_ple, w_ple_gate, w_ple_proj, g_final):
    prm = _prep_params(g_attn, w_in, gq_b, gk_b, w_out, w_router, w_gate, w_up, w_down, w_ple_gate, w_ple_proj)
    args = (prm, sink_a[0], rel_bias, g_out_a[0][None], g_out_b[0][None], g_ffn[0][None], g_ple[0][None],
            g_final[None])
    y_prompt = _trunk(x_prompt, p_prompt[0], *args)
    y_sample = _trunk(x_sample, p_sample[0], *args)
    return (y_prompt, y_sample)
```

```python
import functools
import math

import jax
import jax.numpy as jnp
import numpy as np
from jax import lax
from jax.experimental import pallas as pl
from jax.experimental.pallas import tpu as pltpu

D_MODEL = 1024
HEAD_DIM = 64
N_HEADS = 8
N_KV = 2
GROUP = N_HEADS // N_KV
D_MIX = N_HEADS * HEAD_DIM
D_KV = N_KV * HEAD_DIM
D_GRP = GROUP * HEAD_DIM
WINDOW = 128
NUM_BUCKETS = 32
MAX_DISTANCE = 128
GRID_W = 64
ROPE_THETA = 10000.0
AX_PAIRS = HEAD_DIM // 4
N_EXPERTS = 16
CAPACITY_FACTOR = 2
D_EXPERT = 512
D_PLE = 256
EPS = 1e-6
NEG = -1e30
LOG2E = math.log2(math.e)
QSCALE = HEAD_DIM ** -0.5 * LOG2E

LANES = 128
ROW_SLABS = D_MODEL // LANES

F32 = jnp.float32
BF16 = jnp.bfloat16

_NT = (((1,), (1,)), ((), ()))
_TN = (((0,), (0,)), ((), ()))


VMEM_LIMIT_BYTES = 48 * 1024 * 1024


def _cparams(n_axes):
    return pltpu.CompilerParams(vmem_limit_bytes=VMEM_LIMIT_BYTES, dimension_semantics=("arbitrary",) * n_axes)


def _const_spec(shape):
    zeros = (0,) * len(shape)
    return pl.BlockSpec(shape, lambda *_: zeros)


def _rms(x, g):
    r = lax.rsqrt(jnp.mean(x * x, axis=-1, keepdims=True) + EPS)
    return (x * r) * g


def _split_bf16(x):
    hi = x.astype(BF16)
    lo = (x - hi.astype(F32)).astype(BF16)
    return hi, lo


_Q_ROWS = 2 * D_MIX
D_KX = N_KV * LANES
HALF = HEAD_DIM // 2


def _qkv_kernel(x_ref, g_ref, w1_ref, w2t_ref, bd_ref, gq_ref, gk_ref, ct_ref, st_ref, ck_ref, sk_ref,
                qat_ref, qbt_ref, vat_ref, vbt_ref, ka_ref, kb_ref):
    tn = x_ref.shape[1]
    h = _rms(x_ref[0], g_ref[...]).astype(BF16)
    p1 = jnp.dot(h, w1_ref[...], preferred_element_type=F32)
    p2 = lax.dot_general(w2t_ref[...], h, _NT, preferred_element_type=F32)

    qat_ref[0] = (p2[0:D_MIX] * QSCALE).astype(BF16)
    vat_ref[0] = p2[_Q_ROWS:_Q_ROWS + D_KV].astype(BF16)
    vbt_ref[0] = p2[_Q_ROWS + D_KV:_Q_ROWS + 2 * D_KV].astype(BF16)
    ka_ref[0] = p1[:, 0:D_KX].astype(BF16)

    q = p2[D_MIX:2 * D_MIX].reshape(N_HEADS, HEAD_DIM, tn)
    rq = lax.rsqrt(jnp.sum(q * q, axis=1, keepdims=True) * (1.0 / HEAD_DIM) + EPS)
    gq = jnp.concatenate([gq_ref[...]] * (tn // LANES), axis=1)[None]
    qn = (q * rq) * gq
    x0, x1 = qn[:, :HALF], qn[:, HALF:]
    c, s = ct_ref[...][None], st_ref[...][None]
    qbt_ref[0] = jnp.concatenate([x0 * c - x1 * s, x0 * s + x1 * c], axis=1).reshape(D_MIX, tn).astype(BF16)

    k = p1[:, D_KX:2 * D_KX]
    hi, lo = _split_bf16(k * k)
    ssq = (jnp.dot(hi, bd_ref[...], preferred_element_type=F32)
           + jnp.dot(lo, bd_ref[...], preferred_element_type=F32))
    kn = (k * lax.rsqrt(ssq * (1.0 / HEAD_DIM) + EPS)) * gk_ref[...]
    lane = lax.broadcasted_iota(jnp.int32, (tn, LANES), 1)
    outs = []
    for hk in range(N_KV):
        kh = kn[:, hk * LANES:(hk + 1) * LANES]
        partner = jnp.where(lane < HALF, pltpu.roll(kh, LANES - HALF, 1), pltpu.roll(kh, HALF, 1))
        outs.append(kh * ck_ref[...] + partner * sk_ref[...])
    kb_ref[0] = jnp.concatenate(outs, axis=1).astype(BF16)


def _qkv(x, prm, ct, st, ck, sk, *, tn):
    B, T, _ = x.shape
    tn = min(tn, T)
    grid = (B, T // tn)
    tok = lambda w: pl.BlockSpec((1, tn, w), lambda b, j: (b, j, 0))
    feat = lambda r: pl.BlockSpec((1, r, tn), lambda b, j: (b, 0, j))
    return pl.pallas_call(
        _qkv_kernel,
        grid=grid,
        in_specs=[tok(D_MODEL), _const_spec((1, D_MODEL)), _const_spec((D_MODEL, 2 * D_KX)),
                  _const_spec((_Q_ROWS + 2 * D_KV, D_MODEL)), _const_spec((D_KX, D_KX)),
                  _const_spec((HEAD_DIM, LANES)), _const_spec((1, D_KX)),
                  pl.BlockSpec((HALF, tn), lambda b, j: (0, j)),
                  pl.BlockSpec((HALF, tn), lambda b, j: (0, j)),
                  pl.BlockSpec((tn, LANES), lambda b, j: (j, 0)),
                  pl.BlockSpec((tn, LANES), lambda b, j: (j, 0))],
        out_specs=[feat(D_MIX), feat(D_MIX), feat(D_KV), feat(D_KV), tok(D_KX), tok(D_KX)],
        out_shape=[jax.ShapeDtypeStruct((B, D_MIX, T), BF16), jax.ShapeDtypeStruct((B, D_MIX, T), BF16),
                   jax.ShapeDtypeStruct((B, D_KV, T), BF16), jax.ShapeDtypeStruct((B, D_KV, T), BF16),
                   jax.ShapeDtypeStruct((B, T, D_KX), BF16), jax.ShapeDtypeStruct((B, T, D_KX), BF16)],
        compiler_params=_cparams(2),
        name="qkv",
    )(x, prm["g_attn"], prm["w1"], prm["w2t"], prm["bd"], prm["gq"], prm["gk"], ct, st, ck, sk)


ONES_ROWS = 16
V1_ROWS = HEAD_DIM + ONES_ROWS


def _head_weights(qt_ref, g, extra):
    return jnp.concatenate([qt_ref[0, g * HEAD_DIM:(g + 1) * HEAD_DIM, :], extra], axis=0)


def _normalised(acc):
    return acc[0:HEAD_DIM] * (1.0 / acc[HEAD_DIM:HEAD_DIM + 1])


def _attn_b_kernel(qt_ref, k_ref, vt_ref, o_ref, v1_ref, s0_ref, s1_ref, p0_ref, p1_ref, a0_ref, a1_ref,
                   acc_ref, m_ref, *, kc):
    s_refs, p_refs, a_refs = (s0_ref, s1_ref), (p0_ref, p1_ref), (a0_ref, a1_ref)
    T = k_ref.shape[1]
    tq = qt_ref.shape[2]
    n_chunks = T // kc

    @pl.when(pl.program_id(2) == 0)
    def _():
        v1_ref[0:HEAD_DIM, :] = vt_ref[0]
        v1_ref[HEAD_DIM:V1_ROWS, :] = jnp.ones((ONES_ROWS, T), BF16)

    acc_ref[...] = jnp.zeros(acc_ref.shape, F32)
    m_ref[...] = jnp.full(m_ref.shape, NEG, F32)
    spare = jnp.zeros((HEAD_DIM, tq), BF16)

    heads = range(GROUP)

    def scores(c, slot, hs=heads):
        kblk = k_ref[0, pl.ds(pl.multiple_of(c * kc, kc), kc), :]
        for g in hs:
            s_refs[slot][g] = jnp.dot(kblk, _head_weights(qt_ref, g, spare),
                                      preferred_element_type=F32).astype(BF16)

    def softmax(slot, hs=heads):
        for g in hs:
            s = s_refs[slot][g]
            m_old = m_ref[g]
            m_new = jnp.maximum(m_old, jnp.max(s, axis=0, keepdims=True).astype(F32))
            p_refs[slot][g] = jnp.exp2(s - m_new.astype(BF16))
            a_refs[slot][g] = jnp.exp2(m_old - m_new)
            m_ref[g] = m_new

    def values(c, slot, hs=heads):
        v1 = v1_ref[:, pl.ds(pl.multiple_of(c * kc, kc), kc)]
        for g in hs:
            acc_ref[g] = (a_refs[slot][g] * acc_ref[g]
                          + jnp.dot(v1, p_refs[slot][g], preferred_element_type=F32))

    scores(0, 0)
    softmax(0)
    scores(1, 1)

    def body(j, carry):
        c = 2 * j
        for g in heads:
            scores(c, 0, (g,))
            softmax(1, (g,))
            values(c - 2, 0, (g,))
        for g in heads:
            scores(c + 1, 1, (g,))
            softmax(0, (g,))
            values(c - 1, 1, (g,))
        return carry

    lax.fori_loop(1, n_chunks // 2, body, 0)
    values(n_chunks - 2, 0)
    softmax(1)
    values(n_chunks - 1, 1)
    o_ref[0] = jnp.concatenate([_normalised(acc_ref[g]) for g in range(GROUP)], axis=0).T


def _attn_b(qt, k, vt, *, tq, kc):
    B, _, T = qt.shape
    tq = min(tq, T)
    kc = min(kc, T // 2)
    assert T % (2 * kc) == 0
    grid = (B, N_KV, T // tq)
    return pl.pallas_call(
        functools.partial(_attn_b_kernel, kc=kc),
        grid=grid,
        in_specs=[pl.BlockSpec((1, D_GRP, tq), lambda b, h, i: (b, h, i)),
                  pl.BlockSpec((1, T, LANES), lambda b, h, i: (b, 0, h)),
                  pl.BlockSpec((1, HEAD_DIM, T), lambda b, h, i: (b, h, 0))],
        out_specs=pl.BlockSpec((1, tq, D_GRP), lambda b, h, i: (b, i, h)),
        out_shape=jax.ShapeDtypeStruct((B, T, D_MIX), F32),
        scratch_shapes=[pltpu.VMEM((V1_ROWS, T), BF16),
                        pltpu.VMEM((GROUP, kc, tq), BF16), pltpu.VMEM((GROUP, kc, tq), BF16),
                        pltpu.VMEM((GROUP, kc, tq), BF16), pltpu.VMEM((GROUP, kc, tq), BF16),
                        pltpu.VMEM((GROUP, 1, tq), F32), pltpu.VMEM((GROUP, 1, tq), F32),
                        pltpu.VMEM((GROUP, V1_ROWS, tq), F32),
                        pltpu.VMEM((GROUP, 1, tq), F32)],
        compiler_params=_cparams(3),
        name="attn_b",
    )(qt, k, vt)


HEADS_PER_DOT = 2


def _attn_a_kernel(sink_ref, rb_ref, bucket_ref, qt_ref, k_ref, vt_ref, o_ref, kp_ref, v1_ref, bias_ref):
    T = k_ref.shape[1]
    tq = qt_ref.shape[2]
    kw = tq + 2 * WINDOW
    qi = pl.program_id(1)

    @pl.when((pl.program_id(0) == 0) & (qi == 0))
    def _():
        lane = lax.broadcasted_iota(jnp.int32, (WINDOW, LANES), 1)
        off_seq = (lane == HEAD_DIM).astype(BF16)
        for kvh in range(N_KV):
            kp_ref[kvh, 0:WINDOW, :] = off_seq
            kp_ref[kvh, WINDOW + T:, :] = off_seq
            v1_ref[kvh, 0:HEAD_DIM, 0:WINDOW] = jnp.zeros((HEAD_DIM, WINDOW), BF16)
            v1_ref[kvh, 0:HEAD_DIM, WINDOW + T:] = jnp.zeros((HEAD_DIM, WINDOW), BF16)
            v1_ref[kvh, HEAD_DIM:V1_ROWS, :] = jnp.ones((ONES_ROWS, T + 2 * WINDOW), BF16)
        bucket = bucket_ref[...]
        for h in range(N_HEADS):
            bias = jnp.full((kw, tq), NEG, F32)
            for k in range(NUM_BUCKETS):
                bias = jnp.where(bucket == k + 1, rb_ref[k, h], bias)
            bias_ref[h] = bias * LOG2E

    @pl.when(qi == 0)
    def _():
        for kvh in range(N_KV):
            kp_ref[kvh, WINDOW:WINDOW + T, :] = k_ref[0, :, kvh * LANES:(kvh + 1) * LANES]
            v1_ref[kvh, 0:HEAD_DIM, WINDOW:WINDOW + T] = vt_ref[0, kvh * HEAD_DIM:(kvh + 1) * HEAD_DIM, :]

    start = pl.multiple_of(qi * tq, tq)
    spare_row = lax.broadcasted_iota(jnp.int32, (HEAD_DIM, tq), 0)
    spare = jnp.where(spare_row == 0, NEG, 0.0).astype(BF16)
    def scores(heads):
        kblk = kp_ref[heads[0] // GROUP, pl.ds(start, kw), :]
        w = jnp.concatenate([_head_weights(qt_ref, h, spare) for h in heads], axis=1)
        bias = jnp.concatenate([bias_ref[h] for h in heads], axis=1)
        return (jnp.dot(kblk, w, preferred_element_type=F32) + bias).astype(BF16)

    head_sets = [tuple(range(h, h + HEADS_PER_DOT)) for h in range(0, N_HEADS, HEADS_PER_DOT)]
    outs = []
    s_next = scores(head_sets[0])
    for i, heads in enumerate(head_sets):
        s = s_next
        if i + 1 < len(head_sets):
            s_next = scores(head_sets[i + 1])
        v1 = v1_ref[heads[0] // GROUP, :, pl.ds(start, kw)]
        sink = jnp.concatenate([jnp.full((1, tq), sink_ref[h], F32) for h in heads], axis=1) * LOG2E
        m = jnp.maximum(jnp.max(s, axis=0, keepdims=True).astype(F32), sink).astype(BF16)
        pv = jnp.dot(v1, jnp.exp2(s - m), preferred_element_type=F32)
        den = pv[HEAD_DIM:HEAD_DIM + 1] + jnp.exp2(sink - m.astype(F32))
        o = pv[0:HEAD_DIM] * (1.0 / den)
        outs.extend(o[:, j * tq:(j + 1) * tq] for j in range(len(heads)))
    o_ref[0] = jnp.concatenate(outs, axis=0).T


def _attn_a(qt, k, vt, sink, rel_bias, bucket):
    B, _, T = qt.shape
    kw, tq = bucket.shape
    grid = (B, T // tq)
    smem = pl.BlockSpec(memory_space=pltpu.SMEM)
    return pl.pallas_call(
        _attn_a_kernel,
        grid=grid,
        in_specs=[smem, smem, _const_spec((kw, tq)),
                  pl.BlockSpec((1, D_MIX, tq), lambda b, i: (b, 0, i)),
                  pl.BlockSpec((1, T, D_KX), lambda b, i: (b, 0, 0)),
                  pl.BlockSpec((1, D_KV, T), lambda b, i: (b, 0, 0))],
        out_specs=pl.BlockSpec((1, tq, D_MIX), lambda b, i: (b, i, 0)),
        out_shape=jax.ShapeDtypeStruct((B, T, D_MIX), F32),
        scratch_shapes=[pltpu.VMEM((N_KV, T + 2 * WINDOW, LANES), BF16),
                        pltpu.VMEM((N_KV, V1_ROWS, T + 2 * WINDOW), BF16),
                        pltpu.VMEM((N_HEADS, kw, tq), F32)],
        compiler_params=_cparams(2),
        name="attn_a",
    )(sink, rel_bias, bucket, qt, k, vt)


def _post_kernel(oa_ref, ob_ref, x_ref, ga_ref, gb_ref, wo_ref, gf_ref, wr_ref,
                 x1_ref, h2s_ref, lg_ref):
    tn = x_ref.shape[0]
    mixed = jnp.concatenate([_rms(oa_ref[...], ga_ref[...]), _rms(ob_ref[...], gb_ref[...])], axis=1)
    x1 = x_ref[...] + jnp.dot(mixed.astype(BF16), wo_ref[...], preferred_element_type=F32)
    x1_ref[...] = x1
    h2 = _rms(x1, gf_ref[...])
    for s in range(ROW_SLABS):
        h2s_ref[pl.ds(s, tn, stride=ROW_SLABS), :] = h2[:, s * LANES:(s + 1) * LANES]
    lg_ref[...] = lax.dot_general(wr_ref[...], h2.astype(BF16), _NT, preferred_element_type=F32)


def _post(oa, ob, x, ga, gb, wo, gf, wr, *, tn):
    N = x.shape[0]
    tn = min(tn, N)
    tok = lambda w: pl.BlockSpec((tn, w), lambda j: (j, 0))
    return pl.pallas_call(
        _post_kernel,
        grid=(N // tn,),
        in_specs=[tok(D_MIX), tok(D_MIX), tok(D_MODEL), _const_spec((1, D_MIX)), _const_spec((1, D_MIX)),
                  _const_spec((D_MODEL, D_MODEL)), _const_spec((1, D_MODEL)),
                  _const_spec((N_EXPERTS, D_MODEL))],
        out_specs=[tok(D_MODEL), pl.BlockSpec((ROW_SLABS * tn, LANES), lambda j: (j, 0)),
                   pl.BlockSpec((N_EXPERTS, tn), lambda j: (0, j))],
        out_shape=[jax.ShapeDtypeStruct((N, D_MODEL), F32),
                   jax.ShapeDtypeStruct((ROW_SLABS * N, LANES), F32),
                   jax.ShapeDtypeStruct((N_EXPERTS, N), F32)],
        compiler_params=_cparams(1),
        name="post",
    )(oa, ob, x, ga, gb, wo, gf, wr)


def _select_kernel(lg_ref, idx_ref, gate_ref, pos_ref, bex_ref, *, cap, sc):
    e = pl.program_id(0)
    nb = lg_ref.shape[1]
    lg = lg_ref[...]
    mx = jnp.max(lg, axis=0)
    den = jnp.sum(jnp.exp(lg - mx), axis=0)
    aff = jnp.exp(lg_ref[e] - mx) / den

    def bisect(_, lohi):
        lo, hi = lohi
        mid = lo + jnp.right_shift(hi - lo, 1)
        thr = lax.bitcast_convert_type(mid, F32)
        cnt = jnp.sum((aff >= thr).astype(jnp.int32))
        ok = cnt >= cap
        return jnp.where(ok, mid, lo), jnp.where(ok, hi, mid)

    lo0 = jnp.zeros((1, 1), jnp.int32)
    hi0 = jnp.full((1, 1), 0x7F800000, jnp.int32)
    lo, hi = lax.fori_loop(0, 31, bisect, (lo0, hi0))
    above = aff >= lax.bitcast_convert_type(hi, F32)
    tie = (aff >= lax.bitcast_convert_type(lo, F32)) & jnp.logical_not(above)
    need = cap - jnp.sum(above.astype(jnp.int32))

    li = lax.broadcasted_iota(jnp.int32, (LANES, LANES), 0)
    lj = lax.broadcasted_iota(jnp.int32, (LANES, LANES), 1)
    ut_incl = (li <= lj).astype(BF16)
    ut_excl = (li < lj).astype(BF16)
    bi = lax.broadcasted_iota(jnp.int32, (nb, nb), 0)
    bj = lax.broadcasted_iota(jnp.int32, (nb, nb), 1)
    lt_incl = (bj <= bi).astype(BF16)
    lt_excl = (bj < bi).astype(BF16)

    def last(x):
        return x[:, LANES - 1:LANES]

    tie_b = tie.astype(BF16)
    tie_excl = jnp.dot(tie_b, ut_excl, preferred_element_type=F32)
    tie_incl = jnp.dot(tie_b, ut_incl, preferred_element_type=F32)
    tie_before = last(jnp.dot(lt_excl, tie_incl.astype(BF16), preferred_element_type=F32))
    tie_rank = (tie_before + tie_excl).astype(jnp.int32)
    sel = above | (tie & (tie_rank < need))

    lcum = jnp.dot(sel.astype(BF16), ut_incl, preferred_element_type=F32)
    lcum_b = lcum.astype(BF16)
    bcum_incl = last(jnp.dot(lt_incl, lcum_b, preferred_element_type=F32))
    bcum_excl = bcum_incl - last(lcum)
    pos_ref[0] = jnp.where(sel, (bcum_excl + lcum).astype(jnp.int32) - 1, -1)
    bex_ref[0] = bcum_excl.astype(jnp.int32)

    a1 = aff.astype(BF16).astype(F32)
    a2 = (aff - a1).astype(BF16).astype(F32)
    excl_i = bcum_excl.astype(jnp.int32)
    blk_i = lax.broadcasted_iota(jnp.int32, (nb, 1), 0)
    digits = [jnp.right_shift(excl_i, 7), excl_i & (LANES - 1), jnp.right_shift(blk_i, 7), blk_i & (LANES - 1)]
    side = jnp.concatenate([d.astype(F32) for d in digits] + [jnp.zeros((nb, LANES - len(digits)), F32)], axis=1)
    tables_t = jnp.concatenate([lcum, side, a1, a2], axis=1).T.astype(BF16)
    row = lax.broadcasted_iota(jnp.int32, (LANES, sc), 0)
    for c in range(cap // sc):
        slot = (c * sc + lax.broadcasted_iota(jnp.int32, (1, sc), 1)).astype(F32)
        hit = ((bcum_excl <= slot) & (slot < bcum_incl)).astype(BF16)
        got = jnp.dot(tables_t, hit, preferred_element_type=F32)
        lc_t = got[0:LANES]
        base = got[LANES:LANES + 1] * LANES + got[LANES + 1:LANES + 2]
        blk_of = (got[LANES + 2:LANES + 3] * LANES + got[LANES + 3:LANES + 4]).astype(jnp.int32)
        j_of = jnp.sum((lc_t <= slot - base).astype(jnp.int32), axis=0, keepdims=True)
        idx_ref[0, :, c * sc:(c + 1) * sc] = blk_of * LANES + j_of
        aff_t = got[2 * LANES:3 * LANES] + got[3 * LANES:4 * LANES]
        gate_ref[0, :, c * sc:(c + 1) * sc] = jnp.sum(jnp.where(row == j_of, aff_t, 0.0), axis=0, keepdims=True)


def _select(lg3, *, cap, sc):
    E, nb, _ = lg3.shape
    sc = min(sc, cap)
    per_e = lambda r, c, dt: (pl.BlockSpec((1, r, c), lambda e: (e, 0, 0)), jax.ShapeDtypeStruct((E, r, c), dt))
    specs = [per_e(1, cap, jnp.int32), per_e(1, cap, F32), per_e(nb, LANES, jnp.int32), per_e(nb, 1, jnp.int32)]
    return pl.pallas_call(
        functools.partial(_select_kernel, cap=cap, sc=sc),
        grid=(E,),
        in_specs=[_const_spec((E, nb, LANES))],
        out_specs=[s for s, _ in specs],
        out_shape=[o for _, o in specs],
        compiler_params=_cparams(1),
        name="select",
    )(lg3)


GATHER_UNROLL = 8


def _ffn_kernel(idx_ref, idxn_ref, gate_ref, h2s_ref, wg_ref, wu_ref, wd_ref, ye_ref, xbuf_ref, sem_ref):
    rows = ye_ref.shape[0]
    n = pl.program_id(0)
    n_steps = pl.num_programs(0)
    slot = n % 2

    def row_copy(ids_ref, r, to_slot):
        src = pl.multiple_of(ids_ref[0, 0, r] * ROW_SLABS, ROW_SLABS)
        return pltpu.make_async_copy(h2s_ref.at[pl.ds(src, ROW_SLABS), :],
                                     xbuf_ref.at[to_slot, pl.ds(r * ROW_SLABS, ROW_SLABS), :],
                                     sem_ref.at[to_slot])

    def start_gather(ids_ref, to_slot):
        def body(r0, carry):
            for u in range(GATHER_UNROLL):
                row_copy(ids_ref, r0 * GATHER_UNROLL + u, to_slot).start(priority=u % 2)
            return carry
        lax.fori_loop(0, rows // GATHER_UNROLL, body, 0)

    def wait_slot(s):
        pltpu.make_async_copy(h2s_ref.at[pl.ds(0, rows * ROW_SLABS), :], xbuf_ref.at[s], sem_ref.at[s]).wait()

    @pl.when(n == 0)
    def _():
        start_gather(idx_ref, 0)

    @pl.when(n + 1 < n_steps)
    def _():
        for r in range(rows):
            row_copy(idxn_ref, r, 1 - slot).start(priority=r % 2)

    wait_slot(slot)
    x = jnp.concatenate(
        [xbuf_ref[slot, pl.ds(s, rows, stride=ROW_SLABS), :].astype(BF16) for s in range(ROW_SLABS)], axis=1)
    a = jnp.dot(x, wg_ref[0], preferred_element_type=F32)
    u = jnp.dot(x, wu_ref[0], preferred_element_type=F32)
    hmid = (jax.nn.silu(a) * u).astype(BF16)
    y = jnp.dot(hmid, wd_ref[0], preferred_element_type=F32)
    ri = lax.broadcasted_iota(jnp.int32, (rows, rows), 0)
    ci = lax.broadcasted_iota(jnp.int32, (rows, rows), 1)
    gate_col = jnp.sum(jnp.where(ri == ci, gate_ref[0], 0.0), axis=1, keepdims=True)
    ye_ref[...] = (gate_col * y).astype(BF16)


def _ffn(idx3, gate3, h2s, wg, wu, wd, *, cap):
    n_steps, _, rows = idx3.shape
    per_e = cap // rows
    last = n_steps - 1
    return pl.pallas_call(
        _ffn_kernel,
        grid=(n_steps,),
        in_specs=[pl.BlockSpec((1, 1, rows), lambda n: (n, 0, 0), memory_space=pltpu.SMEM),
                  pl.BlockSpec((1, 1, rows), lambda n: (jnp.minimum(n + 1, last), 0, 0), memory_space=pltpu.SMEM),
                  pl.BlockSpec((1, 1, rows), lambda n: (n, 0, 0)),
                  pl.BlockSpec(memory_space=pl.ANY),
                  pl.BlockSpec((1, D_MODEL, D_EXPERT), lambda n: (n // per_e, 0, 0)),
                  pl.BlockSpec((1, D_MODEL, D_EXPERT), lambda n: (n // per_e, 0, 0)),
                  pl.BlockSpec((1, D_EXPERT, D_MODEL), lambda n: (n // per_e, 0, 0))],
        out_specs=pl.BlockSpec((rows, D_MODEL), lambda n: (n, 0)),
        out_shape=jax.ShapeDtypeStruct((n_steps * rows, D_MODEL), BF16),
        scratch_shapes=[pltpu.VMEM((2, rows * ROW_SLABS, LANES), F32), pltpu.SemaphoreType.DMA((2,))],
        compiler_params=_cparams(1),
        name="ffn",
    )(idx3, idx3, gate3, h2s, wg, wu, wd)


BF16_ROWS = 16
MXU_DEPTH = 256


def _combine_kernel(tab_ref, x1_ref, pos_ref, p_ref, ye_ref, wpg_ref, wpp_ref, gp_ref, gfin_ref,
                    y_ref, win_ref, sem_ref, more_ref, msem_ref, acc_ref, *, cap, win):
    tt = x1_ref.shape[0]
    i = pl.program_id(0)
    n_tiles = pl.num_programs(0)
    slot = i % 2
    total = N_EXPERTS * cap
    per_dot = MXU_DEPTH // win
    wi = lax.broadcasted_iota(jnp.int32, (win, tt), 0)

    def first_row(e, tile):
        first = e * cap + tab_ref[e, tile]
        aligned = lax.shift_left(lax.shift_right_logical(first, 4), 4)
        return jnp.minimum(aligned, total - win)

    def fetch_windows(tile, to_slot):
        for e in range(N_EXPERTS):
            a0 = pl.multiple_of(first_row(e, tile), BF16_ROWS)
            pltpu.make_async_copy(ye_ref.at[pl.ds(a0, win), :], win_ref.at[to_slot, pl.ds(e * win, win), :],
                                  sem_ref.at[to_slot]).start()

    @pl.when(i == 0)
    def _():
        fetch_windows(0, 0)

    @pl.when(i + 1 < n_tiles)
    def _():
        fetch_windows(i + 1, 1 - slot)

    pltpu.make_async_copy(ye_ref.at[pl.ds(0, N_EXPERTS * win), :], win_ref.at[slot], sem_ref.at[slot]).wait()

    def onehot(e, a0):
        pos = pos_ref[pl.ds(e, 1), :]
        return (((pos + (e * cap - a0)) == wi) & (pos >= 0)).astype(BF16)

    def rows_from(e, a0):
        return e * cap + tab_ref[e, i + 1] - a0

    acc = None
    for j in range(N_EXPERTS // per_dot):
        es = range(j * per_dot, (j + 1) * per_dot)
        hot = jnp.concatenate([onehot(e, first_row(e, i)) for e in es], axis=0)
        part = lax.dot_general(hot, win_ref[slot, j * MXU_DEPTH:(j + 1) * MXU_DEPTH, :], _TN,
                               preferred_element_type=F32)
        acc = part if acc is None else acc + part
    acc_ref[...] = acc

    longest = rows_from(0, first_row(0, i))
    for e in range(1, N_EXPERTS):
        longest = jnp.maximum(longest, rows_from(e, first_row(e, i)))

    @pl.when(longest > win)
    def _():
        def per_expert(e, carry):
            a0 = first_row(e, i)
            n_windows = lax.shift_right_logical(rows_from(e, a0) + (win - 1), win.bit_length() - 1)

            def more(c, carry):
                a = pl.multiple_of(jnp.minimum(a0 + c * win, total - win), BF16_ROWS)
                copy = pltpu.make_async_copy(ye_ref.at[pl.ds(a, win), :], more_ref, msem_ref.at[0])
                copy.start()
                copy.wait()
                acc_ref[...] += lax.dot_general(onehot(e, a), more_ref[...], _TN, preferred_element_type=F32)
                return carry
            return lax.fori_loop(1, n_windows, more, carry)
        lax.fori_loop(0, N_EXPERTS, per_expert, 0)

    x2 = x1_ref[...] + acc_ref[...]
    gate = jax.nn.sigmoid(jnp.dot(_rms(x2, gp_ref[...]).astype(BF16), wpg_ref[...], preferred_element_type=F32))
    x3 = x2 + gate * jnp.dot(p_ref[...].astype(BF16), wpp_ref[...], preferred_element_type=F32)
    y_ref[...] = _rms(x3, gfin_ref[...])


def _combine(tab, x1, pos, p, ye, wpg, wpp, gp, gfin, *, cap, tt, win):
    N = x1.shape[0]
    grid_spec = pltpu.PrefetchScalarGridSpec(
        num_scalar_prefetch=1,
        grid=(N // tt,),
        in_specs=[pl.BlockSpec((tt, D_MODEL), lambda i, t: (i, 0)),
                  pl.BlockSpec((N_EXPERTS, tt), lambda i, t: (0, i)),
                  pl.BlockSpec((tt, D_PLE), lambda i, t: (i, 0)),
                  pl.BlockSpec(memory_space=pl.ANY),
                  pl.BlockSpec((D_MODEL, D_MODEL), lambda i, t: (0, 0)),
                  pl.BlockSpec((D_PLE, D_MODEL), lambda i, t: (0, 0)),
                  pl.BlockSpec((1, D_MODEL), lambda i, t: (0, 0)),
                  pl.BlockSpec((1, D_MODEL), lambda i, t: (0, 0))],
        out_specs=pl.BlockSpec((tt, D_MODEL), lambda i, t: (i, 0)),
        scratch_shapes=[pltpu.VMEM((2, N_EXPERTS * win, D_MODEL), BF16), pltpu.SemaphoreType.DMA((2,)),
                        pltpu.VMEM((win, D_MODEL), BF16), pltpu.SemaphoreType.DMA((1,)),
                        pltpu.VMEM((tt, D_MODEL), F32)],
    )
    return pl.pallas_call(
        functools.partial(_combine_kernel, cap=cap, win=win),
        grid_spec=grid_spec,
        out_shape=jax.ShapeDtypeStruct((N, D_MODEL), F32),
        compiler_params=_cparams(1),
        name="combine",
    )(tab, x1, pos, p, ye, wpg, wpp, gp, gfin)


def _t5_bucket(rel):
    nb = NUM_BUCKETS // 2
    ret = jnp.where(rel > 0, nb, 0)
    n = jnp.abs(rel)
    max_exact = nb // 2
    large = max_exact + (jnp.log(jnp.maximum(n, 1).astype(F32) / max_exact)
                         / math.log(MAX_DISTANCE / max_exact) * (nb - max_exact)).astype(jnp.int32)
    large = jnp.minimum(large, nb - 1)
    return ret + jnp.where(n < max_exact, n, large)


def _bucket_table(tq):
    kw = tq + 2 * WINDOW
    rel = jnp.arange(kw)[:, None] - WINDOW - jnp.arange(tq)[None, :]
    code = jnp.where(jnp.abs(rel) <= WINDOW, _t5_bucket(rel) + 1, 0).astype(jnp.int32)
    return code & (2 * NUM_BUCKETS - 1)


def _rope_tables(T):
    rows = T // GRID_W
    row = jnp.repeat(jnp.arange(rows), GRID_W).astype(F32)
    col = jnp.tile(jnp.arange(GRID_W), rows).astype(F32)
    freqs = ROPE_THETA ** (-jnp.arange(AX_PAIRS, dtype=F32) / AX_PAIRS)
    ang = jnp.concatenate([row[:, None] * freqs, col[:, None] * freqs], axis=-1)
    cos, sin = jnp.cos(ang), jnp.sin(ang)
    spare = jnp.zeros((T, LANES - HEAD_DIM), F32)
    return cos.T, sin.T, jnp.concatenate([cos, cos, spare], axis=1), jnp.concatenate([-sin, sin, spare], axis=1)


def _prep_params(g_attn, w_in, gq_b, gk_b, w_out, w_router, w_gate, w_up, w_down, w_ple_gate, w_ple_proj):
    w = w_in[0]
    o = np.cumsum([0, D_MIX, D_KV, D_KV, D_MIX, D_KV, D_KV])
    wqa, wka, wva, wqb, wkb, wvb = (w[:, o[i]:o[i + 1]] for i in range(6))
    halves = jnp.concatenate([jnp.arange(0, HEAD_DIM, 2), jnp.arange(1, HEAD_DIM, 2)])
    perm_q = (jnp.arange(N_HEADS)[:, None] * HEAD_DIM + halves[None, :]).reshape(-1)
    perm_k = (jnp.arange(N_KV)[:, None] * HEAD_DIM + halves[None, :]).reshape(-1)

    def spread(a):
        z = jnp.zeros(a.shape[:-1] + (LANES - HEAD_DIM,), a.dtype)
        return jnp.concatenate([a[..., :HEAD_DIM], z, a[..., HEAD_DIM:], z], axis=-1)

    w1 = jnp.concatenate([spread(wka), spread(wkb[:, perm_k])], axis=1).astype(BF16)
    w2t = jnp.concatenate([wqa.T, wqb[:, perm_q].T, wva.T, wvb.T], axis=0).astype(BF16)
    hd = jnp.arange(D_KX) // HEAD_DIM
    bd = (hd[:, None] == hd[None, :]).astype(BF16)
    gq = (gq_b[0] * QSCALE)[halves]
    gk = gk_b[0][halves]
    return dict(
        g_attn=g_attn[0][None], w1=w1, w2t=w2t, bd=bd,
        gq=jnp.broadcast_to(gq[:, None], (HEAD_DIM, LANES)),
        gk=spread(jnp.tile(gk, N_KV))[None],
        wo=w_out[0].astype(BF16), wr=w_router[0].T.astype(BF16),
        wg=w_gate[0].astype(BF16), wu=w_up[0].astype(BF16), wd=w_down[0].astype(BF16),
        wpg=w_ple_gate[0].astype(BF16), wpp=w_ple_proj[0].astype(BF16))


TOKEN_TILE = 512
WINDOW_Q_TILE = 256
DENSE_Q_TILE = 512
DENSE_KEY_CHUNK = 512
SELECT_SLOT_CHUNK = 1024
FFN_ROWS = 512
COMBINE_TILE = 256
COMBINE_WINDOW = 128


def _trunk(x, p, prm, sink, rel_bias, g_out_a, g_out_b, g_ffn, g_ple, g_final):
    B, T, _ = x.shape
    N = B * T
    cap = CAPACITY_FACTOR * N // N_EXPERTS
    qat, qbt, vat, vbt, ka, kb = _qkv(x, prm, *_rope_tables(T), tn=TOKEN_TILE)
    oa = _attn_a(qat, ka, vat, sink, rel_bias, _bucket_table(min(WINDOW_Q_TILE, T)))
    ob = _attn_b(qbt, kb, vbt, tq=DENSE_Q_TILE, kc=DENSE_KEY_CHUNK)
    x1, h2s, lg = _post(oa.reshape(N, D_MIX), ob.reshape(N, D_MIX), x.reshape(N, D_MODEL),
                        g_out_a, g_out_b, prm["wo"], g_ffn, prm["wr"], tn=TOKEN_TILE)
    lg3 = lg.reshape(N_EXPERTS, N // LANES, LANES)
    idx, gate, pos, bex = _select(lg3, cap=cap, sc=SELECT_SLOT_CHUNK)
    rows = min(FFN_ROWS, cap)
    ye = _ffn(idx.reshape(-1, 1, rows), gate.reshape(-1, 1, rows), h2s, prm["wg"], prm["wu"], prm["wd"], cap=cap)
    tt = min(COMBINE_TILE, N)
    win = min(COMBINE_WINDOW, cap)
    tile_start = bex.reshape(N_EXPERTS, N // LANES)[:, ::tt // LANES]
    tab = jnp.concatenate([tile_start, jnp.full((N_EXPERTS, 1), cap, jnp.int32)], axis=1)
    y = _combine(tab, x1, pos.reshape(N_EXPERTS, N), p.reshape(N, D_PLE), ye, prm["wpg"], prm["wpp"],
                 g_ple, g_final, cap=cap, tt=tt, win=win)
    return y.reshape(B, T, D_MODEL)


def kernel(x_prompt, x_sample, p_prompt, p_sample, g_attn, w_in, sink_a, rel_bias, gq_b, gk_b, g_out_a, g_out_b, w_out, g_ffn, w_router, w_gate, w_up, w_down, g_ple, w_ple_gate, w_ple_proj, g_final):
    prm = _prep_params(g_attn, w_in, gq_b, gk_b, w_out, w_router, w_gate, w_up, w_down, w_ple_gate, w_ple_proj)
    args = (prm, sink_a[0], rel_bias, g_out_a[0][None], g_out_b[0][None], g_ffn[0][None], g_ple[0][None],
            g_final[None])
    y_prompt = _trunk(x_prompt, p_prompt[0], *args)
    y_sample = _trunk(x_sample, p_sample[0], *args)
    return (y_prompt, y_sample)
```

```python
import functools
import math

import jax
import jax.numpy as jnp
import numpy as np
from jax import lax
from jax.experimental import pallas as pl
from jax.experimental.pallas import tpu as pltpu

D_MODEL = 1024
HEAD_DIM = 64
N_HEADS = 8
N_KV = 2
GROUP = N_HEADS // N_KV
D_MIX = N_HEADS * HEAD_DIM
D_KV = N_KV * HEAD_DIM
D_GRP = GROUP * HEAD_DIM
WINDOW = 128
NUM_BUCKETS = 32
MAX_DISTANCE = 128
GRID_W = 64
ROPE_THETA = 10000.0
AX_PAIRS = HEAD_DIM // 4
N_EXPERTS = 16
CAPACITY_FACTOR = 2
D_EXPERT = 512
D_PLE = 256
EPS = 1e-6
NEG = -1e30
LOG2E = math.log2(math.e)
QSCALE = HEAD_DIM ** -0.5 * LOG2E

LANES = 128
ROW_SLABS = D_MODEL // LANES

F32 = jnp.float32
BF16 = jnp.bfloat16

_NT = (((1,), (1,)), ((), ()))
_TN = (((0,), (0,)), ((), ()))


VMEM_LIMIT_BYTES = 48 * 1024 * 1024


def _cparams(n_axes):
    return pltpu.CompilerParams(vmem_limit_bytes=VMEM_LIMIT_BYTES, dimension_semantics=("arbitrary",) * n_axes)


def _const_spec(shape):
    zeros = (0,) * len(shape)
    return pl.BlockSpec(shape, lambda *_: zeros)


def _rms(x, g):
    r = lax.rsqrt(jnp.mean(x * x, axis=-1, keepdims=True) + EPS)
    return (x * r) * g


def _split_bf16(x):
    hi = x.astype(BF16)
    lo = (x - hi.astype(F32)).astype(BF16)
    return hi, lo


_Q_ROWS = 2 * D_MIX
D_KX = N_KV * LANES
HALF = HEAD_DIM // 2


def _qkv_kernel(x_ref, g_ref, w1_ref, w2t_ref, bd_ref, gq_ref, gk_ref, ct_ref, st_ref, ck_ref, sk_ref,
                qat_ref, qbt_ref, vat_ref, vbt_ref, ka_ref, kb_ref):
    tn = x_ref.shape[1]
    h = _rms(x_ref[0], g_ref[...]).astype(BF16)
    p1 = jnp.dot(h, w1_ref[...], preferred_element_type=F32)
    p2 = lax.dot_general(w2t_ref[...], h, _NT, preferred_element_type=F32)

    qat_ref[0] = (p2[0:D_MIX] * QSCALE).astype(BF16)
    vat_ref[0] = p2[_Q_ROWS:_Q_ROWS + D_KV].astype(BF16)
    vbt_ref[0] = p2[_Q_ROWS + D_KV:_Q_ROWS + 2 * D_KV].astype(BF16)
    ka_ref[0] = p1[:, 0:D_KX].astype(BF16)

    q = p2[D_MIX:2 * D_MIX].reshape(N_HEADS, HEAD_DIM, tn)
    rq = lax.rsqrt(jnp.sum(q * q, axis=1, keepdims=True) * (1.0 / HEAD_DIM) + EPS)
    gq = jnp.concatenate([gq_ref[...]] * (tn // LANES), axis=1)[None]
    qn = (q * rq) * gq
    x0, x1 = qn[:, :HALF], qn[:, HALF:]
    c, s = ct_ref[...][None], st_ref[...][None]
    qbt_ref[0] = jnp.concatenate([x0 * c - x1 * s, x0 * s + x1 * c], axis=1).reshape(D_MIX, tn).astype(BF16)

    k = p1[:, D_KX:2 * D_KX]
    hi, lo = _split_bf16(k * k)
    ssq = (jnp.dot(hi, bd_ref[...], preferred_element_type=F32)
           + jnp.dot(lo, bd_ref[...], preferred_element_type=F32))
    kn = (k * lax.rsqrt(ssq * (1.0 / HEAD_DIM) + EPS)) * gk_ref[...]
    lane = lax.broadcasted_iota(jnp.int32, (tn, LANES), 1)
    outs = []
    for hk in range(N_KV):
        kh = kn[:, hk * LANES:(hk + 1) * LANES]
        partner = jnp.where(lane < HALF, pltpu.roll(kh, LANES - HALF, 1), pltpu.roll(kh, HALF, 1))
        outs.append(kh * ck_ref[...] + partner * sk_ref[...])
    kb_ref[0] = jnp.concatenate(outs, axis=1).astype(BF16)


def _qkv(x, prm, ct, st, ck, sk, *, tn):
    B, T, _ = x.shape
    tn = min(tn, T)
    grid = (B, T // tn)
    tok = lambda w: pl.BlockSpec((1, tn, w), lambda b, j: (b, j, 0))
    feat = lambda r: pl.BlockSpec((1, r, tn), lambda b, j: (b, 0, j))
    return pl.pallas_call(
        _qkv_kernel,
        grid=grid,
        in_specs=[tok(D_MODEL), _const_spec((1, D_MODEL)), _const_spec((D_MODEL, 2 * D_KX)),
                  _const_spec((_Q_ROWS + 2 * D_KV, D_MODEL)), _const_spec((D_KX, D_KX)),
                  _const_spec((HEAD_DIM, LANES)), _const_spec((1, D_KX)),
                  pl.BlockSpec((HALF, tn), lambda b, j: (0, j)),
                  pl.BlockSpec((HALF, tn), lambda b, j: (0, j)),
                  pl.BlockSpec((tn, LANES), lambda b, j: (j, 0)),
                  pl.BlockSpec((tn, LANES), lambda b, j: (j, 0))],
        out_specs=[feat(D_MIX), feat(D_MIX), feat(D_KV), feat(D_KV), tok(D_KX), tok(D_KX)],
        out_shape=[jax.ShapeDtypeStruct((B, D_MIX, T), BF16), jax.ShapeDtypeStruct((B, D_MIX, T), BF16),
                   jax.ShapeDtypeStruct((B, D_KV, T), BF16), jax.ShapeDtypeStruct((B, D_KV, T), BF16),
                   jax.ShapeDtypeStruct((B, T, D_KX), BF16), jax.ShapeDtypeStruct((B, T, D_KX), BF16)],
        compiler_params=_cparams(2),
        name="qkv",
    )(x, prm["g_attn"], prm["w1"], prm["w2t"], prm["bd"], prm["gq"], prm["gk"], ct, st, ck, sk)


ONES_ROWS = 16
V1_ROWS = HEAD_DIM + ONES_ROWS


def _head_weights(qt_ref, g, extra):
    return jnp.concatenate([qt_ref[0, g * HEAD_DIM:(g + 1) * HEAD_DIM, :], extra], axis=0)


def _normalised(acc):
    return acc[0:HEAD_DIM] * (1.0 / acc[HEAD_DIM:HEAD_DIM + 1])


def _attn_b_kernel(qt_ref, k_ref, vt_ref, o_ref, v1_ref, s0_ref, s1_ref, p0_ref, p1_ref, a0_ref, a1_ref,
                   acc_ref, m_ref, *, kc):
    s_refs, p_refs, a_refs = (s0_ref, s1_ref), (p0_ref, p1_ref), (a0_ref, a1_ref)
    T = k_ref.shape[1]
    tq = qt_ref.shape[2]
    n_chunks = T // kc

    @pl.when(pl.program_id(2) == 0)
    def _():
        v1_ref[0:HEAD_DIM, :] = vt_ref[0]
        v1_ref[HEAD_DIM:V1_ROWS, :] = jnp.ones((ONES_ROWS, T), BF16)

    acc_ref[...] = jnp.zeros(acc_ref.shape, F32)
    m_ref[...] = jnp.full(m_ref.shape, NEG, F32)
    spare = jnp.zeros((HEAD_DIM, tq), BF16)

    heads = range(GROUP)

    def scores(c, slot, hs=heads):
        kblk = k_ref[0, pl.ds(pl.multiple_of(c * kc, kc), kc), :]
        for g in hs:
            s_refs[slot][g] = jnp.dot(kblk, _head_weights(qt_ref, g, spare),
                                      preferred_element_type=F32).astype(BF16)

    def softmax(slot, hs=heads):
        for g in hs:
            s = s_refs[slot][g]
            m_old = m_ref[g]
            m_new = jnp.maximum(m_old, jnp.max(s, axis=0, keepdims=True).astype(F32))
            p_refs[slot][g] = jnp.exp2(s - m_new.astype(BF16))
            a_refs[slot][g] = jnp.exp2(m_old - m_new)
            m_ref[g] = m_new

    def values(c, slot, hs=heads):
        v1 = v1_ref[:, pl.ds(pl.multiple_of(c * kc, kc), kc)]
        for g in hs:
            acc_ref[g] = (a_refs[slot][g] * acc_ref[g]
                          + jnp.dot(v1, p_refs[slot][g], preferred_element_type=F32))

    scores(0, 0)
    softmax(0)
    scores(1, 1)

    def body(j, carry):
        c = 2 * j
        for g in heads:
            scores(c, 0, (g,))
            softmax(1, (g,))
            values(c - 2, 0, (g,))
        for g in heads:
            scores(c + 1, 1, (g,))
            softmax(0, (g,))
            values(c - 1, 1, (g,))
        return carry

    lax.fori_loop(1, n_chunks // 2, body, 0)
    values(n_chunks - 2, 0)
    softmax(1)
    values(n_chunks - 1, 1)
    o_ref[0] = jnp.concatenate([_normalised(acc_ref[g]) for g in range(GROUP)], axis=0).T


def _attn_b(qt, k, vt, *, tq, kc):
    B, _, T = qt.shape
    tq = min(tq, T)
    kc = min(kc, T // 2)
    assert T % (2 * kc) == 0
    grid = (B, N_KV, T // tq)
    return pl.pallas_call(
        functools.partial(_attn_b_kernel, kc=kc),
        grid=grid,
        in_specs=[pl.BlockSpec((1, D_GRP, tq), lambda b, h, i: (b, h, i)),
                  pl.BlockSpec((1, T, LANES), lambda b, h, i: (b, 0, h)),
                  pl.BlockSpec((1, HEAD_DIM, T), lambda b, h, i: (b, h, 0))],
        out_specs=pl.BlockSpec((1, tq, D_GRP), lambda b, h, i: (b, i, h)),
        out_shape=jax.ShapeDtypeStruct((B, T, D_MIX), F32),
        scratch_shapes=[pltpu.VMEM((V1_ROWS, T), BF16),
                        pltpu.VMEM((GROUP, kc, tq), BF16), pltpu.VMEM((GROUP, kc, tq), BF16),
                        pltpu.VMEM((GROUP, kc, tq), BF16), pltpu.VMEM((GROUP, kc, tq), BF16),
                        pltpu.VMEM((GROUP, 1, tq), F32), pltpu.VMEM((GROUP, 1, tq), F32),
                        pltpu.VMEM((GROUP, V1_ROWS, tq), F32),
                        pltpu.VMEM((GROUP, 1, tq), F32)],
        compiler_params=_cparams(3),
        name="attn_b",
    )(qt, k, vt)


HEADS_PER_DOT = 4


def _attn_a_kernel(sink_ref, rb_ref, bucket_ref, qt_ref, k_ref, vt_ref, o_ref, kp_ref, v1_ref, bias_ref):
    T = k_ref.shape[1]
    tq = qt_ref.shape[2]
    kw = tq + 2 * WINDOW
    qi = pl.program_id(1)

    @pl.when((pl.program_id(0) == 0) & (qi == 0))
    def _():
        lane = lax.broadcasted_iota(jnp.int32, (WINDOW, LANES), 1)
        off_seq = (lane == HEAD_DIM).astype(BF16)
        for kvh in range(N_KV):
            kp_ref[kvh, 0:WINDOW, :] = off_seq
            kp_ref[kvh, WINDOW + T:, :] = off_seq
            v1_ref[kvh, 0:HEAD_DIM, 0:WINDOW] = jnp.zeros((HEAD_DIM, WINDOW), BF16)
            v1_ref[kvh, 0:HEAD_DIM, WINDOW + T:] = jnp.zeros((HEAD_DIM, WINDOW), BF16)
            v1_ref[kvh, HEAD_DIM:V1_ROWS, :] = jnp.ones((ONES_ROWS, T + 2 * WINDOW), BF16)
        bucket = bucket_ref[...]
        for h in range(N_HEADS):
            bias = jnp.full((kw, tq), NEG, F32)
            for k in range(NUM_BUCKETS):
                bias = jnp.where(bucket == k + 1, rb_ref[k, h], bias)
            bias_ref[h] = bias * LOG2E

    @pl.when(qi == 0)
    def _():
        for kvh in range(N_KV):
            kp_ref[kvh, WINDOW:WINDOW + T, :] = k_ref[0, :, kvh * LANES:(kvh + 1) * LANES]
            v1_ref[kvh, 0:HEAD_DIM, WINDOW:WINDOW + T] = vt_ref[0, kvh * HEAD_DIM:(kvh + 1) * HEAD_DIM, :]

    start = pl.multiple_of(qi * tq, tq)
    spare_row = lax.broadcasted_iota(jnp.int32, (HEAD_DIM, tq), 0)
    spare = jnp.where(spare_row == 0, NEG, 0.0).astype(BF16)
    def scores(heads):
        kblk = kp_ref[heads[0] // GROUP, pl.ds(start, kw), :]
        w = jnp.concatenate([_head_weights(qt_ref, h, spare) for h in heads], axis=1)
        bias = jnp.concatenate([bias_ref[h] for h in heads], axis=1)
        return (jnp.dot(kblk, w, preferred_element_type=F32) + bias).astype(BF16)

    head_sets = [tuple(range(h, h + HEADS_PER_DOT)) for h in range(0, N_HEADS, HEADS_PER_DOT)]
    outs = []
    s_next = scores(head_sets[0])
    for i, heads in enumerate(head_sets):
        s = s_next
        if i + 1 < len(head_sets):
            s_next = scores(head_sets[i + 1])
        v1 = v1_ref[heads[0] // GROUP, :, pl.ds(start, kw)]
        sink = jnp.concatenate([jnp.full((1, tq), sink_ref[h], F32) for h in heads], axis=1) * LOG2E
        m = jnp.maximum(jnp.max(s, axis=0, keepdims=True).astype(F32), sink).astype(BF16)
        pv = jnp.dot(v1, jnp.exp2(s - m), preferred_element_type=F32)
        den = pv[HEAD_DIM:HEAD_DIM + 1] + jnp.exp2(sink - m.astype(F32))
        o = pv[0:HEAD_DIM] * (1.0 / den)
        outs.extend(o[:, j * tq:(j + 1) * tq] for j in range(len(heads)))
    o_ref[0] = jnp.concatenate(outs, axis=0).T


def _attn_a(qt, k, vt, sink, rel_bias, bucket):
    B, _, T = qt.shape
    kw, tq = bucket.shape
    grid = (B, T // tq)
    smem = pl.BlockSpec(memory_space=pltpu.SMEM)
    return pl.pallas_call(
        _attn_a_kernel,
        grid=grid,
        in_specs=[smem, smem, _const_spec((kw, tq)),
                  pl.BlockSpec((1, D_MIX, tq), lambda b, i: (b, 0, i)),
                  pl.BlockSpec((1, T, D_KX), lambda b, i: (b, 0, 0)),
                  pl.BlockSpec((1, D_KV, T), lambda b, i: (b, 0, 0))],
        out_specs=pl.BlockSpec((1, tq, D_MIX), lambda b, i: (b, i, 0)),
        out_shape=jax.ShapeDtypeStruct((B, T, D_MIX), F32),
        scratch_shapes=[pltpu.VMEM((N_KV, T + 2 * WINDOW, LANES), BF16),
                        pltpu.VMEM((N_KV, V1_ROWS, T + 2 * WINDOW), BF16),
                        pltpu.VMEM((N_HEADS, kw, tq), F32)],
        compiler_params=_cparams(2),
        name="attn_a",
    )(sink, rel_bias, bucket, qt, k, vt)


def _post_kernel(oa_ref, ob_ref, x_ref, ga_ref, gb_ref, wo_ref, gf_ref, wr_ref,
                 x1_ref, h2s_ref, lg_ref):
    tn = x_ref.shape[0]
    mixed = jnp.concatenate([_rms(oa_ref[...], ga_ref[...]), _rms(ob_ref[...], gb_ref[...])], axis=1)
    x1 = x_ref[...] + jnp.dot(mixed.astype(BF16), wo_ref[...], preferred_element_type=F32)
    x1_ref[...] = x1
    h2 = _rms(x1, gf_ref[...])
    for s in range(ROW_SLABS):
        h2s_ref[pl.ds(s, tn, stride=ROW_SLABS), :] = h2[:, s * LANES:(s + 1) * LANES]
    lg_ref[...] = lax.dot_general(wr_ref[...], h2.astype(BF16), _NT, preferred_element_type=F32)


def _post(oa, ob, x, ga, gb, wo, gf, wr, *, tn):
    N = x.shape[0]
    tn = min(tn, N)
    tok = lambda w: pl.BlockSpec((tn, w), lambda j: (j, 0))
    return pl.pallas_call(
        _post_kernel,
        grid=(N // tn,),
        in_specs=[tok(D_MIX), tok(D_MIX), tok(D_MODEL), _const_spec((1, D_MIX)), _const_spec((1, D_MIX)),
                  _const_spec((D_MODEL, D_MODEL)), _const_spec((1, D_MODEL)),
                  _const_spec((N_EXPERTS, D_MODEL))],
        out_specs=[tok(D_MODEL), pl.BlockSpec((ROW_SLABS * tn, LANES), lambda j: (j, 0)),
                   pl.BlockSpec((N_EXPERTS, tn), lambda j: (0, j))],
        out_shape=[jax.ShapeDtypeStruct((N, D_MODEL), F32),
                   jax.ShapeDtypeStruct((ROW_SLABS * N, LANES), F32),
                   jax.ShapeDtypeStruct((N_EXPERTS, N), F32)],
        compiler_params=_cparams(1),
        name="post",
    )(oa, ob, x, ga, gb, wo, gf, wr)


def _select_kernel(lg_ref, idx_ref, gate_ref, pos_ref, bex_ref, *, cap, sc):
    e = pl.program_id(0)
    nb = lg_ref.shape[1]
    lg = lg_ref[...]
    mx = jnp.max(lg, axis=0)
    den = jnp.sum(jnp.exp(lg - mx), axis=0)
    aff = jnp.exp(lg_ref[e] - mx) / den

    def bisect(_, lohi):
        lo, hi = lohi
        mid = lo + jnp.right_shift(hi - lo, 1)
        thr = lax.bitcast_convert_type(mid, F32)
        cnt = jnp.sum((aff >= thr).astype(jnp.int32))
        ok = cnt >= cap
        return jnp.where(ok, mid, lo), jnp.where(ok, hi, mid)

    lo0 = jnp.zeros((1, 1), jnp.int32)
    hi0 = jnp.full((1, 1), 0x7F800000, jnp.int32)
    lo, hi = lax.fori_loop(0, 31, bisect, (lo0, hi0))
    above = aff >= lax.bitcast_convert_type(hi, F32)
    tie = (aff >= lax.bitcast_convert_type(lo, F32)) & jnp.logical_not(above)
    need = cap - jnp.sum(above.astype(jnp.int32))

    li = lax.broadcasted_iota(jnp.int32, (LANES, LANES), 0)
    lj = lax.broadcasted_iota(jnp.int32, (LANES, LANES), 1)
    ut_incl = (li <= lj).astype(BF16)
    ut_excl = (li < lj).astype(BF16)
    bi = lax.broadcasted_iota(jnp.int32, (nb, nb), 0)
    bj = lax.broadcasted_iota(jnp.int32, (nb, nb), 1)
    lt_incl = (bj <= bi).astype(BF16)
    lt_excl = (bj < bi).astype(BF16)

    def last(x):
        return x[:, LANES - 1:LANES]

    tie_b = tie.astype(BF16)
    tie_excl = jnp.dot(tie_b, ut_excl, preferred_element_type=F32)
    tie_incl = jnp.dot(tie_b, ut_incl, preferred_element_type=F32)
    tie_before = last(jnp.dot(lt_excl, tie_incl.astype(BF16), preferred_element_type=F32))
    tie_rank = (tie_before + tie_excl).astype(jnp.int32)
    sel = above | (tie & (tie_rank < need))

    lcum = jnp.dot(sel.astype(BF16), ut_incl, preferred_element_type=F32)
    lcum_b = lcum.astype(BF16)
    bcum_incl = last(jnp.dot(lt_incl, lcum_b, preferred_element_type=F32))
    bcum_excl = bcum_incl - last(lcum)
    pos_ref[0] = jnp.where(sel, (bcum_excl + lcum).astype(jnp.int32) - 1, -1)
    bex_ref[0] = bcum_excl.astype(jnp.int32)

    a1 = aff.astype(BF16).astype(F32)
    a2 = (aff - a1).astype(BF16).astype(F32)
    excl_i = bcum_excl.astype(jnp.int32)
    blk_i = lax.broadcasted_iota(jnp.int32, (nb, 1), 0)
    digits = [jnp.right_shift(excl_i, 7), excl_i & (LANES - 1), jnp.right_shift(blk_i, 7), blk_i & (LANES - 1)]
    side = jnp.concatenate([d.astype(F32) for d in digits] + [jnp.zeros((nb, LANES - len(digits)), F32)], axis=1)
    tables_t = jnp.concatenate([lcum, side, a1, a2], axis=1).T.astype(BF16)
    row = lax.broadcasted_iota(jnp.int32, (LANES, sc), 0)
    for c in range(cap // sc):
        slot = (c * sc + lax.broadcasted_iota(jnp.int32, (1, sc), 1)).astype(F32)
        hit = ((bcum_excl <= slot) & (slot < bcum_incl)).astype(BF16)
        got = jnp.dot(tables_t, hit, preferred_element_type=F32)
        lc_t = got[0:LANES]
        base = got[LANES:LANES + 1] * LANES + got[LANES + 1:LANES + 2]
        blk_of = (got[LANES + 2:LANES + 3] * LANES + got[LANES + 3:LANES + 4]).astype(jnp.int32)
        j_of = jnp.sum((lc_t <= slot - base).astype(jnp.int32), axis=0, keepdims=True)
        idx_ref[0, :, c * sc:(c + 1) * sc] = blk_of * LANES + j_of
        aff_t = got[2 * LANES:3 * LANES] + got[3 * LANES:4 * LANES]
        gate_ref[0, :, c * sc:(c + 1) * sc] = jnp.sum(jnp.where(row == j_of, aff_t, 0.0), axis=0, keepdims=True)


def _select(lg3, *, cap, sc):
    E, nb, _ = lg3.shape
    sc = min(sc, cap)
    per_e = lambda r, c, dt: (pl.BlockSpec((1, r, c), lambda e: (e, 0, 0)), jax.ShapeDtypeStruct((E, r, c), dt))
    specs = [per_e(1, cap, jnp.int32), per_e(1, cap, F32), per_e(nb, LANES, jnp.int32), per_e(nb, 1, jnp.int32)]
    return pl.pallas_call(
        functools.partial(_select_kernel, cap=cap, sc=sc),
        grid=(E,),
        in_specs=[_const_spec((E, nb, LANES))],
        out_specs=[s for s, _ in specs],
        out_shape=[o for _, o in specs],
        compiler_params=_cparams(1),
        name="select",
    )(lg3)


GATHER_UNROLL = 8


def _ffn_kernel(idx_ref, idxn_ref, gate_ref, h2s_ref, wg_ref, wu_ref, wd_ref, ye_ref, xbuf_ref, sem_ref):
    rows = ye_ref.shape[0]
    n = pl.program_id(0)
    n_steps = pl.num_programs(0)
    slot = n % 2

    def row_copy(ids_ref, r, to_slot):
        src = pl.multiple_of(ids_ref[0, 0, r] * ROW_SLABS, ROW_SLABS)
        return pltpu.make_async_copy(h2s_ref.at[pl.ds(src, ROW_SLABS), :],
                                     xbuf_ref.at[to_slot, pl.ds(r * ROW_SLABS, ROW_SLABS), :],
                                     sem_ref.at[to_slot])

    def start_gather(ids_ref, to_slot):
        def body(r0, carry):
            for u in range(GATHER_UNROLL):
                row_copy(ids_ref, r0 * GATHER_UNROLL + u, to_slot).start(priority=u % 2)
            return carry
        lax.fori_loop(0, rows // GATHER_UNROLL, body, 0)

    def wait_slot(s):
        pltpu.make_async_copy(h2s_ref.at[pl.ds(0, rows * ROW_SLABS), :], xbuf_ref.at[s], sem_ref.at[s]).wait()

    @pl.when(n == 0)
    def _():
        start_gather(idx_ref, 0)

    @pl.when(n + 1 < n_steps)
    def _():
        for r in range(rows):
            row_copy(idxn_ref, r, 1 - slot).start(priority=r % 2)

    wait_slot(slot)
    x = jnp.concatenate(
        [xbuf_ref[slot, pl.ds(s, rows, stride=ROW_SLABS), :].astype(BF16) for s in range(ROW_SLABS)], axis=1)
    a = jnp.dot(x, wg_ref[0], preferred_element_type=F32)
    u = jnp.dot(x, wu_ref[0], preferred_element_type=F32)
    hmid = (jax.nn.silu(a) * u).astype(BF16)
    y = jnp.dot(hmid, wd_ref[0], preferred_element_type=F32)
    gate_col = jnp.broadcast_to(gate_ref[0], (LANES, rows)).T[:, 0:1]
    ye_ref[...] = (gate_col * y).astype(BF16)


def _ffn(idx3, gate3, h2s, wg, wu, wd, *, cap):
    n_steps, _, rows = idx3.shape
    per_e = cap // rows
    last = n_steps - 1
    return pl.pallas_call(
        _ffn_kernel,
        grid=(n_steps,),
        in_specs=[pl.BlockSpec((1, 1, rows), lambda n: (n, 0, 0), memory_space=pltpu.SMEM),
                  pl.BlockSpec((1, 1, rows), lambda n: (jnp.minimum(n + 1, last), 0, 0), memory_space=pltpu.SMEM),
                  pl.BlockSpec((1, 1, rows), lambda n: (n, 0, 0)),
                  pl.BlockSpec(memory_space=pl.ANY),
                  pl.BlockSpec((1, D_MODEL, D_EXPERT), lambda n: (n // per_e, 0, 0)),
                  pl.BlockSpec((1, D_MODEL, D_EXPERT), lambda n: (n // per_e, 0, 0)),
                  pl.BlockSpec((1, D_EXPERT, D_MODEL), lambda n: (n // per_e, 0, 0))],
        out_specs=pl.BlockSpec((rows, D_MODEL), lambda n: (n, 0)),
        out_shape=jax.ShapeDtypeStruct((n_steps * rows, D_MODEL), BF16),
        scratch_shapes=[pltpu.VMEM((2, rows * ROW_SLABS, LANES), F32), pltpu.SemaphoreType.DMA((2,))],
        compiler_params=_cparams(1),
        name="ffn",
    )(idx3, idx3, gate3, h2s, wg, wu, wd)


BF16_ROWS = 16
MXU_DEPTH = 256


def _combine_kernel(tab_ref, x1_ref, pos_ref, p_ref, ye_ref, wpg_ref, wpp_ref, gp_ref, gfin_ref,
                    y_ref, win_ref, sem_ref, more_ref, msem_ref, acc_ref, *, cap, win):
    tt = x1_ref.shape[0]
    i = pl.program_id(0)
    n_tiles = pl.num_programs(0)
    slot = i % 2
    total = N_EXPERTS * cap
    per_dot = MXU_DEPTH // win
    wi = lax.broadcasted_iota(jnp.int32, (win, tt), 0)

    def first_row(e, tile):
        first = e * cap + tab_ref[e, tile]
        aligned = lax.shift_left(lax.shift_right_logical(first, 4), 4)
        return jnp.minimum(aligned, total - win)

    def fetch_windows(tile, to_slot):
        for e in range(N_EXPERTS):
            a0 = pl.multiple_of(first_row(e, tile), BF16_ROWS)
            pltpu.make_async_copy(ye_ref.at[pl.ds(a0, win), :], win_ref.at[to_slot, pl.ds(e * win, win), :],
                                  sem_ref.at[to_slot]).start()

    @pl.when(i == 0)
    def _():
        fetch_windows(0, 0)

    @pl.when(i + 1 < n_tiles)
    def _():
        fetch_windows(i + 1, 1 - slot)

    pltpu.make_async_copy(ye_ref.at[pl.ds(0, N_EXPERTS * win), :], win_ref.at[slot], sem_ref.at[slot]).wait()

    def onehot(e, a0):
        pos = pos_ref[pl.ds(e, 1), :]
        return (((pos + (e * cap - a0)) == wi) & (pos >= 0)).astype(BF16)

    def rows_from(e, a0):
        return e * cap + tab_ref[e, i + 1] - a0

    acc = None
    for j in range(N_EXPERTS // per_dot):
        es = range(j * per_dot, (j + 1) * per_dot)
        hot = jnp.concatenate([onehot(e, first_row(e, i)) for e in es], axis=0)
        part = lax.dot_general(hot, win_ref[slot, j * MXU_DEPTH:(j + 1) * MXU_DEPTH, :], _TN,
                               preferred_element_type=F32)
        acc = part if acc is None else acc + part
    acc_ref[...] = acc

    longest = rows_from(0, first_row(0, i))
    for e in range(1, N_EXPERTS):
        longest = jnp.maximum(longest, rows_from(e, first_row(e, i)))

    @pl.when(longest > win)
    def _():
        def per_expert(e, carry):
            a0 = first_row(e, i)
            n_windows = lax.shift_right_logical(rows_from(e, a0) + (win - 1), win.bit_length() - 1)

            def more(c, carry):
                a = pl.multiple_of(jnp.minimum(a0 + c * win, total - win), BF16_ROWS)
                copy = pltpu.make_async_copy(ye_ref.at[pl.ds(a, win), :], more_ref, msem_ref.at[0])
                copy.start()
                copy.wait()
                acc_ref[...] += lax.dot_general(onehot(e, a), more_ref[...], _TN, preferred_element_type=F32)
                return carry
            return lax.fori_loop(1, n_windows, more, carry)
        lax.fori_loop(0, N_EXPERTS, per_expert, 0)

    x2 = x1_ref[...] + acc_ref[...]
    gate = jax.nn.sigmoid(jnp.dot(_rms(x2, gp_ref[...]).astype(BF16), wpg_ref[...], preferred_element_type=F32))
    x3 = x2 + gate * jnp.dot(p_ref[...].astype(BF16), wpp_ref[...], preferred_element_type=F32)
    y_ref[...] = _rms(x3, gfin_ref[...])


def _combine(tab, x1, pos, p, ye, wpg, wpp, gp, gfin, *, cap, tt, win):
    N = x1.shape[0]
    grid_spec = pltpu.PrefetchScalarGridSpec(
        num_scalar_prefetch=1,
        grid=(N // tt,),
        in_specs=[pl.BlockSpec((tt, D_MODEL), lambda i, t: (i, 0)),
                  pl.BlockSpec((N_EXPERTS, tt), lambda i, t: (0, i)),
                  pl.BlockSpec((tt, D_PLE), lambda i, t: (i, 0)),
                  pl.BlockSpec(memory_space=pl.ANY),
                  pl.BlockSpec((D_MODEL, D_MODEL), lambda i, t: (0, 0)),
                  pl.BlockSpec((D_PLE, D_MODEL), lambda i, t: (0, 0)),
                  pl.BlockSpec((1, D_MODEL), lambda i, t: (0, 0)),
                  pl.BlockSpec((1, D_MODEL), lambda i, t: (0, 0))],
        out_specs=pl.BlockSpec((tt, D_MODEL), lambda i, t: (i, 0)),
        scratch_shapes=[pltpu.VMEM((2, N_EXPERTS * win, D_MODEL), BF16), pltpu.SemaphoreType.DMA((2,)),
                        pltpu.VMEM((win, D_MODEL), BF16), pltpu.SemaphoreType.DMA((1,)),
                        pltpu.VMEM((tt, D_MODEL), F32)],
    )
    return pl.pallas_call(
        functools.partial(_combine_kernel, cap=cap, win=win),
        grid_spec=grid_spec,
        out_shape=jax.ShapeDtypeStruct((N, D_MODEL), F32),
        compiler_params=_cparams(1),
        name="combine",
    )(tab, x1, pos, p, ye, wpg, wpp, gp, gfin)


def _t5_bucket(rel):
    nb = NUM_BUCKETS // 2
    ret = jnp.where(rel > 0, nb, 0)
    n = jnp.abs(rel)
    max_exact = nb // 2
    large = max_exact + (jnp.log(jnp.maximum(n, 1).astype(F32) / max_exact)
                         / math.log(MAX_DISTANCE / max_exact) * (nb - max_exact)).astype(jnp.int32)
    large = jnp.minimum(large, nb - 1)
    return ret + jnp.where(n < max_exact, n, large)


def _bucket_table(tq):
    kw = tq + 2 * WINDOW
    rel = jnp.arange(kw)[:, None] - WINDOW - jnp.arange(tq)[None, :]
    code = jnp.where(jnp.abs(rel) <= WINDOW, _t5_bucket(rel) + 1, 0).astype(jnp.int32)
    return code & (2 * NUM_BUCKETS - 1)


def _rope_tables(T):
    rows = T // GRID_W
    row = jnp.repeat(jnp.arange(rows), GRID_W).astype(F32)
    col = jnp.tile(jnp.arange(GRID_W), rows).astype(F32)
    freqs = ROPE_THETA ** (-jnp.arange(AX_PAIRS, dtype=F32) / AX_PAIRS)
    ang = jnp.concatenate([row[:, None] * freqs, col[:, None] * freqs], axis=-1)
    cos, sin = jnp.cos(ang), jnp.sin(ang)
    spare = jnp.zeros((T, LANES - HEAD_DIM), F32)
    return cos.T, sin.T, jnp.concatenate([cos, cos, spare], axis=1), jnp.concatenate([-sin, sin, spare], axis=1)


def _prep_params(g_attn, w_in, gq_b, gk_b, w_out, w_router, w_gate, w_up, w_down, w_ple_gate, w_ple_proj):
    w = w_in[0]
    o = np.cumsum([0, D_MIX, D_KV, D_KV, D_MIX, D_KV, D_KV])
    wqa, wka, wva, wqb, wkb, wvb = (w[:, o[i]:o[i + 1]] for i in range(6))
    halves = jnp.concatenate([jnp.arange(0, HEAD_DIM, 2), jnp.arange(1, HEAD_DIM, 2)])
    perm_q = (jnp.arange(N_HEADS)[:, None] * HEAD_DIM + halves[None, :]).reshape(-1)
    perm_k = (jnp.arange(N_KV)[:, None] * HEAD_DIM + halves[None, :]).reshape(-1)

    def spread(a):
        z = jnp.zeros(a.shape[:-1] + (LANES - HEAD_DIM,), a.dtype)
        return jnp.concatenate([a[..., :HEAD_DIM], z, a[..., HEAD_DIM:], z], axis=-1)

    w1 = jnp.concatenate([spread(wka), spread(wkb[:, perm_k])], axis=1).astype(BF16)
    w2t = jnp.concatenate([wqa.T, wqb[:, perm_q].T, wva.T, wvb.T], axis=0).astype(BF16)
    hd = jnp.arange(D_KX) // HEAD_DIM
    bd = (hd[:, None] == hd[None, :]).astype(BF16)
    gq = (gq_b[0] * QSCALE)[halves]
    gk = gk_b[0][halves]
    return dict(
        g_attn=g_attn[0][None], w1=w1, w2t=w2t, bd=bd,
        gq=jnp.broadcast_to(gq[:, None], (HEAD_DIM, LANES)),
        gk=spread(jnp.tile(gk, N_KV))[None],
        wo=w_out[0].astype(BF16), wr=w_router[0].T.astype(BF16),
        wg=w_gate[0].astype(BF16), wu=w_up[0].astype(BF16), wd=w_down[0].astype(BF16),
        wpg=w_ple_gate[0].astype(BF16), wpp=w_ple_proj[0].astype(BF16))


TOKEN_TILE = 512
WINDOW_Q_TILE = 256
DENSE_Q_TILE = 512
DENSE_KEY_CHUNK = 512
SELECT_SLOT_CHUNK = 1024
FFN_ROWS = 1024
COMBINE_TILE = 256
COMBINE_WINDOW = 128


def _trunk(x, p, prm, sink, rel_bias, g_out_a, g_out_b, g_ffn, g_ple, g_final):
    B, T, _ = x.shape
    N = B * T
    cap = CAPACITY_FACTOR * N // N_EXPERTS
    qat, qbt, vat, vbt, ka, kb = _qkv(x, prm, *_rope_tables(T), tn=TOKEN_TILE)
    oa = _attn_a(qat, ka, vat, sink, rel_bias, _bucket_table(min(WINDOW_Q_TILE, T)))
    ob = _attn_b(qbt, kb, vbt, tq=DENSE_Q_TILE, kc=DENSE_KEY_CHUNK)
    x1, h2s, lg = _post(oa.reshape(N, D_MIX), ob.reshape(N, D_MIX), x.reshape(N, D_MODEL),
                        g_out_a, g_out_b, prm["wo"], g_ffn, prm["wr"], tn=TOKEN_TILE)
    lg3 = lg.reshape(N_EXPERTS, N // LANES, LANES)
    idx, gate, pos, bex = _select(lg3, cap=cap, sc=SELECT_SLOT_CHUNK)
    rows = min(FFN_ROWS, cap)
    ye = _ffn(idx.reshape(-1, 1, rows), gate.reshape(-1, 1, rows), h2s, prm["wg"], prm["wu"], prm["wd"], cap=cap)
    tt = min(COMBINE_TILE, N)
    win = min(COMBINE_WINDOW, cap)
    tile_start = bex.reshape(N_EXPERTS, N // LANES)[:, ::tt // LANES]
    tab = jnp.concatenate([tile_start, jnp.full((N_EXPERTS, 1), cap, jnp.int32)], axis=1)
    y = _combine(tab, x1, pos.reshape(N_EXPERTS, N), p.reshape(N, D_PLE), ye, prm["wpg"], prm["wpp"],
                 g_ple, g_final, cap=cap, tt=tt, win=win)
    return y.reshape(B, T, D_MODEL)


def kernel(x_prompt, x_sample, p_prompt, p_sample, g_attn, w_in, sink_a, rel_bias, gq_b, gk_b, g_out_a, g_out_b, w_out, g_ffn, w_router, w_gate, w_up, w_down, g_ple, w_ple_gate, w_ple_proj, g_final):
    prm = _prep_params(g_attn, w_in, gq_b, gk_b, w_out, w_router, w_gate, w_up, w_down, w_ple_gate, w_ple_proj)
    args = (prm, sink_a[0], rel_bias, g_out_a[0][None], g_out_b[0][None], g_ffn[0][None], g_ple[0][None],
            g_final[None])
    y_prompt = _trunk(x_prompt, p_prompt[0], *args)
    y_sample = _trunk(x_sample, p_sample[0], *args)
    return (y_prompt, y_sample)
```

```python
import functools
import math

import jax
import jax.numpy as jnp
import numpy as np
from jax import lax
from jax.experimental import pallas as pl
from jax.experimental.pallas import tpu as pltpu

D_MODEL = 1024
HEAD_DIM = 64
N_HEADS = 8
N_KV = 2
GROUP = N_HEADS // N_KV
D_MIX = N_HEADS * HEAD_DIM
D_KV = N_KV * HEAD_DIM
D_GRP = GROUP * HEAD_DIM
WINDOW = 128
NUM_BUCKETS = 32
MAX_DISTANCE = 128
GRID_W = 64
ROPE_THETA = 10000.0
AX_PAIRS = HEAD_DIM // 4
N_EXPERTS = 16
CAPACITY_FACTOR = 2
D_EXPERT = 512
D_PLE = 256
EPS = 1e-6
NEG = -1e30
LOG2E = math.log2(math.e)
QSCALE = HEAD_DIM ** -0.5 * LOG2E

LANES = 128
ROW_SLABS = D_MODEL // LANES

F32 = jnp.float32
BF16 = jnp.bfloat16

_NT = (((1,), (1,)), ((), ()))
_TN = (((0,), (0,)), ((), ()))


VMEM_LIMIT_BYTES = 48 * 1024 * 1024


def _cparams(n_axes):
    return pltpu.CompilerParams(vmem_limit_bytes=VMEM_LIMIT_BYTES, dimension_semantics=("arbitrary",) * n_axes)


def _const_spec(shape):
    zeros = (0,) * len(shape)
    return pl.BlockSpec(shape, lambda *_: zeros)


def _rms(x, g):
    r = lax.rsqrt(jnp.mean(x * x, axis=-1, keepdims=True) + EPS)
    return (x * r) * g


def _split_bf16(x):
    hi = x.astype(BF16)
    lo = (x - hi.astype(F32)).astype(BF16)
    return hi, lo


_Q_ROWS = 2 * D_MIX
D_KX = N_KV * LANES
HALF = HEAD_DIM // 2


def _qkv_kernel(x_ref, g_ref, w1_ref, w2t_ref, bd_ref, gq_ref, gk_ref, ct_ref, st_ref, ck_ref, sk_ref,
                qat_ref, qbt_ref, vat_ref, vbt_ref, ka_ref, kb_ref):
    tn = x_ref.shape[1]
    h = _rms(x_ref[0], g_ref[...]).astype(BF16)
    p1 = jnp.dot(h, w1_ref[...], preferred_element_type=F32)
    p2 = lax.dot_general(w2t_ref[...], h, _NT, preferred_element_type=F32)

    qat_ref[0] = (p2[0:D_MIX] * QSCALE).astype(BF16)
    vat_ref[0] = p2[_Q_ROWS:_Q_ROWS + D_KV].astype(BF16)
    vbt_ref[0] = p2[_Q_ROWS + D_KV:_Q_ROWS + 2 * D_KV].astype(BF16)
    ka_ref[0] = p1[:, 0:D_KX].astype(BF16)

    q = p2[D_MIX:2 * D_MIX].reshape(N_HEADS, HEAD_DIM, tn)
    rq = lax.rsqrt(jnp.sum(q * q, axis=1, keepdims=True) * (1.0 / HEAD_DIM) + EPS)
    gq = jnp.concatenate([gq_ref[...]] * (tn // LANES), axis=1)[None]
    qn = (q * rq) * gq
    x0, x1 = qn[:, :HALF], qn[:, HALF:]
    c, s = ct_ref[...][None], st_ref[...][None]
    qbt_ref[0] = jnp.concatenate([x0 * c - x1 * s, x0 * s + x1 * c], axis=1).reshape(D_MIX, tn).astype(BF16)

    k = p1[:, D_KX:2 * D_KX]
    hi, lo = _split_bf16(k * k)
    ssq = (jnp.dot(hi, bd_ref[...], preferred_element_type=F32)
           + jnp.dot(lo, bd_ref[...], preferred_element_type=F32))
    kn = (k * lax.rsqrt(ssq * (1.0 / HEAD_DIM) + EPS)) * gk_ref[...]
    lane = lax.broadcasted_iota(jnp.int32, (tn, LANES), 1)
    outs = []
    for hk in range(N_KV):
        kh = kn[:, hk * LANES:(hk + 1) * LANES]
        partner = jnp.where(lane < HALF, pltpu.roll(kh, LANES - HALF, 1), pltpu.roll(kh, HALF, 1))
        outs.append(kh * ck_ref[...] + partner * sk_ref[...])
    kb_ref[0] = jnp.concatenate(outs, axis=1).astype(BF16)


def _qkv(x, prm, ct, st, ck, sk, *, tn):
    B, T, _ = x.shape
    tn = min(tn, T)
    grid = (B, T // tn)
    tok = lambda w: pl.BlockSpec((1, tn, w), lambda b, j: (b, j, 0))
    feat = lambda r: pl.BlockSpec((1, r, tn), lambda b, j: (b, 0, j))
    return pl.pallas_call(
        _qkv_kernel,
        grid=grid,
        in_specs=[tok(D_MODEL), _const_spec((1, D_MODEL)), _const_spec((D_MODEL, 2 * D_KX)),
                  _const_spec((_Q_ROWS + 2 * D_KV, D_MODEL)), _const_spec((D_KX, D_KX)),
                  _const_spec((HEAD_DIM, LANES)), _const_spec((1, D_KX)),
                  pl.BlockSpec((HALF, tn), lambda b, j: (0, j)),
                  pl.BlockSpec((HALF, tn), lambda b, j: (0, j)),
                  pl.BlockSpec((tn, LANES), lambda b, j: (j, 0)),
                  pl.BlockSpec((tn, LANES), lambda b, j: (j, 0))],
        out_specs=[feat(D_MIX), feat(D_MIX), feat(D_KV), feat(D_KV), tok(D_KX), tok(D_KX)],
        out_shape=[jax.ShapeDtypeStruct((B, D_MIX, T), BF16), jax.ShapeDtypeStruct((B, D_MIX, T), BF16),
                   jax.ShapeDtypeStruct((B, D_KV, T), BF16), jax.ShapeDtypeStruct((B, D_KV, T), BF16),
                   jax.ShapeDtypeStruct((B, T, D_KX), BF16), jax.ShapeDtypeStruct((B, T, D_KX), BF16)],
        compiler_params=_cparams(2),
        name="qkv",
    )(x, prm["g_attn"], prm["w1"], prm["w2t"], prm["bd"], prm["gq"], prm["gk"], ct, st, ck, sk)


ONES_ROWS = 16
V1_ROWS = HEAD_DIM + ONES_ROWS


def _head_weights(qt_ref, g, extra):
    return jnp.concatenate([qt_ref[0, g * HEAD_DIM:(g + 1) * HEAD_DIM, :], extra], axis=0)


def _normalised(acc):
    return acc[0:HEAD_DIM] * (1.0 / acc[HEAD_DIM:HEAD_DIM + 1])


def _attn_b_kernel(qt_ref, k_ref, vt_ref, o_ref, v1_ref, s0_ref, s1_ref, p0_ref, p1_ref, a0_ref, a1_ref,
                   acc_ref, m_ref, *, kc):
    s_refs, p_refs, a_refs = (s0_ref, s1_ref), (p0_ref, p1_ref), (a0_ref, a1_ref)
    T = k_ref.shape[1]
    tq = qt_ref.shape[2]
    n_chunks = T // kc

    @pl.when(pl.program_id(2) == 0)
    def _():
        v1_ref[0:HEAD_DIM, :] = vt_ref[0]
        v1_ref[HEAD_DIM:V1_ROWS, :] = jnp.ones((ONES_ROWS, T), BF16)

    acc_ref[...] = jnp.zeros(acc_ref.shape, F32)
    m_ref[...] = jnp.full(m_ref.shape, NEG, F32)
    spare = jnp.zeros((HEAD_DIM, tq), BF16)

    heads = range(GROUP)

    def scores(c, slot, hs=heads):
        kblk = k_ref[0, pl.ds(pl.multiple_of(c * kc, kc), kc), :]
        for g in hs:
            s_refs[slot][g] = jnp.dot(kblk, _head_weights(qt_ref, g, spare),
                                      preferred_element_type=F32).astype(BF16)

    def softmax(slot, hs=heads):
        for g in hs:
            s = s_refs[slot][g]
            m_old = m_ref[g]
            m_new = jnp.maximum(m_old, jnp.max(s, axis=0, keepdims=True).astype(F32))
            p_refs[slot][g] = jnp.exp2(s - m_new.astype(BF16))
            a_refs[slot][g] = jnp.exp2(m_old - m_new)
            m_ref[g] = m_new

    def values(c, slot, hs=heads):
        v1 = v1_ref[:, pl.ds(pl.multiple_of(c * kc, kc), kc)]
        for g in hs:
            acc_ref[g] = (a_refs[slot][g] * acc_ref[g]
                          + jnp.dot(v1, p_refs[slot][g], preferred_element_type=F32))

    scores(0, 0)
    softmax(0)
    scores(1, 1)

    def body(j, carry):
        c = 2 * j
        for g in heads:
            scores(c, 0, (g,))
            softmax(1, (g,))
            values(c - 2, 0, (g,))
        for g in heads:
            scores(c + 1, 1, (g,))
            softmax(0, (g,))
            values(c - 1, 1, (g,))
        return carry

    lax.fori_loop(1, n_chunks // 2, body, 0)
    values(n_chunks - 2, 0)
    softmax(1)
    values(n_chunks - 1, 1)
    o_ref[0] = jnp.concatenate([_normalised(acc_ref[g]) for g in range(GROUP)], axis=0).T


def _attn_b(qt, k, vt, *, tq, kc):
    B, _, T = qt.shape
    tq = min(tq, T)
    kc = min(kc, T // 2)
    assert T % (2 * kc) == 0
    grid = (B, N_KV, T // tq)
    return pl.pallas_call(
        functools.partial(_attn_b_kernel, kc=kc),
        grid=grid,
        in_specs=[pl.BlockSpec((1, D_GRP, tq), lambda b, h, i: (b, h, i)),
                  pl.BlockSpec((1, T, LANES), lambda b, h, i: (b, 0, h)),
                  pl.BlockSpec((1, HEAD_DIM, T), lambda b, h, i: (b, h, 0))],
        out_specs=pl.BlockSpec((1, tq, D_GRP), lambda b, h, i: (b, i, h)),
        out_shape=jax.ShapeDtypeStruct((B, T, D_MIX), F32),
        scratch_shapes=[pltpu.VMEM((V1_ROWS, T), BF16),
                        pltpu.VMEM((GROUP, kc, tq), BF16), pltpu.VMEM((GROUP, kc, tq), BF16),
                        pltpu.VMEM((GROUP, kc, tq), BF16), pltpu.VMEM((GROUP, kc, tq), BF16),
                        pltpu.VMEM((GROUP, 1, tq), F32), pltpu.VMEM((GROUP, 1, tq), F32),
                        pltpu.VMEM((GROUP, V1_ROWS, tq), F32),
                        pltpu.VMEM((GROUP, 1, tq), F32)],
        compiler_params=_cparams(3),
        name="attn_b",
    )(qt, k, vt)


HEADS_PER_DOT = 4


def _attn_a_kernel(sink_ref, rb_ref, bucket_ref, qt_ref, k_ref, vt_ref, o_ref, kp_ref, v1_ref, bias_ref):
    T = k_ref.shape[1]
    tq = qt_ref.shape[2]
    kw = tq + 2 * WINDOW
    qi = pl.program_id(1)

    @pl.when((pl.program_id(0) == 0) & (qi == 0))
    def _():
        lane = lax.broadcasted_iota(jnp.int32, (WINDOW, LANES), 1)
        off_seq = (lane == HEAD_DIM).astype(BF16)
        for kvh in range(N_KV):
            kp_ref[kvh, 0:WINDOW, :] = off_seq
            kp_ref[kvh, WINDOW + T:, :] = off_seq
            v1_ref[kvh, 0:HEAD_DIM, 0:WINDOW] = jnp.zeros((HEAD_DIM, WINDOW), BF16)
            v1_ref[kvh, 0:HEAD_DIM, WINDOW + T:] = jnp.zeros((HEAD_DIM, WINDOW), BF16)
            v1_ref[kvh, HEAD_DIM:V1_ROWS, :] = jnp.ones((ONES_ROWS, T + 2 * WINDOW), BF16)
        bucket = bucket_ref[...]
        for h in range(N_HEADS):
            bias = jnp.full((kw, tq), NEG, F32)
            for k in range(NUM_BUCKETS):
                bias = jnp.where(bucket == k + 1, rb_ref[k, h], bias)
            bias_ref[h] = bias * LOG2E

    @pl.when(qi == 0)
    def _():
        for kvh in range(N_KV):
            kp_ref[kvh, WINDOW:WINDOW + T, :] = k_ref[0, :, kvh * LANES:(kvh + 1) * LANES]
            v1_ref[kvh, 0:HEAD_DIM, WINDOW:WINDOW + T] = vt_ref[0, kvh * HEAD_DIM:(kvh + 1) * HEAD_DIM, :]

    start = pl.multiple_of(qi * tq, tq)
    spare_row = lax.broadcasted_iota(jnp.int32, (HEAD_DIM, tq), 0)
    spare = jnp.where(spare_row == 0, NEG, 0.0).astype(BF16)
    def scores(heads):
        kblk = kp_ref[heads[0] // GROUP, pl.ds(start, kw), :]
        w = jnp.concatenate([_head_weights(qt_ref, h, spare) for h in heads], axis=1)
        bias = jnp.concatenate([bias_ref[h] for h in heads], axis=1)
        return (jnp.dot(kblk, w, preferred_element_type=F32) + bias).astype(BF16)

    head_sets = [tuple(range(h, h + HEADS_PER_DOT)) for h in range(0, N_HEADS, HEADS_PER_DOT)]
    outs = []
    s_next = scores(head_sets[0])
    for i, heads in enumerate(head_sets):
        s = s_next
        if i + 1 < len(head_sets):
            s_next = scores(head_sets[i + 1])
        v1 = v1_ref[heads[0] // GROUP, :, pl.ds(start, kw)]
        sink = jnp.concatenate([jnp.full((1, tq), sink_ref[h], F32) for h in heads], axis=1) * LOG2E
        m = jnp.maximum(jnp.max(s, axis=0, keepdims=True).astype(F32), sink).astype(BF16)
        pv = jnp.dot(v1, jnp.exp2(s - m), preferred_element_type=F32)
        den = pv[HEAD_DIM:HEAD_DIM + 1] + jnp.exp2(sink - m.astype(F32))
        o = pv[0:HEAD_DIM] * (1.0 / den)
        outs.extend(o[:, j * tq:(j + 1) * tq] for j in range(len(heads)))
    o_ref[0] = jnp.concatenate(outs, axis=0).T


def _attn_a(qt, k, vt, sink, rel_bias, bucket):
    B, _, T = qt.shape
    kw, tq = bucket.shape
    grid = (B, T // tq)
    smem = pl.BlockSpec(memory_space=pltpu.SMEM)
    return pl.pallas_call(
        _attn_a_kernel,
        grid=grid,
        in_specs=[smem, smem, _const_spec((kw, tq)),
                  pl.BlockSpec((1, D_MIX, tq), lambda b, i: (b, 0, i)),
                  pl.BlockSpec((1, T, D_KX), lambda b, i: (b, 0, 0)),
                  pl.BlockSpec((1, D_KV, T), lambda b, i: (b, 0, 0))],
        out_specs=pl.BlockSpec((1, tq, D_MIX), lambda b, i: (b, i, 0)),
        out_shape=jax.ShapeDtypeStruct((B, T, D_MIX), F32),
        scratch_shapes=[pltpu.VMEM((N_KV, T + 2 * WINDOW, LANES), BF16),
                        pltpu.VMEM((N_KV, V1_ROWS, T + 2 * WINDOW), BF16),
                        pltpu.VMEM((N_HEADS, kw, tq), F32)],
        compiler_params=_cparams(2),
        name="attn_a",
    )(sink, rel_bias, bucket, qt, k, vt)


def _post_kernel(oa_ref, ob_ref, x_ref, ga_ref, gb_ref, wo_ref, gf_ref, wr_ref,
                 x1_ref, h2s_ref, lg_ref):
    tn = x_ref.shape[0]
    mixed = jnp.concatenate([_rms(oa_ref[...], ga_ref[...]), _rms(ob_ref[...], gb_ref[...])], axis=1)
    x1 = x_ref[...] + jnp.dot(mixed.astype(BF16), wo_ref[...], preferred_element_type=F32)
    x1_ref[...] = x1
    h2 = _rms(x1, gf_ref[...])
    for s in range(ROW_SLABS):
        h2s_ref[pl.ds(s, tn, stride=ROW_SLABS), :] = h2[:, s * LANES:(s + 1) * LANES]
    lg_ref[...] = lax.dot_general(wr_ref[...], h2.astype(BF16), _NT, preferred_element_type=F32)


def _post(oa, ob, x, ga, gb, wo, gf, wr, *, tn):
    N = x.shape[0]
    tn = min(tn, N)
    tok = lambda w: pl.BlockSpec((tn, w), lambda j: (j, 0))
    return pl.pallas_call(
        _post_kernel,
        grid=(N // tn,),
        in_specs=[tok(D_MIX), tok(D_MIX), tok(D_MODEL), _const_spec((1, D_MIX)), _const_spec((1, D_MIX)),
                  _const_spec((D_MODEL, D_MODEL)), _const_spec((1, D_MODEL)),
                  _const_spec((N_EXPERTS, D_MODEL))],
        out_specs=[tok(D_MODEL), pl.BlockSpec((ROW_SLABS * tn, LANES), lambda j: (j, 0)),
                   pl.BlockSpec((N_EXPERTS, tn), lambda j: (0, j))],
        out_shape=[jax.ShapeDtypeStruct((N, D_MODEL), F32),
                   jax.ShapeDtypeStruct((ROW_SLABS * N, LANES), F32),
                   jax.ShapeDtypeStruct((N_EXPERTS, N), F32)],
        compiler_params=_cparams(1),
        name="post",
    )(oa, ob, x, ga, gb, wo, gf, wr)


def _select_kernel(lg_ref, idx_ref, gate_ref, pos_ref, bex_ref, *, cap, sc):
    e = pl.program_id(0)
    nb = lg_ref.shape[1]
    lg = lg_ref[...]
    mx = jnp.max(lg, axis=0)
    den = jnp.sum(jnp.exp(lg - mx), axis=0)
    aff = jnp.exp(lg_ref[e] - mx) / den

    def bisect(_, lohi):
        lo, hi = lohi
        mid = lo + jnp.right_shift(hi - lo, 1)
        thr = lax.bitcast_convert_type(mid, F32)
        cnt = jnp.sum((aff >= thr).astype(jnp.int32))
        ok = cnt >= cap
        return jnp.where(ok, mid, lo), jnp.where(ok, hi, mid)

    lo0 = jnp.zeros((1, 1), jnp.int32)
    hi0 = jnp.full((1, 1), 0x7F800000, jnp.int32)
    lo, hi = lax.fori_loop(0, 31, bisect, (lo0, hi0))
    above = aff >= lax.bitcast_convert_type(hi, F32)
    tie = (aff >= lax.bitcast_convert_type(lo, F32)) & jnp.logical_not(above)
    need = cap - jnp.sum(above.astype(jnp.int32))

    li = lax.broadcasted_iota(jnp.int32, (LANES, LANES), 0)
    lj = lax.broadcasted_iota(jnp.int32, (LANES, LANES), 1)
    ut_incl = (li <= lj).astype(BF16)
    ut_excl = (li < lj).astype(BF16)
    bi = lax.broadcasted_iota(jnp.int32, (nb, nb), 0)
    bj = lax.broadcasted_iota(jnp.int32, (nb, nb), 1)
    lt_incl = (bj <= bi).astype(BF16)
    lt_excl = (bj < bi).astype(BF16)

    def last(x):
        return x[:, LANES - 1:LANES]

    tie_b = tie.astype(BF16)
    tie_excl = jnp.dot(tie_b, ut_excl, preferred_element_type=F32)
    tie_incl = jnp.dot(tie_b, ut_incl, preferred_element_type=F32)
    tie_before = last(jnp.dot(lt_excl, tie_incl.astype(BF16), preferred_element_type=F32))
    tie_rank = (tie_before + tie_excl).astype(jnp.int32)
    sel = above | (tie & (tie_rank < need))

    lcum = jnp.dot(sel.astype(BF16), ut_incl, preferred_element_type=F32)
    lcum_b = lcum.astype(BF16)
    bcum_incl = last(jnp.dot(lt_incl, lcum_b, preferred_element_type=F32))
    bcum_excl = bcum_incl - last(lcum)
    pos_ref[0] = jnp.where(sel, (bcum_excl + lcum).astype(jnp.int32) - 1, -1)
    bex_ref[0] = bcum_excl.astype(jnp.int32)

    a1 = aff.astype(BF16).astype(F32)
    a2 = (aff - a1).astype(BF16).astype(F32)
    excl_i = bcum_excl.astype(jnp.int32)
    blk_i = lax.broadcasted_iota(jnp.int32, (nb, 1), 0)
    digits = [jnp.right_shift(excl_i, 7), excl_i & (LANES - 1), jnp.right_shift(blk_i, 7), blk_i & (LANES - 1)]
    side = jnp.concatenate([d.astype(F32) for d in digits] + [jnp.zeros((nb, LANES - len(digits)), F32)], axis=1)
    tables_t = jnp.concatenate([lcum, side, a1, a2], axis=1).T.astype(BF16)
    row = lax.broadcasted_iota(jnp.int32, (LANES, sc), 0)
    for c in range(cap // sc):
        slot = (c * sc + lax.broadcasted_iota(jnp.int32, (1, sc), 1)).astype(F32)
        hit = ((bcum_excl <= slot) & (slot < bcum_incl)).astype(BF16)
        got = jnp.dot(tables_t, hit, preferred_element_type=F32)
        lc_t = got[0:LANES]
        base = got[LANES:LANES + 1] * LANES + got[LANES + 1:LANES + 2]
        blk_of = (got[LANES + 2:LANES + 3] * LANES + got[LANES + 3:LANES + 4]).astype(jnp.int32)
        j_of = jnp.sum((lc_t <= slot - base).astype(jnp.int32), axis=0, keepdims=True)
        idx_ref[0, :, c * sc:(c + 1) * sc] = blk_of * LANES + j_of
        aff_t = got[2 * LANES:3 * LANES] + got[3 * LANES:4 * LANES]
        gate_ref[0, :, c * sc:(c + 1) * sc] = jnp.sum(jnp.where(row == j_of, aff_t, 0.0), axis=0, keepdims=True)


def _select(lg3, *, cap, sc):
    E, nb, _ = lg3.shape
    sc = min(sc, cap)
    per_e = lambda r, c, dt: (pl.BlockSpec((1, r, c), lambda e: (e, 0, 0)), jax.ShapeDtypeStruct((E, r, c), dt))
    specs = [per_e(1, cap, jnp.int32), per_e(1, cap, F32), per_e(nb, LANES, jnp.int32), per_e(nb, 1, jnp.int32)]
    return pl.pallas_call(
        functools.partial(_select_kernel, cap=cap, sc=sc),
        grid=(E,),
        in_specs=[_const_spec((E, nb, LANES))],
        out_specs=[s for s, _ in specs],
        out_shape=[o for _, o in specs],
        compiler_params=_cparams(1),
        name="select",
    )(lg3)


GATHER_UNROLL = 8


def _ffn_kernel(idx_ref, idxn_ref, gate_ref, h2s_ref, wg_ref, wu_ref, wd_ref, ye_ref, xbuf_ref, sem_ref):
    rows = ye_ref.shape[0]
    n = pl.program_id(0)
    n_steps = pl.num_programs(0)
    slot = n % 2

    def row_copy(ids_ref, r, to_slot):
        src = pl.multiple_of(ids_ref[0, 0, r] * ROW_SLABS, ROW_SLABS)
        return pltpu.make_async_copy(h2s_ref.at[pl.ds(src, ROW_SLABS), :],
                                     xbuf_ref.at[to_slot, pl.ds(r * ROW_SLABS, ROW_SLABS), :],
                                     sem_ref.at[to_slot])

    def start_gather(ids_ref, to_slot):
        def body(r0, carry):
            for u in range(GATHER_UNROLL):
                row_copy(ids_ref, r0 * GATHER_UNROLL + u, to_slot).start(priority=u % 2)
            return carry
        lax.fori_loop(0, rows // GATHER_UNROLL, body, 0)

    def wait_slot(s):
        pltpu.make_async_copy(h2s_ref.at[pl.ds(0, rows * ROW_SLABS), :], xbuf_ref.at[s], sem_ref.at[s]).wait()

    @pl.when(n == 0)
    def _():
        start_gather(idx_ref, 0)

    @pl.when(n + 1 < n_steps)
    def _():
        for r in range(rows):
            row_copy(idxn_ref, r, 1 - slot).start(priority=r % 2)

    wait_slot(slot)
    x = jnp.concatenate(
        [xbuf_ref[slot, pl.ds(s, rows, stride=ROW_SLABS), :].astype(BF16) for s in range(ROW_SLABS)], axis=1)
    a = jnp.dot(x, wg_ref[0], preferred_element_type=F32)
    u = jnp.dot(x, wu_ref[0], preferred_element_type=F32)
    hmid = (jax.nn.silu(a) * u).astype(BF16)
    y = jnp.dot(hmid, wd_ref[0], preferred_element_type=F32)
    gate_col = jnp.broadcast_to(gate_ref[0], (LANES, rows)).T[:, 0:1]
    ye_ref[...] = (gate_col * y).astype(BF16)


def _ffn(idx3, gate3, h2s, wg, wu, wd, *, cap):
    n_steps, _, rows = idx3.shape
    per_e = cap // rows
    last = n_steps - 1
    return pl.pallas_call(
        _ffn_kernel,
        grid=(n_steps,),
        in_specs=[pl.BlockSpec((1, 1, rows), lambda n: (n, 0, 0), memory_space=pltpu.SMEM),
                  pl.BlockSpec((1, 1, rows), lambda n: (jnp.minimum(n + 1, last), 0, 0), memory_space=pltpu.SMEM),
                  pl.BlockSpec((1, 1, rows), lambda n: (n, 0, 0)),
                  pl.BlockSpec(memory_space=pl.ANY),
                  pl.BlockSpec((1, D_MODEL, D_EXPERT), lambda n: (n // per_e, 0, 0)),
                  pl.BlockSpec((1, D_MODEL, D_EXPERT), lambda n: (n // per_e, 0, 0)),
                  pl.BlockSpec((1, D_EXPERT, D_MODEL), lambda n: (n // per_e, 0, 0))],
        out_specs=pl.BlockSpec((rows, D_MODEL), lambda n: (n, 0)),
        out_shape=jax.ShapeDtypeStruct((n_steps * rows, D_MODEL), BF16),
        scratch_shapes=[pltpu.VMEM((2, rows * ROW_SLABS, LANES), F32), pltpu.SemaphoreType.DMA((2,))],
        compiler_params=_cparams(1),
        name="ffn",
    )(idx3, idx3, gate3, h2s, wg, wu, wd)


BF16_ROWS = 16
MXU_DEPTH = 256


def _combine_kernel(tab_ref, x1_ref, pos_ref, p_ref, ye_ref, wpg_ref, wpp_ref, gp_ref, gfin_ref,
                    y_ref, win_ref, sem_ref, more_ref, msem_ref, acc_ref, *, cap, win):
    tt = x1_ref.shape[0]
    i = pl.program_id(0)
    n_tiles = pl.num_programs(0)
    slot = i % 2
    total = N_EXPERTS * cap
    per_dot = MXU_DEPTH // win
    wi = lax.broadcasted_iota(jnp.int32, (win, tt), 0)

    def first_row(e, tile):
        first = e * cap + tab_ref[e, tile]
        aligned = lax.shift_left(lax.shift_right_logical(first, 4), 4)
        return jnp.minimum(aligned, total - win)

    def fetch_windows(tile, to_slot):
        for e in range(N_EXPERTS):
            a0 = pl.multiple_of(first_row(e, tile), BF16_ROWS)
            pltpu.make_async_copy(ye_ref.at[pl.ds(a0, win), :], win_ref.at[to_slot, pl.ds(e * win, win), :],
                                  sem_ref.at[to_slot]).start()

    @pl.when(i == 0)
    def _():
        fetch_windows(0, 0)

    @pl.when(i + 1 < n_tiles)
    def _():
        fetch_windows(i + 1, 1 - slot)

    pltpu.make_async_copy(ye_ref.at[pl.ds(0, N_EXPERTS * win), :], win_ref.at[slot], sem_ref.at[slot]).wait()

    def onehot(e, a0):
        pos = pos_ref[pl.ds(e, 1), :]
        return (((pos + (e * cap - a0)) == wi) & (pos >= 0)).astype(BF16)

    def rows_from(e, a0):
        return e * cap + tab_ref[e, i + 1] - a0

    acc = None
    for j in range(N_EXPERTS // per_dot):
        es = range(j * per_dot, (j + 1) * per_dot)
        hot = jnp.concatenate([onehot(e, first_row(e, i)) for e in es], axis=0)
        part = lax.dot_general(hot, win_ref[slot, j * MXU_DEPTH:(j + 1) * MXU_DEPTH, :], _TN,
                               preferred_element_type=F32)
        acc = part if acc is None else acc + part
    acc_ref[...] = acc

    longest = rows_from(0, first_row(0, i))
    for e in range(1, N_EXPERTS):
        longest = jnp.maximum(longest, rows_from(e, first_row(e, i)))

    @pl.when(longest > win)
    def _():
        def per_expert(e, carry):
            a0 = first_row(e, i)
            n_windows = lax.shift_right_logical(rows_from(e, a0) + (win - 1), win.bit_length() - 1)

            def more(c, carry):
                a = pl.multiple_of(jnp.minimum(a0 + c * win, total - win), BF16_ROWS)
                copy = pltpu.make_async_copy(ye_ref.at[pl.ds(a, win), :], more_ref, msem_ref.at[0])
                copy.start()
                copy.wait()
                acc_ref[...] += lax.dot_general(onehot(e, a), more_ref[...], _TN, preferred_element_type=F32)
                return carry
            return lax.fori_loop(1, n_windows, more, carry)
        lax.fori_loop(0, N_EXPERTS, per_expert, 0)

    x2 = x1_ref[...] + acc_ref[...]
    gate = jax.nn.sigmoid(jnp.dot(_rms(x2, gp_ref[...]).astype(BF16), wpg_ref[...], preferred_element_type=F32))
    x3 = x2 + gate * jnp.dot(p_ref[...].astype(BF16), wpp_ref[...], preferred_element_type=F32)
    y_ref[...] = _rms(x3, gfin_ref[...])


def _combine(tab, x1, pos, p, ye, wpg, wpp, gp, gfin, *, cap, tt, win):
    N = x1.shape[0]
    grid_spec = pltpu.PrefetchScalarGridSpec(
        num_scalar_prefetch=1,
        grid=(N // tt,),
        in_specs=[pl.BlockSpec((tt, D_MODEL), lambda i, t: (i, 0)),
                  pl.BlockSpec((N_EXPERTS, tt), lambda i, t: (0, i)),
                  pl.BlockSpec((tt, D_PLE), lambda i, t: (i, 0)),
                  pl.BlockSpec(memory_space=pl.ANY),
                  pl.BlockSpec((D_MODEL, D_MODEL), lambda i, t: (0, 0)),
                  pl.BlockSpec((D_PLE, D_MODEL), lambda i, t: (0, 0)),
                  pl.BlockSpec((1, D_MODEL), lambda i, t: (0, 0)),
                  pl.BlockSpec((1, D_MODEL), lambda i, t: (0, 0))],
        out_specs=pl.BlockSpec((tt, D_MODEL), lambda i, t: (i, 0)),
        scratch_shapes=[pltpu.VMEM((2, N_EXPERTS * win, D_MODEL), BF16), pltpu.SemaphoreType.DMA((2,)),
                        pltpu.VMEM((win, D_MODEL), BF16), pltpu.SemaphoreType.DMA((1,)),
                        pltpu.VMEM((tt, D_MODEL), F32)],
    )
    return pl.pallas_call(
        functools.partial(_combine_kernel, cap=cap, win=win),
        grid_spec=grid_spec,
        out_shape=jax.ShapeDtypeStruct((N, D_MODEL), F32),
        compiler_params=_cparams(1),
        name="combine",
    )(tab, x1, pos, p, ye, wpg, wpp, gp, gfin)


def _t5_bucket(rel):
    nb = NUM_BUCKETS // 2
    ret = jnp.where(rel > 0, nb, 0)
    n = jnp.abs(rel)
    max_exact = nb // 2
    large = max_exact + (jnp.log(jnp.maximum(n, 1).astype(F32) / max_exact)
                         / math.log(MAX_DISTANCE / max_exact) * (nb - max_exact)).astype(jnp.int32)
    large = jnp.minimum(large, nb - 1)
    return ret + jnp.where(n < max_exact, n, large)


def _bucket_table(tq):
    kw = tq + 2 * WINDOW
    rel = jnp.arange(kw)[:, None] - WINDOW - jnp.arange(tq)[None, :]
    code = jnp.where(jnp.abs(rel) <= WINDOW, _t5_bucket(rel) + 1, 0).astype(jnp.int32)
    return code & (2 * NUM_BUCKETS - 1)


def _rope_tables(T):
    rows = T // GRID_W
    row = jnp.repeat(jnp.arange(rows), GRID_W).astype(F32)
    col = jnp.tile(jnp.arange(GRID_W), rows).astype(F32)
    freqs = ROPE_THETA ** (-jnp.arange(AX_PAIRS, dtype=F32) / AX_PAIRS)
    ang = jnp.concatenate([row[:, None] * freqs, col[:, None] * freqs], axis=-1)
    cos, sin = jnp.cos(ang), jnp.sin(ang)
    spare = jnp.zeros((T, LANES - HEAD_DIM), F32)
    return cos.T, sin.T, jnp.concatenate([cos, cos, spare], axis=1), jnp.concatenate([-sin, sin, spare], axis=1)


def _prep_params(g_attn, w_in, gq_b, gk_b, w_out, w_router, w_gate, w_up, w_down, w_ple_gate, w_ple_proj):
    w = w_in[0]
    o = np.cumsum([0, D_MIX, D_KV, D_KV, D_MIX, D_KV, D_KV])
    wqa, wka, wva, wqb, wkb, wvb = (w[:, o[i]:o[i + 1]] for i in range(6))
    halves = jnp.concatenate([jnp.arange(0, HEAD_DIM, 2), jnp.arange(1, HEAD_DIM, 2)])
    perm_q = (jnp.arange(N_HEADS)[:, None] * HEAD_DIM + halves[None, :]).reshape(-1)
    perm_k = (jnp.arange(N_KV)[:, None] * HEAD_DIM + halves[None, :]).reshape(-1)

    def spread(a):
        z = jnp.zeros(a.shape[:-1] + (LANES - HEAD_DIM,), a.dtype)
        return jnp.concatenate([a[..., :HEAD_DIM], z, a[..., HEAD_DIM:], z], axis=-1)

    w1 = jnp.concatenate([spread(wka), spread(wkb[:, perm_k])], axis=1).astype(BF16)
    w2t = jnp.concatenate([wqa.T, wqb[:, perm_q].T, wva.T, wvb.T], axis=0).astype(BF16)
    hd = jnp.arange(D_KX) // HEAD_DIM
    bd = (hd[:, None] == hd[None, :]).astype(BF16)
    gq = (gq_b[0] * QSCALE)[halves]
    gk = gk_b[0][halves]
    return dict(
        g_attn=g_attn[0][None], w1=w1, w2t=w2t, bd=bd,
        gq=jnp.broadcast_to(gq[:, None], (HEAD_DIM, LANES)),
        gk=spread(jnp.tile(gk, N_KV))[None],
        wo=w_out[0].astype(BF16), wr=w_router[0].T.astype(BF16),
        wg=w_gate[0].astype(BF16), wu=w_up[0].astype(BF16), wd=w_down[0].astype(BF16),
        wpg=w_ple_gate[0].astype(BF16), wpp=w_ple_proj[0].astype(BF16))


TOKEN_TILE = 1024
WINDOW_Q_TILE = 256
DENSE_Q_TILE = 512
DENSE_KEY_CHUNK = 512
SELECT_SLOT_CHUNK = 1024
FFN_ROWS = 1024
COMBINE_TILE = 256
COMBINE_WINDOW = 128


def _trunk(x, p, prm, sink, rel_bias, g_out_a, g_out_b, g_ffn, g_ple, g_final):
    B, T, _ = x.shape
    N = B * T
    cap = CAPACITY_FACTOR * N // N_EXPERTS
    qat, qbt, vat, vbt, ka, kb = _qkv(x, prm, *_rope_tables(T), tn=TOKEN_TILE)
    oa = _attn_a(qat, ka, vat, sink, rel_bias, _bucket_table(min(WINDOW_Q_TILE, T)))
    ob = _attn_b(qbt, kb, vbt, tq=DENSE_Q_TILE, kc=DENSE_KEY_CHUNK)
    x1, h2s, lg = _post(oa.reshape(N, D_MIX), ob.reshape(N, D_MIX), x.reshape(N, D_MODEL),
                        g_out_a, g_out_b, prm["wo"], g_ffn, prm["wr"], tn=TOKEN_TILE)
    lg3 = lg.reshape(N_EXPERTS, N // LANES, LANES)
    idx, gate, pos, bex = _select(lg3, cap=cap, sc=SELECT_SLOT_CHUNK)
    rows = min(FFN_ROWS, cap)
    ye = _ffn(idx.reshape(-1, 1, rows), gate.reshape(-1, 1, rows), h2s, prm["wg"], prm["wu"], prm["wd"], cap=cap)
    tt = min(COMBINE_TILE, N)
    win = min(COMBINE_WINDOW, cap)
    tile_start = bex.reshape(N_EXPERTS, N // LANES)[:, ::tt // LANES]
    tab = jnp.concatenate([tile_start, jnp.full((N_EXPERTS, 1), cap, jnp.int32)], axis=1)
    y = _combine(tab, x1, pos.reshape(N_EXPERTS, N), p.reshape(N, D_PLE), ye, prm["wpg"], prm["wpp"],
                 g_ple, g_final, cap=cap, tt=tt, win=win)
    return y.reshape(B, T, D_MODEL)


def kernel(x_prompt, x_sample, p_prompt, p_sample, g_attn, w_in, sink_a, rel_bias, gq_b, gk_b, g_out_a, g_out_b, w_out, g_ffn, w_router, w_gate, w_up, w_down, g_ple, w_ple_gate, w_ple_proj, g_final):
    prm = _prep_params(g_attn, w_in, gq_b, gk_b, w_out, w_router, w_gate, w_up, w_down, w_ple_gate, w_ple_proj)
    args = (prm, sink_a[0], rel_bias, g_out_a[0][None], g_out_b[0][None], g_ffn[0][None], g_ple[0][None],
            g_final[None])
    y_prompt = _trunk(x_prompt, p_prompt[0], *args)
    y_sample = _trunk(x_sample, p_sample[0], *args)
    return (y_prompt, y_sample)
```
